```python
import math
import jax
import jax.numpy as jnp
from jax import lax
import numpy as np

D_MODEL = 1024
BATCH = 8
SEQ = 4096
DEPTH = 4

HEAD_DIM = 64
D_MIX = D_MODEL
H_RWKV = D_MIX // (4 * HEAD_DIM)
C_RWKV = H_RWKV * HEAD_DIM
H_FOX = (D_MIX - C_RWKV) // (2 * HEAD_DIM)
C_FOX = H_FOX * HEAD_DIM
H_NSA = (D_MIX - C_RWKV - C_FOX) // HEAD_DIM
C_NSA = H_NSA * HEAD_DIM
G_NSA = 2
HPG = H_NSA // G_NSA
C_KV_NSA = G_NSA * HEAD_DIM
R_DECAY = 32
R_AAA = 32
R_GATE = 64
L_CMP = 32
D_CMP = 16
CMP_HID = 128
L_SLC = 64
N_SLC = 16
N_LOCAL = 2
WINDOW = 512
Q_BLOCK = 128
SLC_Q_CHUNK = 64
NUM_BUCKETS = 32
MAX_DISTANCE = 128
D_FF = 2816
CONV_W = 3
RMS_EPS = 1e-6
GN_EPS = 64e-5
NEG_INF = -1e30
FORCE_SCORE = 1e4

RWKV_COLS = (C_RWKV, C_RWKV, C_RWKV, R_DECAY, R_AAA, R_GATE)
FOX_COLS = (C_FOX, C_FOX, C_FOX, H_FOX)
NSA_COLS = (C_NSA,) + (C_KV_NSA,) * 6 + (3 * H_NSA,)
N_RWKV_IN = sum(RWKV_COLS)
N_FOX_IN = sum(FOX_COLS)
N_NSA_IN = sum(NSA_COLS)
N_IN = N_RWKV_IN + N_FOX_IN + N_NSA_IN

kernel_name = "hybrid_rwkv7_fox_nsa_convffn"


def rmsnorm(x, g):
    xf = x.astype(jnp.float32)
    y = xf * lax.rsqrt(jnp.mean(xf * xf, axis=-1, keepdims=True) + RMS_EPS)
    return (y * g.astype(jnp.float32)).astype(x.dtype)


def split_cols(p, sizes):
    idx = np.cumsum(np.array(sizes))[:-1].tolist()
    return jnp.split(p, idx, axis=-1)


def shift_right(x):
    return jnp.pad(x, ((0, 0), (1, 0), (0, 0)))[:, : x.shape[1]]


def masked_softmax(logits, mask):
    p = jax.nn.softmax(jnp.where(mask, logits, NEG_INF), axis=-1)
    return jnp.where(mask, p, 0.0)


def t5_bucket(dist):
    n = jnp.maximum(dist, 0)
    max_exact = NUM_BUCKETS // 2
    nf = jnp.maximum(n, 1).astype(jnp.float32)
    large = max_exact + (jnp.log(nf / max_exact) / math.log(MAX_DISTANCE / max_exact)
                         * (NUM_BUCKETS - max_exact)).astype(jnp.int32)
    large = jnp.minimum(large, NUM_BUCKETS - 1)
    return jnp.where(n < max_exact, n, large)


def wkv7_scan(r, w, k, v, a, b):
    B, S, H, N = r.shape

    def step(state, inp):
        r_t, w_t, k_t, v_t, a_t, b_t = inp
        sa = jnp.einsum('bhvk,bhk->bhv', state, a_t)
        state = (state * w_t[:, :, None, :] + sa[..., None] * b_t[:, :, None, :]
                 + v_t[..., None] * k_t[:, :, None, :])
        return state, jnp.einsum('bhvk,bhk->bhv', state, r_t)

    xs = tuple(jnp.moveaxis(t, 1, 0) for t in (r, w, k, v, a, b))
    state0 = jnp.zeros((B, H, N, N), jnp.float32)
    _, ys = lax.scan(step, state0, xs)
    return jnp.moveaxis(ys, 0, 1)


def rwkv7_mix(p, mu, w0, w_up, a0, a_up, g_up, k_k, k_a, r_k, ln_w, ln_b):
    B, S, _ = p.shape
    dt = p.dtype
    p = p + (shift_right(p) - p) * mu
    r, k, v, xw, xa, xg = split_cols(p.astype(jnp.float32), RWKV_COLS)
    w = -jax.nn.softplus(-(w0 + jnp.tanh(xw) @ w_up)) - 0.5
    decay = jnp.exp(-jnp.exp(w))
    a = jax.nn.sigmoid(a0 + xa @ a_up)
    g = jax.nn.sigmoid(xg) @ g_up
    heads = lambda t: t.reshape(B, S, H_RWKV, HEAD_DIM)
    kk = heads(k * k_k)
    kk = kk / jnp.maximum(jnp.linalg.norm(kk, axis=-1, keepdims=True), 1e-12)
    k = k * (1.0 + (a - 1.0) * k_a)
    r_h, k_h, v_h, a_h = heads(r), heads(k), heads(v), heads(a)
    y = wkv7_scan(r_h, heads(decay), k_h, v_h, -kk, kk * a_h)
    mean = jnp.mean(y, axis=-1, keepdims=True)
    var = jnp.mean(jnp.square(y - mean), axis=-1, keepdims=True)
    yn = ((y - mean) * lax.rsqrt(var + GN_EPS)).reshape(B, S, C_RWKV) * ln_w + ln_b
    bonus = jnp.sum(r_h * k_h * r_k, axis=-1, keepdims=True) * v_h
    return ((yn + bonus.reshape(B, S, C_RWKV)) * g).astype(dt)


def fox_mix(p, b_f):
    B, S, _ = p.shape
    q, k, v, f_logit = split_cols(p, FOX_COLS)
    to_heads = lambda t: t.reshape(B, S, H_FOX, HEAD_DIM).transpose(0, 2, 1, 3)
    q, k, v = to_heads(q), to_heads(k), to_heads(v)
    log_f = jax.nn.log_sigmoid((f_logit + b_f).astype(jnp.float32))
    cum = jnp.cumsum(log_f, axis=1).transpose(0, 2, 1)
    scale = HEAD_DIM ** -0.5
    kpos = jnp.arange(S)

    def block(i):
        q0 = i * Q_BLOCK
        qb = lax.dynamic_slice_in_dim(q, q0, Q_BLOCK, axis=2)
        cb = lax.dynamic_slice_in_dim(cum, q0, Q_BLOCK, axis=2)
        logits = (jnp.einsum('bhqd,bhkd->bhqk', qb, k).astype(jnp.float32) * scale
                  + cb[..., None] - cum[:, :, None, :])
        qpos = q0 + jnp.arange(Q_BLOCK)
        probs = masked_softmax(logits, kpos[None, :] <= qpos[:, None])
        return jnp.einsum('bhqk,bhkd->bhqd', probs.astype(v.dtype), v)

    out = lax.map(block, jnp.arange(S // Q_BLOCK))
    return out.transpose(1, 0, 3, 2, 4).reshape(B, S, C_FOX)


def nsa_mix(p, pe_k, pe_v, ck_w1, ck_w2, cv_w1, cv_w2, rel_bias):
    B, S, _ = p.shape
    dt = p.dtype
    q, kc, vc, ks, vs, kw, vw, gate_logit = split_cols(p, NSA_COLS)
    q = q.reshape(B, S, G_NSA, HPG, HEAD_DIM)
    kvh = lambda t: t.reshape(B, S, G_NSA, HEAD_DIM)
    kc, vc, ks, vs, kw, vw = (kvh(t) for t in (kc, vc, ks, vs, kw, vw))
    scale = HEAD_DIM ** -0.5
    tpos = jnp.arange(S)

    n_ch = S // D_CMP
    ov = L_CMP // D_CMP
    n_cmp = n_ch - ov + 1

    def compress(t, pe, w1, w2):
        ch = t.reshape(B, n_ch, D_CMP, G_NSA, HEAD_DIM)
        blocks = jnp.concatenate([ch[:, j:j + n_cmp] for j in range(ov)], axis=2)
        blocks = blocks + pe[:, None, :]
        flat = blocks.transpose(0, 1, 3, 2, 4).reshape(B, n_cmp, G_NSA, L_CMP * HEAD_DIM)
        return jax.nn.gelu(flat @ w1) @ w2

    k_cmp = compress(kc, pe_k, ck_w1, ck_w2)
    v_cmp = compress(vc, pe_v, cv_w1, cv_w2)
    cmp_start = jnp.arange(n_cmp) * D_CMP
    cmp_end = cmp_start + L_CMP - 1
    dist_c = tpos[:, None] - cmp_end[None, :]
    bias_c = rel_bias[t5_bucket(dist_c)].reshape(S, n_cmp, G_NSA, HPG).transpose(2, 3, 0, 1)
    logits_c = jnp.einsum('bsghd,bcgd->bghsc', q, k_cmp).astype(jnp.float32) * scale + bias_c
    p_cmp = masked_softmax(logits_c, dist_c >= 0)
    o_cmp = jnp.einsum('bghsc,bcgd->bsghd', p_cmp.astype(dt), v_cmp)

    nsb = S // L_SLC
    slc_start = jnp.arange(nsb) * L_SLC
    overlap = ((cmp_start[:, None] <= slc_start[None, :] + L_SLC - 1)
               & (cmp_end[:, None] >= slc_start[None, :])).astype(jnp.float32)
    imp = jnp.einsum('bghsc,cj->bgsj', p_cmp, overlap)
    cur = tpos // L_SLC
    jb = jnp.arange(nsb)
    back = cur[:, None] - jb[None, :]
    valid = back >= 0
    forced = (jb[None, :] == 0) | (valid & (back < N_LOCAL))
    score = jnp.where(valid, jnp.where(forced, FORCE_SCORE, imp), -1.0)
    n_sel = min(N_SLC, nsb)
    top_val, top_idx = lax.top_k(score, n_sel)
    sel_ok = top_val >= 0.0

    kb = ks.reshape(B, nsb, L_SLC, G_NSA, HEAD_DIM).transpose(0, 3, 1, 2, 4)
    vb = vs.reshape(B, nsb, L_SLC, G_NSA, HEAD_DIM).transpose(0, 3, 1, 2, 4)
    gather = jax.vmap(jax.vmap(lambda blocks, idx: blocks[idx]))
    qg = q.transpose(0, 2, 1, 3, 4)
    tbl_g = rel_bias.reshape(NUM_BUCKETS, G_NSA, HPG).transpose(1, 0, 2)
    g_idx = jnp.arange(G_NSA)[None, :, None, None, None]

    def slc_chunk(i):
        q0 = i * SLC_Q_CHUNK
        qc = lax.dynamic_slice_in_dim(qg, q0, SLC_Q_CHUNK, axis=2)
        ic = lax.dynamic_slice_in_dim(top_idx, q0, SLC_Q_CHUNK, axis=2)
        okc = lax.dynamic_slice_in_dim(sel_ok, q0, SLC_Q_CHUNK, axis=2)
        kg = gather(kb, ic)
        vg = gather(vb, ic)
        qpos = q0 + jnp.arange(SLC_Q_CHUNK)
        kpos = ic[..., None] * L_SLC + jnp.arange(L_SLC)
        dist = qpos[None, None, :, None, None] - kpos
        mask = (okc[..., None] & (dist >= 0))[:, :, :, None]
        bias = tbl_g[g_idx, t5_bucket(dist)].transpose(0, 1, 2, 5, 3, 4)
        logits = jnp.einsum('bgqhd,bgqnld->bgqhnl', qc, kg).astype(jnp.float32) * scale + bias
        shp = logits.shape
        flat_mask = jnp.broadcast_to(mask, shp).reshape(shp[:4] + (n_sel * L_SLC,))
        probs = masked_softmax(logits.reshape(shp[:4] + (n_sel * L_SLC,)), flat_mask).reshape(shp)
        return jnp.einsum('bgqhnl,bgqnld->bgqhd', probs.astype(dt), vg)

    o_slc = lax.map(slc_chunk, jnp.arange(S // SLC_Q_CHUNK))
    o_slc = o_slc.transpose(1, 0, 3, 2, 4, 5).reshape(B, S, G_NSA, HPG, HEAD_DIM)

    kw_pad = jnp.pad(kw, ((0, 0), (WINDOW, 0), (0, 0), (0, 0)))
    vw_pad = jnp.pad(vw, ((0, 0), (WINDOW, 0), (0, 0), (0, 0)))
    n_keys = Q_BLOCK + WINDOW

    def win_block(i):
        q0 = i * Q_BLOCK
        qb = lax.dynamic_slice_in_dim(q, q0, Q_BLOCK, axis=1)
        kblk = lax.dynamic_slice_in_dim(kw_pad, q0, n_keys, axis=1)
        vblk = lax.dynamic_slice_in_dim(vw_pad, q0, n_keys, axis=1)
        qpos = q0 + jnp.arange(Q_BLOCK)
        kpos = q0 - WINDOW + jnp.arange(n_keys)
        dist = qpos[:, None] - kpos[None, :]
        mask = (dist >= 0) & (dist < WINDOW) & (kpos[None, :] >= 0)
        bias = rel_bias[t5_bucket(dist)].reshape(Q_BLOCK, n_keys, G_NSA, HPG).transpose(2, 3, 0, 1)
        logits = jnp.einsum('bqghd,bkgd->bghqk', qb, kblk).astype(jnp.float32) * scale + bias
        probs = masked_softmax(logits, mask)
        return jnp.einsum('bghqk,bkgd->bqghd', probs.astype(dt), vblk)

    o_win = lax.map(win_block, jnp.arange(S // Q_BLOCK))
    o_win = o_win.transpose(1, 0, 2, 3, 4, 5).reshape(B, S, G_NSA, HPG, HEAD_DIM)

    gates = jax.nn.sigmoid(gate_logit).reshape(B, S, G_NSA, HPG, 3)
    o = gates[..., 0:1] * o_cmp + gates[..., 1:2] * o_slc + gates[..., 2:3] * o_win
    return o.reshape(B, S, C_NSA)


def conv_ffn(h, w_up, conv_w, conv_b, w_down):
    S = h.shape[1]
    u = h @ w_up
    u_pad = jnp.pad(u, ((0, 0), (CONV_W - 1, 0), (0, 0)))
    u = conv_b + sum(conv_w[j] * u_pad[:, j:j + S] for j in range(CONV_W))
    gate, val = jnp.split(u, 2, axis=-1)
    return (jax.nn.silu(gate) * val) @ w_down


def setup_inputs(seed: int = 0) -> dict:
    key = jax.random.key(seed)
    ks = iter(jax.random.split(key, 40))

    def nrm(shape, scale):
        return scale * jax.random.normal(next(ks), shape, jnp.float32)

    L, D = DEPTH, D_MODEL
    return {
        "x": nrm((BATCH, SEQ, D), 1.0),
        "c": nrm((BATCH, D), 1.0),
        "ada_w": nrm((L, D, 6 * D), 0.5 * D ** -0.5),
        "ada_b": nrm((L, 6 * D), 0.1),
        "norm_g": 1.0 + nrm((L, 4, D), 0.02),
        "w_in": nrm((L, D, N_IN), D ** -0.5),
        "rwkv_mu": 0.5 + nrm((L, N_RWKV_IN), 0.1),
        "rwkv_w0": nrm((L, C_RWKV), 0.5),
        "rwkv_w_up": nrm((L, R_DECAY, C_RWKV), 0.5 * R_DECAY ** -0.5),
        "rwkv_a0": nrm((L, C_RWKV), 0.3),
        "rwkv_a_up": nrm((L, R_AAA, C_RWKV), 0.5 * R_AAA ** -0.5),
        "rwkv_g_up": nrm((L, R_GATE, C_RWKV), R_GATE ** -0.5),
        "rwkv_k_k": 0.85 + nrm((L, C_RWKV), 0.05),
        "rwkv_k_a": 1.0 + nrm((L, C_RWKV), 0.05),
        "rwkv_r_k": nrm((L, H_RWKV, HEAD_DIM), 0.1),
        "rwkv_ln_w": 1.0 + nrm((L, C_RWKV), 0.02),
        "rwkv_ln_b": nrm((L, C_RWKV), 0.02),
        "fox_b_f": 3.0 + nrm((L, H_FOX), 0.5),
        "nsa_pe_k": nrm((L, L_CMP, HEAD_DIM), 0.1),
        "nsa_pe_v": nrm((L, L_CMP, HEAD_DIM), 0.1),
        "nsa_ck_w1": nrm((L, L_CMP * HEAD_DIM, CMP_HID), (L_CMP * HEAD_DIM) ** -0.5),
        "nsa_ck_w2": nrm((L, CMP_HID, HEAD_DIM), CMP_HID ** -0.5),
        "nsa_cv_w1": nrm((L, L_CMP * HEAD_DIM, CMP_HID), (L_CMP * HEAD_DIM) ** -0.5),
        "nsa_cv_w2": nrm((L, CMP_HID, HEAD_DIM), CMP_HID ** -0.5),
        "rel_bias": nrm((NUM_BUCKETS, H_NSA), 0.5),
        "w_out": nrm((L, D_MIX, D), D_MIX ** -0.5),
        "ffn_up": nrm((L, D, 2 * D_FF), D ** -0.5),
        "ffn_conv_w": nrm((L, CONV_W, 2 * D_FF), 0.3).at[:, -1].add(1.0),
        "ffn_conv_b": nrm((L, 2 * D_FF), 0.02),
        "ffn_down": nrm((L, D_FF, D), D_FF ** -0.5),
    }


def reference(x, c, ada_w, ada_b, norm_g, w_in, rwkv_mu, rwkv_w0, rwkv_w_up, rwkv_a0,
              rwkv_a_up, rwkv_g_up, rwkv_k_k, rwkv_k_a, rwkv_r_k, rwkv_ln_w, rwkv_ln_b,
              fox_b_f, nsa_pe_k, nsa_pe_v, nsa_ck_w1, nsa_ck_w2, nsa_cv_w1, nsa_cv_w2,
              rel_bias, w_out, ffn_up, ffn_conv_w, ffn_conv_b, ffn_down):
    for l in range(DEPTH):
        mod = jax.nn.silu(c) @ ada_w[l] + ada_b[l]
        sh_m, sc_m, g_m, sh_f, sc_f, g_f = jnp.split(mod[:, None, :], 6, axis=-1)

        h = rmsnorm(x, norm_g[l, 0]) * (1.0 + sc_m) + sh_m
        p = h @ w_in[l]
        p_a, p_b, p_c = split_cols(p, (N_RWKV_IN, N_FOX_IN, N_NSA_IN))
        y_a = rwkv7_mix(p_a, rwkv_mu[l], rwkv_w0[l], rwkv_w_up[l], rwkv_a0[l], rwkv_a_up[l],
                        rwkv_g_up[l], rwkv_k_k[l], rwkv_k_a[l], rwkv_r_k[l],
                        rwkv_ln_w[l], rwkv_ln_b[l])
        y_b = fox_mix(p_b, fox_b_f[l])
        y_c = nsa_mix(p_c, nsa_pe_k[l], nsa_pe_v[l], nsa_ck_w1[l], nsa_ck_w2[l],
                      nsa_cv_w1[l], nsa_cv_w2[l], rel_bias)
        y = jnp.concatenate([y_a, y_b, y_c], axis=-1) @ w_out[l]
        x = x + g_m * rmsnorm(y, norm_g[l, 1])

        h = rmsnorm(x, norm_g[l, 2]) * (1.0 + sc_f) + sh_f
        f = conv_ffn(h, ffn_up[l], ffn_conv_w[l], ffn_conv_b[l], ffn_down[l])
        x = x + g_f * rmsnorm(f, norm_g[l, 3])
    return x
```

```python
import functools
import math

import numpy as np
import jax
import jax.numpy as jnp
from jax import lax
from jax.experimental import pallas as pl
from jax.experimental.pallas import tpu as pltpu

F32 = jnp.float32
BF16 = jnp.bfloat16

HEAD_DIM = 64
H_RWKV = 4
C_RWKV = H_RWKV * HEAD_DIM
H_FOX = 6
C_FOX = H_FOX * HEAD_DIM
H_NSA = 6
C_NSA = H_NSA * HEAD_DIM
G_NSA = 2
HPG = H_NSA // G_NSA
R_DECAY = 32
R_AAA = 32
R_GATE = 64
L_CMP = 32
D_CMP = 16
CMP_HID = 128
L_SLC = 64
N_SLC = 16
N_LOCAL = 2
WINDOW = 512
NUM_BUCKETS = 32
MAX_DISTANCE = 128
CONV_W = 3
RMS_EPS = 1e-6
GN_EPS = 64e-5
NEG_INF = -1e30
FORCE_SCORE = 1e4

LANE = 128
CHUNK = 64
VMEM_LIMIT = 56 * 1024 * 1024

PF_R, PF_K, PF_V, PF_XW, PF_XA, PF_XG, PF_MISC = 0, 256, 512, 768, 896, 1024, 1152
PF_RWKV = 1152
PF_COLS = 1280
MISC_F = 0
MISC_GATE = 8
PB_FQ, PB_FK, PB_NQ = 0, 768, 1536
PB_KS, PB_VS, PB_KW, PB_VW = 2304, 2560, 2816, 3072
PB_FV, PB_KC, PB_VC = 3328, 3712, 3840
PB_COLS = 3968


def _bucket_lower_bounds():
    n = np.arange(0, 4 * MAX_DISTANCE, dtype=np.int64)
    max_exact = NUM_BUCKETS // 2
    nf = np.maximum(n, 1).astype(np.float32)
    large = max_exact + (np.log(nf / np.float32(max_exact)) / np.float32(math.log(MAX_DISTANCE / max_exact))
                         * np.float32(NUM_BUCKETS - max_exact)).astype(np.int32)
    large = np.minimum(large, NUM_BUCKETS - 1)
    bucket = np.where(n < max_exact, n, large)
    return [int(np.argmax(bucket >= b)) for b in range(NUM_BUCKETS)]


BUCKET_LB = _bucket_lower_bounds()
BIAS_CONST_DIST = BUCKET_LB[NUM_BUCKETS - 1]


def _cparams(sem, vmem=None):
    return pltpu.CompilerParams(dimension_semantics=sem, vmem_limit_bytes=vmem or VMEM_LIMIT)


def _dot(a, b):
    return jnp.dot(a, b, preferred_element_type=F32)


def _dot_nt(a, b):
    return lax.dot_general(a, b, (((1,), (1,)), ((), ())), preferred_element_type=F32)


def _split_bf16(x, n):
    parts, r = [], x
    for i in range(n):
        p = r.astype(BF16)
        parts.append(p)
        if i + 1 < n:
            r = r - p.astype(F32)
    return parts


def _dot_hp(a, b, nt=False):
    f = _dot_nt if nt else _dot
    ah, al = _split_bf16(a, 2)
    bh, bl = _split_bf16(b, 2)
    return f(ah, bh) + (f(ah, bl) + f(al, bh))


def _dot_exact_lhs(a_bf16, b, n):
    out = None
    for p in _split_bf16(b, n):
        t = _dot(a_bf16, p)
        out = t if out is None else out + t
    return out


def _softplus(x):
    return jnp.maximum(x, 0.0) + jnp.log(1.0 + jnp.exp(-jnp.abs(x)))


def _sigmoid(x):
    return 1.0 / (1.0 + jnp.exp(-x))


def _rmsnorm(x, g):
    return x * lax.rsqrt(jnp.mean(x * x, axis=-1, keepdims=True) + RMS_EPS) * g


def _mod_kernel(c_ref, w_ref, b_ref, o_ref):
    c = c_ref[...]
    s = (c * _sigmoid(c)).astype(BF16)
    o_ref[0] = _dot(s, w_ref[0].astype(BF16)) + b_ref[0]


def _mod_call(c, ada_w, ada_b):
    L, D, N = ada_w.shape
    B = c.shape[0]
    tn = 1536
    return pl.pallas_call(
        _mod_kernel,
        out_shape=jax.ShapeDtypeStruct((L, B, N), F32),
        grid=(L, N // tn),
        in_specs=[pl.BlockSpec((B, D), lambda l, j: (0, 0)),
                  pl.BlockSpec((1, D, tn), lambda l, j: (l, 0, j)),
                  pl.BlockSpec((1, 1, tn), lambda l, j: (l, 0, j))],
        out_specs=pl.BlockSpec((1, B, tn), lambda l, j: (l, 0, j)),
        compiler_params=_cparams(("arbitrary", "arbitrary")),
        name="adaln_mod",
    )(c, ada_w, ada_b.reshape(L, 1, N))


def _inproj_kernel(x_ref, sc_ref, sh_ref, g_ref, w_ref, o_ref, *, n_chunk):
    h = _rmsnorm(x_ref[...], g_ref[...]) * (1.0 + sc_ref[0]) + sh_ref[0]
    h = h.astype(BF16)
    n = o_ref.shape[1]
    for n0 in range(0, n, n_chunk):
        n1 = min(n, n0 + n_chunk)
        o_ref[:, n0:n1] = _dot(h, w_ref[:, n0:n1]).astype(o_ref.dtype)


def _inproj_call(x2d, sc, sh, g, w, out_dtype, S, tm=512):
    N, D = x2d.shape
    C = w.shape[1]
    per = S // tm
    return pl.pallas_call(
        functools.partial(_inproj_kernel, n_chunk=512),
        out_shape=jax.ShapeDtypeStruct((N, C), out_dtype),
        grid=(N // tm,),
        in_specs=[pl.BlockSpec((tm, D), lambda i: (i, 0)),
                  pl.BlockSpec((1, 1, D), lambda i: (i // per, 0, 0)),
                  pl.BlockSpec((1, 1, D), lambda i: (i // per, 0, 0)),
                  pl.BlockSpec((1, D), lambda i: (0, 0)),
                  pl.BlockSpec((D, C), lambda i: (0, 0))],
        out_specs=pl.BlockSpec((tm, C), lambda i: (i, 0)),
        compiler_params=_cparams(("arbitrary",)),
        name="in_proj",
    )(x2d, sc, sh, g.reshape(1, D), w)


def _rwkv_local_kernel(p_ref, halo_ref, mu_ref, w0_ref, wup_ref, a0_ref, aup_ref, gup_ref,
                       kk_ref, ka_ref, rk_ref,
                       ry_ref, mg_ref, bonus_ref, gate_ref, *, tr, rows_per_seq):
    C = C_RWKV
    i = pl.program_id(0)
    first = (i * tr) % rows_per_seq == 0
    p = p_ref[:, :PF_RWKV]
    row = lax.broadcasted_iota(jnp.int32, (tr, 1), 0)
    prev_last = jnp.where(first, 0.0, halo_ref[7:8, :PF_RWKV])
    prev = jnp.where(row == 0, prev_last, pltpu.roll(p, 1, 0))
    ps = p + (prev - p) * mu_ref[...]
    r = ps[:, PF_R:PF_R + C]
    k = ps[:, PF_K:PF_K + C]
    v = ps[:, PF_V:PF_V + C]
    xw = ps[:, PF_XW:PF_XW + LANE]
    xa = ps[:, PF_XA:PF_XA + LANE]
    xg = ps[:, PF_XG:PF_XG + LANE]

    wl = w0_ref[...] + _dot(jnp.tanh(xw).astype(BF16), wup_ref[...])
    lw = -jnp.exp(-_softplus(-wl) - 0.5)
    a = _sigmoid(a0_ref[...] + _dot(xa.astype(BF16), aup_ref[...]))
    gate_ref[...] = _dot(_sigmoid(xg).astype(BF16), gup_ref[...])

    li = lax.broadcasted_iota(jnp.int32, (C, C), 0) // HEAD_DIM
    lj = lax.broadcasted_iota(jnp.int32, (C, C), 1) // HEAD_DIM
    same_head = li == lj
    head_ones = jnp.where(same_head, 1.0, 0.0).astype(BF16)

    kk = k * kk_ref[...]
    nrm = jnp.sqrt(_dot_exact_lhs_t(kk * kk, head_ones))
    kk = kk / jnp.maximum(nrm, 1e-12)
    k2 = k * (1.0 + (a - 1.0) * ka_ref[...])
    bonus_ref[...] = _dot_exact_lhs_t(r * k2 * rk_ref[...], head_ones) * v
    avec = -kk
    bvec = kk * a

    ti = lax.broadcasted_iota(jnp.int32, (CHUNK, CHUNK), 0)
    tj = lax.broadcasted_iota(jnp.int32, (CHUNK, CHUNK), 1)
    tri_incl = jnp.where(tj <= ti, 1.0, 0.0).astype(BF16)
    lower_strict = tj < ti
    lower_incl = tj <= ti
    lane = lax.broadcasted_iota(jnp.int32, (1, C), 1)
    eye = lax.broadcasted_iota(jnp.int32, (C, C), 0) == lax.broadcasted_iota(jnp.int32, (C, C), 1)

    for c in range(tr // CHUNK):
        sl = slice(c * CHUNK, (c + 1) * CHUNK)
        lwc = lw[sl]
        Lc = _dot_exact_lhs(tri_incl, lwc, 3)
        Lend = Lc[CHUNK - 1:CHUNK]
        e_prev = jnp.exp(Lc - lwc)
        e_cur = jnp.exp(Lc)
        e_inv = jnp.exp(-Lc)
        e_end = jnp.exp(Lend - Lc)
        At = avec[sl] * e_prev
        Rt = r[sl] * e_cur
        Bt = bvec[sl] * e_inv
        Kt = k2[sl] * e_inv
        Bh = bvec[sl] * e_end
        Kh = k2[sl] * e_end
        vc = v[sl]

        x_all = None
        rbar = Rt
        y0 = None
        for h in range(H_RWKV):
            mh = jnp.where((lane >= h * HEAD_DIM) & (lane < (h + 1) * HEAD_DIM), 1.0, 0.0)
            lhs = jnp.concatenate([At * mh, Rt * mh], axis=0)
            ab = _dot_hp(lhs, Bt, nt=True)
            ak = _dot_hp(lhs, Kt, nt=True)
            n_h = jnp.where(lower_strict, ab[:CHUNK], 0.0)
            a_ak = jnp.where(lower_strict, ak[:CHUNK], 0.0)
            a_rb = jnp.where(lower_incl, ab[CHUNK:], 0.0)
            a_rk = jnp.where(lower_incl, ak[CHUNK:], 0.0)
            vm = vc * mh
            av = _dot_hp(jnp.concatenate([a_ak, a_rk], axis=0), vm)
            x = jnp.concatenate([At * mh, av[:CHUNK]], axis=1)
            npow = n_h
            x = x + _dot_hp(npow, x)
            for _ in range(5):
                npow = _dot_hp(npow, npow)
                x = x + _dot_hp(npow, x)
            corr = _dot_hp(a_rb, x)
            rbar = rbar + corr[:, :C]
            y0h = av[CHUNK:] + corr[:, C:]
            y0 = y0h if y0 is None else y0 + y0h
            x_all = x if x_all is None else x_all + x
        ry_ref[sl, :C] = rbar
        ry_ref[sl, C:] = y0

        bkT = jnp.concatenate([Bh, Kh], axis=0).T
        m_mat = _dot_hp(bkT[:, :CHUNK], x_all)
        g_add = _dot_hp(bkT[:, CHUNK:], vc)
        pc = jnp.exp(Lend)
        m_blk = jnp.where(eye, pc, 0.0) + m_mat[:, :C]
        g_blk = m_mat[:, C:] + g_add
        mg_ref[c, :, :C] = jnp.where(same_head, m_blk, 0.0)
        mg_ref[c, :, C:] = jnp.where(same_head, g_blk, 0.0)


def _dot_exact_lhs_t(x, ones_bf16):
    xh, xl = _split_bf16(x, 2)
    return _dot(xh, ones_bf16) + _dot(xl, ones_bf16)


def _rwkv_local_call(pf, mu, w0, wup, a0, aup, gup, k_k, k_a, r_k, S, tr=128):
    N = pf.shape[0]
    C = C_RWKV
    row = lambda a: a.reshape(1, -1)
    full = lambda a: pl.BlockSpec(a.shape, lambda i: (0,) * a.ndim)
    args = [row(mu), row(w0), wup, row(a0), aup, gup, row(k_k), row(k_a), row(r_k)]
    return pl.pallas_call(
        functools.partial(_rwkv_local_kernel, tr=tr, rows_per_seq=S),
        out_shape=(jax.ShapeDtypeStruct((N, 2 * C), F32),
                   jax.ShapeDtypeStruct((N // CHUNK, C, 2 * C), F32),
                   jax.ShapeDtypeStruct((N, C), F32),
                   jax.ShapeDtypeStruct((N, C), F32)),
        grid=(N // tr,),
        in_specs=[pl.BlockSpec((tr, PF_COLS), lambda i: (i, 0)),
                  pl.BlockSpec((8, PF_COLS), lambda i: (jnp.maximum(i * (tr // 8) - 1, 0), 0))]
                 + [full(a) for a in args],
        out_specs=(pl.BlockSpec((tr, 2 * C), lambda i: (i, 0)),
                   pl.BlockSpec((tr // CHUNK, C, 2 * C), lambda i: (i, 0, 0)),
                   pl.BlockSpec((tr, C), lambda i: (i, 0)),
                   pl.BlockSpec((tr, C), lambda i: (i, 0))),
        compiler_params=_cparams(("arbitrary",)),
        name="rwkv_local",
    )(pf, pf, *args)


def _rwkv_scan_kernel(ry_ref, mg_ref, bonus_ref, gate_ref, lnw_ref, lnb_ref, o_ref, s_scr):
    C = C_RWKV
    B = ry_ref.shape[0]

    @pl.when(pl.program_id(0) == 0)
    def _():
        s_scr[...] = jnp.zeros_like(s_scr)

    li = lax.broadcasted_iota(jnp.int32, (C, C), 0) // HEAD_DIM
    lj = lax.broadcasted_iota(jnp.int32, (C, C), 1) // HEAD_DIM
    head_avg = jnp.where(li == lj, 1.0 / HEAD_DIM, 0.0).astype(BF16)
    for b in range(B):
        s0 = s_scr[b]
        y = _dot_hp(ry_ref[b, :, :C], s0) + ry_ref[b, :, C:]
        s_scr[b] = _dot_hp(mg_ref[b, 0, :, :C], s0) + mg_ref[b, 0, :, C:]
        mean = _dot_exact_lhs_t(y, head_avg)
        d = y - mean
        var = _dot_exact_lhs_t(d * d, head_avg)
        yn = d * lax.rsqrt(var + GN_EPS) * lnw_ref[...] + lnb_ref[...]
        o_ref[b] = ((yn + bonus_ref[b]) * gate_ref[b]).astype(o_ref.dtype)


def _rwkv_scan_call(ry, mg, bonus, gate, ln_w, ln_b, B, S):
    C = C_RWKV
    nc = S // CHUNK
    return pl.pallas_call(
        _rwkv_scan_kernel,
        out_shape=jax.ShapeDtypeStruct((B, S, C), BF16),
        grid=(nc,),
        in_specs=[pl.BlockSpec((B, CHUNK, 2 * C), lambda c: (0, c, 0)),
                  pl.BlockSpec((B, 1, C, 2 * C), lambda c: (0, c, 0, 0)),
                  pl.BlockSpec((B, CHUNK, C), lambda c: (0, c, 0)),
                  pl.BlockSpec((B, CHUNK, C), lambda c: (0, c, 0)),
                  pl.BlockSpec((1, C), lambda c: (0, 0)),
                  pl.BlockSpec((1, C), lambda c: (0, 0))],
        out_specs=pl.BlockSpec((B, CHUNK, C), lambda c: (0, c, 0)),
        scratch_shapes=[pltpu.VMEM((B, C, C), F32)],
        compiler_params=_cparams(("arbitrary",)),
        name="rwkv_scan",
    )(ry.reshape(B, S, 2 * C), mg.reshape(B, nc, C, 2 * C), bonus.reshape(B, S, C),
      gate.reshape(B, S, C), ln_w.reshape(1, C), ln_b.reshape(1, C))


def _online_update(carry, s, v):
    m, l, acc = carry
    m_new = jnp.maximum(m, jnp.max(s, axis=-1, keepdims=True))
    alpha = jnp.exp(m - m_new)
    p = jnp.exp(s - m_new)
    l = alpha * l + jnp.sum(p, axis=-1, keepdims=True)
    acc = alpha * acc + _dot(p.astype(BF16), v)
    return m_new, l, acc


def _attn_init(rows):
    return (jnp.full((rows, 1), NEG_INF, F32), jnp.zeros((rows, 1), F32), jnp.zeros((rows, LANE), F32))


def _fox_prep_kernel(misc_ref, k_ref, bf_ref, place_ref, o_ref, carry_scr, *, t):
    @pl.when(pl.program_id(1) == 0)
    def _():
        carry_scr[...] = jnp.zeros_like(carry_scr)

    lf = -_softplus(-(misc_ref[0] + bf_ref[...]))
    ti = lax.broadcasted_iota(jnp.int32, (t, t), 0)
    tj = lax.broadcasted_iota(jnp.int32, (t, t), 1)
    tri = jnp.where(tj <= ti, 1.0, 0.0).astype(BF16)
    cum = _dot_exact_lhs(tri, lf, 3) + carry_scr[...]
    carry_scr[...] = cum[t - 1:t]
    parts = _split_bf16(-cum, 3)
    out = k_ref[0].astype(F32)
    for j, part in enumerate(parts):
        out = out + _dot(part, place_ref[j])
    o_ref[0] = out.astype(BF16)


def _fox_prep_call(pf3, pb3, b_f_row, place, t=512):
    B, S, _ = pf3.shape
    W = H_FOX * LANE
    return pl.pallas_call(
        functools.partial(_fox_prep_kernel, t=t),
        out_shape=jax.ShapeDtypeStruct((B, S, W), BF16),
        grid=(B, S // t),
        in_specs=[pl.BlockSpec((1, t, LANE), lambda b, i: (b, i, PF_MISC // LANE)),
                  pl.BlockSpec((1, t, W), lambda b, i: (b, i, PB_FK // W)),
                  pl.BlockSpec((1, LANE), lambda b, i: (0, 0)),
                  pl.BlockSpec((3, LANE, W), lambda b, i: (0, 0, 0))],
        out_specs=pl.BlockSpec((1, t, W), lambda b, i: (b, i, 0)),
        scratch_shapes=[pltpu.VMEM((1, LANE), F32)],
        compiler_params=_cparams(("arbitrary", "arbitrary")),
        name="fox_prep",
    )(pf3, pb3, b_f_row, place)


def _fox_attn_kernel(q_ref, k_ref, v_ref, o_ref, *, tq):
    qi = pl.program_id(2)
    lane = lax.broadcasted_iota(jnp.int32, (1, LANE), 1)
    ones3 = jnp.where((lane >= HEAD_DIM) & (lane < HEAD_DIM + 3), 1.0, 0.0).astype(BF16)
    row = lax.broadcasted_iota(jnp.int32, (tq, tq), 0)
    col = lax.broadcasted_iota(jnp.int32, (tq, tq), 1)
    outs = []
    for h in range(2):
        q = q_ref[0, :, h * LANE:(h + 1) * LANE] + ones3

        def tile(j, h=h, q=q):
            start = pl.multiple_of(j * tq, tq)
            kt = k_ref[0, pl.ds(start, tq), h * LANE:(h + 1) * LANE]
            vt = v_ref[0, pl.ds(start, tq), :]
            return _dot_nt(q, kt), vt

        s, vt = tile(qi)
        carry = _online_update(_attn_init(tq), jnp.where(col <= row, s, NEG_INF), vt)

        def body(j, carry, tile=tile):
            s, vt = tile(j)
            return _online_update(carry, s, vt)

        m, l, acc = lax.fori_loop(0, qi, body, carry)
        outs.append(acc / l)
    o_ref[0] = jnp.where(lane < HEAD_DIM, outs[0], outs[1]).astype(o_ref.dtype)


def _fox_attn_call(pb3, kaug, tq=256):
    B, S, _ = pb3.shape
    return pl.pallas_call(
        functools.partial(_fox_attn_kernel, tq=tq),
        out_shape=jax.ShapeDtypeStruct((B, S, C_FOX), BF16),
        grid=(B, H_FOX // 2, S // tq),
        in_specs=[pl.BlockSpec((1, tq, 2 * LANE), lambda b, hp, i: (b, i, PB_FQ // (2 * LANE) + hp)),
                  pl.BlockSpec((1, S, 2 * LANE), lambda b, hp, i: (b, 0, hp)),
                  pl.BlockSpec((1, S, LANE), lambda b, hp, i: (b, 0, PB_FV // LANE + hp))],
        out_specs=pl.BlockSpec((1, tq, LANE), lambda b, hp, i: (b, i, hp)),
        compiler_params=_cparams(("arbitrary", "arbitrary", "arbitrary")),
        name="fox_attn",
    )(pb3, kaug, pb3)


def _bias_of_dist(n, tab_ref, h):
    val = jnp.zeros(n.shape, F32) + tab_ref[0, h]
    for b in range(1, NUM_BUCKETS):
        val = jnp.where(n >= BUCKET_LB[b], tab_ref[b, h], val)
    return val - tab_ref[NUM_BUCKETS - 1, h]


def _bias_cmp_kernel(tab_ref, o_ref, *, tt):
    t0 = pl.program_id(0) * tt
    nc = o_ref.shape[1]
    c = lax.broadcasted_iota(jnp.int32, (nc, tt), 0)
    t = lax.broadcasted_iota(jnp.int32, (nc, tt), 1) + t0
    n = jnp.maximum(t - (c * D_CMP + L_CMP - 1), 0)
    for h in range(H_NSA):
        o_ref[h] = _bias_of_dist(n, tab_ref, h)


def _bias_near_kernel(tab_ref, o_ref, *, tq):
    i = lax.broadcasted_iota(jnp.int32, (tq, tq), 0)
    j = lax.broadcasted_iota(jnp.int32, (tq, tq), 1)
    for g in range(G_NSA):
        for near in range(2):
            n = jnp.maximum(i - j + near * tq, 0)
            for h in range(HPG):
                o_ref[g, near, h * tq:(h + 1) * tq, :] = _bias_of_dist(n, tab_ref, g * HPG + h)


def _bias_tables(rel_bias, S, tq):
    nc = S // D_CMP
    tt = 512
    smem = pl.BlockSpec(memory_space=pltpu.SMEM)
    bias_c = pl.pallas_call(
        functools.partial(_bias_cmp_kernel, tt=tt),
        out_shape=jax.ShapeDtypeStruct((H_NSA, nc, S), F32),
        grid=(S // tt,),
        in_specs=[smem],
        out_specs=pl.BlockSpec((H_NSA, nc, tt), lambda i: (0, 0, i)),
        compiler_params=_cparams(("arbitrary",)),
        name="nsa_bias_cmp",
    )(rel_bias)
    bias_n = pl.pallas_call(
        functools.partial(_bias_near_kernel, tq=tq),
        out_shape=jax.ShapeDtypeStruct((G_NSA, 2, HPG * tq, tq), F32),
        in_specs=[smem],
        name="nsa_bias_near",
    )(rel_bias)
    return bias_c, bias_n


def _gelu_tanh(x):
    return 0.5 * x * (1.0 + jnp.tanh(math.sqrt(2.0 / math.pi) * (x + 0.044715 * (x * x * x))))


def _nsa_compress_kernel(kc_ref, vc_ref, wk1_ref, wv1_ref, wk2_ref, wv2_ref, pek_ref, pev_ref,
                         kcmp_ref, vcmpT_ref):
    nc = kc_ref.shape[1]
    for g in range(G_NSA):
        for src_ref, w1_ref, w2_ref, pe_ref, is_k in ((kc_ref, wk1_ref, wk2_ref, pek_ref, True),
                                                      (vc_ref, wv1_ref, wv2_ref, pev_ref, False)):
            ch = src_ref[0]
            p1 = _dot(ch, w1_ref[g, 0])
            p2 = _dot(ch, w1_ref[g, 1])
            pec = (_dot(pe_ref[0], w1_ref[g, 0].astype(F32)) + _dot(pe_ref[1], w1_ref[g, 1].astype(F32)))[0:1]
            hid = p1 + pltpu.roll(p2, nc - 1, 0) + pec
            act = _gelu_tanh(hid).astype(BF16)
            if is_k:
                kcmp_ref[0, g] = _dot(act, w2_ref[...]).astype(BF16)
            else:
                vcmpT_ref[0, g] = _dot_nt(w2_ref[...], act).astype(BF16)


def _nsa_compress_call(kc_flat, vc_flat, wk1, wv1, wk2, wv2, pek, pev):
    B, nc, W = kc_flat.shape
    full = lambda a: pl.BlockSpec(a.shape, lambda b: (0,) * a.ndim)
    return pl.pallas_call(
        _nsa_compress_kernel,
        out_shape=(jax.ShapeDtypeStruct((B, G_NSA, nc, LANE), BF16),
                   jax.ShapeDtypeStruct((B, G_NSA, HEAD_DIM, nc), BF16)),
        grid=(B,),
        in_specs=[pl.BlockSpec((1, nc, W), lambda b: (b, 0, 0)),
                  pl.BlockSpec((1, nc, W), lambda b: (b, 0, 0)),
                  full(wk1), full(wv1), full(wk2), full(wv2), full(pek), full(pev)],
        out_specs=(pl.BlockSpec((1, G_NSA, nc, LANE), lambda b: (b, 0, 0, 0)),
                   pl.BlockSpec((1, G_NSA, HEAD_DIM, nc), lambda b: (b, 0, 0, 0))),
        compiler_params=_cparams(("arbitrary",)),
        name="nsa_compress",
    )(kc_flat, vc_flat, wk1, wv1, wk2, wv2, pek, pev)


def _nsa_select_kernel(q_ref, kcmp_ref, vcmpT_ref, bias_ref, ov_ref, ocmp_ref, neg_ref, *, tq):
    qi = pl.program_id(1)
    nc = kcmp_ref.shape[2]
    nsb = ov_ref.shape[0]
    t = lax.broadcasted_iota(jnp.int32, (1, tq), 1) + qi * tq
    cidx = lax.broadcasted_iota(jnp.int32, (nc, 1), 0)
    valid_c = (cidx * D_CMP + L_CMP - 1) <= t
    jf = lax.broadcasted_iota(jnp.int32, (nsb, tq), 0).astype(F32)
    jb = lax.broadcasted_iota(jnp.int32, (nsb, 1), 0)
    back = t // L_SLC - jb
    valid_b = back >= 0
    forced = (jb == 0) | (valid_b & (back < N_LOCAL))
    o_t = []
    for g in range(G_NSA):
        imp = jnp.zeros((nsb, tq), F32)
        for h in range(HPG):
            hh = g * HPG + h
            q = q_ref[0, :, hh * LANE:(hh + 1) * LANE]
            s = _dot_nt(kcmp_ref[0, g], q) + bias_ref[hh]
            s = jnp.where(valid_c, s, NEG_INF)
            m = jnp.max(s, axis=0, keepdims=True)
            p = jnp.where(valid_c, jnp.exp(s - m), 0.0)
            l = jnp.sum(p, axis=0, keepdims=True)
            p = (p / jnp.maximum(l, 1e-30)).astype(BF16)
            o_t.append(_dot(vcmpT_ref[0, g], p))
            imp = imp + _dot(ov_ref[...], p)
        score = jnp.where(valid_b, jnp.where(forced, FORCE_SCORE, imp), -1.0)
        sel = jnp.zeros((nsb, tq), F32)
        for _ in range(N_SLC):
            mx = jnp.max(score, axis=0, keepdims=True)
            first = jnp.min(jnp.where(score == mx, jf, float(nsb)), axis=0, keepdims=True)
            pick = jf == first
            sel = jnp.where(pick & (mx >= 0.0), 1.0, sel)
            score = jnp.where(pick, -2.0, score)
        neg = jnp.where(sel > 0.0, 0.0, NEG_INF)
        pieces = [jnp.zeros((HEAD_DIM, tq), F32), neg]
        if nsb < HEAD_DIM:
            pieces.append(jnp.zeros((HEAD_DIM - nsb, tq), F32))
        neg_ref[0, g] = jnp.concatenate(pieces, axis=0).T.astype(BF16)
    for pair in range(H_NSA // 2):
        both = jnp.concatenate([o_t[2 * pair], o_t[2 * pair + 1]], axis=0)
        ocmp_ref[0, :, pair * LANE:(pair + 1) * LANE] = both.T


def _nsa_select_call(pb3, kcmp, vcmpT, bias_c, overlap_t, tq):
    B, S, _ = pb3.shape
    nc = kcmp.shape[2]
    W = H_NSA * LANE
    return pl.pallas_call(
        functools.partial(_nsa_select_kernel, tq=tq),
        out_shape=(jax.ShapeDtypeStruct((B, S, C_NSA), F32),
                   jax.ShapeDtypeStruct((B, G_NSA, S, LANE), BF16)),
        grid=(B, S // tq),
        in_specs=[pl.BlockSpec((1, tq, W), lambda b, i: (b, i, PB_NQ // W)),
                  pl.BlockSpec((1, G_NSA, nc, LANE), lambda b, i: (b, 0, 0, 0)),
                  pl.BlockSpec((1, G_NSA, HEAD_DIM, nc), lambda b, i: (b, 0, 0, 0)),
                  pl.BlockSpec((H_NSA, nc, tq), lambda b, i: (0, 0, i)),
                  pl.BlockSpec(overlap_t.shape, lambda b, i: (0, 0))],
        out_specs=(pl.BlockSpec((1, tq, C_NSA), lambda b, i: (b, i, 0)),
                   pl.BlockSpec((1, G_NSA, tq, LANE), lambda b, i: (b, 0, i, 0))),
        compiler_params=_cparams(("arbitrary", "arbitrary")),
        name="nsa_select",
    )(pb3, kcmp, vcmpT, bias_c, overlap_t)


def _nsa_attn_kernel(q_ref, neg_ref, ks_ref, vs_ref, kw_ref, vw_ref, ocmp_ref, misc_ref, bias_ref,
                     o_ref, kaug_scr, *, tq):
    qi = pl.program_id(1)
    S = ks_ref.shape[1]
    rows = HPG * tq

    @pl.when(qi == 0)
    def _():
        srow = lax.broadcasted_iota(jnp.int32, (S, LANE), 0)
        slane = lax.broadcasted_iota(jnp.int32, (S, LANE), 1)
        onehot = jnp.where(slane == HEAD_DIM + srow // L_SLC, 1.0, 0.0).astype(BF16)
        for g in range(G_NSA):
            kaug_scr[g] = ks_ref[0, :, g * LANE:(g + 1) * LANE] + onehot

    ri = lax.broadcasted_iota(jnp.int32, (rows, tq), 0) % tq
    cj = lax.broadcasted_iota(jnp.int32, (rows, tq), 1)
    causal = cj <= ri
    lane = lax.broadcasted_iota(jnp.int32, (1, LANE), 1)
    gates = _sigmoid(misc_ref[0])
    ocmp = ocmp_ref[0]
    prev = jnp.maximum(qi - 1, 0)
    has_prev = qi >= 1
    heads = []
    for g in range(G_NSA):
        qa = jnp.concatenate(
            [q_ref[0, :, (g * HPG + h) * LANE:(g * HPG + h + 1) * LANE] + neg_ref[0, g] for h in range(HPG)],
            axis=0)
        gl = slice(g * LANE, (g + 1) * LANE)

        def sel_tile(j, g=g, qa=qa, gl=gl):
            start = pl.multiple_of(j * tq, tq)
            return _dot_nt(qa, kaug_scr[g, pl.ds(start, tq), :]), vs_ref[0, pl.ds(start, tq), gl]

        def win_tile(j, qa=qa, gl=gl):
            start = pl.multiple_of(j * tq, tq)
            return _dot_nt(qa, kw_ref[0, pl.ds(start, tq), gl]), vw_ref[0, pl.ds(start, tq), gl]

        s, vt = sel_tile(qi)
        carry = _online_update(_attn_init(rows), jnp.where(causal, s + bias_ref[g, 0], NEG_INF), vt)
        s, vt = sel_tile(prev)
        carry = _online_update(carry, jnp.where(has_prev, s + bias_ref[g, 1], NEG_INF), vt)

        def body(j, carry, sel_tile=sel_tile):
            s, vt = sel_tile(j)
            return _online_update(carry, s, vt)

        m, l, acc = lax.fori_loop(0, prev, body, carry)
        o_slc = acc / l

        s, vt = win_tile(qi)
        carry = _online_update(_attn_init(rows), jnp.where(causal, s + bias_ref[g, 0], NEG_INF), vt)
        s, vt = win_tile(prev)
        carry = _online_update(carry, jnp.where(has_prev, s + bias_ref[g, 1], NEG_INF), vt)
        n_win = WINDOW // tq
        for d in range(2, n_win + 1):
            j = qi - d
            s, vt = win_tile(jnp.maximum(j, 0))
            ok = j >= 0
            if d == n_win:
                s = jnp.where(ok & (cj > ri), s, NEG_INF)
            else:
                s = jnp.where(ok, s, NEG_INF)
            carry = _online_update(carry, s, vt)
        m, l, acc = carry
        o_win = acc / l

        for h in range(HPG):
            hh = g * HPG + h
            base = MISC_GATE + hh * 3
            rs = slice(h * tq, (h + 1) * tq)
            oc = ocmp[:, hh * HEAD_DIM:(hh + 1) * HEAD_DIM]
            o = (gates[:, base:base + 1] * oc + gates[:, base + 1:base + 2] * o_slc[rs, :HEAD_DIM]
                 + gates[:, base + 2:base + 3] * o_win[rs, :HEAD_DIM])
            heads.append(o)
    o_ref[0] = jnp.concatenate(heads, axis=1).astype(o_ref.dtype)


def _nsa_attn_call(pb3, neg, ocmp, pf3, bias_n, tq):
    B, S, _ = pb3.shape
    W = H_NSA * LANE
    kv = lambda col: pl.BlockSpec((1, S, 2 * LANE), lambda b, i: (b, 0, col // (2 * LANE)))
    return pl.pallas_call(
        functools.partial(_nsa_attn_kernel, tq=tq),
        out_shape=jax.ShapeDtypeStruct((B, S, C_NSA), BF16),
        grid=(B, S // tq),
        in_specs=[pl.BlockSpec((1, tq, W), lambda b, i: (b, i, PB_NQ // W)),
                  pl.BlockSpec((1, G_NSA, tq, LANE), lambda b, i: (b, 0, i, 0)),
                  kv(PB_KS), kv(PB_VS), kv(PB_KW), kv(PB_VW),
                  pl.BlockSpec((1, tq, C_NSA), lambda b, i: (b, i, 0)),
                  pl.BlockSpec((1, tq, LANE), lambda b, i: (b, i, PF_MISC // LANE)),
                  pl.BlockSpec(bias_n.shape, lambda b, i: (0, 0, 0, 0))],
        out_specs=pl.BlockSpec((1, tq, C_NSA), lambda b, i: (b, i, 0)),
        scratch_shapes=[pltpu.VMEM((G_NSA, S, LANE), BF16)],
        compiler_params=_cparams(("arbitrary", "arbitrary")),
        name="nsa_attn",
    )(pb3, neg, pb3, pb3, pb3, pb3, ocmp, pf3, bias_n)


def _outproj_kernel(x_ref, ya_ref, yb_ref, yc_ref, w_ref, gm_ref, g_ref, o_ref):
    ca, cb = ya_ref.shape[1], yb_ref.shape[1]
    y = (_dot(ya_ref[...], w_ref[:ca]) + _dot(yb_ref[...], w_ref[ca:ca + cb])
         + _dot(yc_ref[...], w_ref[ca + cb:]))
    o_ref[...] = x_ref[...] + gm_ref[0] * _rmsnorm(y, g_ref[...])


def _outproj_call(x2d, ya, yb, yc, w, gm, g, S, tm=512):
    N, D = x2d.shape
    per = S // tm
    rows = lambda a: pl.BlockSpec((tm, a.shape[1]), lambda i: (i, 0))
    return pl.pallas_call(
        _outproj_kernel,
        out_shape=jax.ShapeDtypeStruct((N, D), F32),
        grid=(N // tm,),
        in_specs=[rows(x2d), rows(ya), rows(yb), rows(yc),
                  pl.BlockSpec(w.shape, lambda i: (0, 0)),
                  pl.BlockSpec((1, 1, D), lambda i: (i // per, 0, 0)),
                  pl.BlockSpec((1, D), lambda i: (0, 0))],
        out_specs=pl.BlockSpec((tm, D), lambda i: (i, 0)),
        compiler_params=_cparams(("arbitrary",)),
        name="out_proj",
    )(x2d, ya, yb, yc, w, gm, g.reshape(1, D))


def _ffn_kernel(x_ref, halo_ref, sc_ref, sh_ref, gf_ref, g2_ref, g3_ref, wg_ref, wv_ref,
                cwg_ref, cwv_ref, cbg_ref, cbv_ref, wd_ref, o_ref, h_scr, acc_scr, *, tm, rows_per_seq):
    i = pl.program_id(0)
    f = pl.program_id(1)

    @pl.when(f == 0)
    def _():
        xe = jnp.concatenate([halo_ref[...], x_ref[...]], axis=0)
        h = _rmsnorm(xe, g2_ref[...]) * (1.0 + sc_ref[0]) + sh_ref[0]
        row = lax.broadcasted_iota(jnp.int32, (tm + 8, 1), 0)
        first = (i * tm) % rows_per_seq == 0
        h_scr[...] = jnp.where((row < 8) & first, 0.0, h).astype(BF16)
        acc_scr[...] = jnp.zeros_like(acc_scr)

    h = h_scr[...]

    def conv(w_ref, cw_ref, cb_ref):
        u = _dot(h, w_ref[...])
        y = (cw_ref[2:3] * u + cw_ref[1:2] * pltpu.roll(u, 1, 0) + cw_ref[0:1] * pltpu.roll(u, 2, 0)
             + cb_ref[...])
        return y[8:]

    gate = conv(wg_ref, cwg_ref, cbg_ref)
    val = conv(wv_ref, cwv_ref, cbv_ref)
    act = (gate * _sigmoid(gate) * val).astype(BF16)
    acc_scr[...] += _dot(act, wd_ref[...])

    @pl.when(f == pl.num_programs(1) - 1)
    def _():
        o_ref[...] = x_ref[...] + gf_ref[0] * _rmsnorm(acc_scr[...], g3_ref[...])


def _ffn_call(x2d, sc, sh, gf, g2, g3, w_up, conv_w, conv_b, w_down, S, tm=512, tf=1408):
    N, D = x2d.shape
    F = w_down.shape[0]
    nf = F // tf
    per = S // tm
    mod = pl.BlockSpec((1, 1, D), lambda i, f: (i // per, 0, 0))
    vec = pl.BlockSpec((1, D), lambda i, f: (0, 0))
    cb = conv_b.reshape(1, 2 * F)
    return pl.pallas_call(
        functools.partial(_ffn_kernel, tm=tm, rows_per_seq=S),
        out_shape=jax.ShapeDtypeStruct((N, D), F32),
        grid=(N // tm, nf),
        in_specs=[pl.BlockSpec((tm, D), lambda i, f: (i, 0)),
                  pl.BlockSpec((8, D), lambda i, f: (jnp.maximum(i * (tm // 8) - 1, 0), 0)),
                  mod, mod, mod, vec, vec,
                  pl.BlockSpec((D, tf), lambda i, f: (0, f)),
                  pl.BlockSpec((D, tf), lambda i, f: (0, nf + f)),
                  pl.BlockSpec((CONV_W, tf), lambda i, f: (0, f)),
                  pl.BlockSpec((CONV_W, tf), lambda i, f: (0, nf + f)),
                  pl.BlockSpec((1, tf), lambda i, f: (0, f)),
                  pl.BlockSpec((1, tf), lambda i, f: (0, nf + f)),
                  pl.BlockSpec((tf, D), lambda i, f: (f, 0))],
        out_specs=pl.BlockSpec((tm, D), lambda i, f: (i, 0)),
        scratch_shapes=[pltpu.VMEM((tm + 8, D), BF16), pltpu.VMEM((tm, D), F32)],
        compiler_params=_cparams(("arbitrary", "arbitrary")),
        name="conv_ffn",
    )(x2d, x2d, sc, sh, gf, g2.reshape(1, D), g3.reshape(1, D), w_up, w_up, conv_w, conv_w, cb, cb, w_down)


def _column_maps():
    n_rwkv = 3 * C_RWKV + R_DECAY + R_AAA + R_GATE
    n_fox = 3 * C_FOX + H_FOX
    fox0 = n_rwkv
    nsa0 = n_rwkv + n_fox
    pf = np.full(PF_COLS, -1, np.int64)
    pf[PF_R:PF_R + 3 * C_RWKV] = np.arange(3 * C_RWKV)
    pf[PF_XW:PF_XW + R_DECAY] = 3 * C_RWKV + np.arange(R_DECAY)
    pf[PF_XA:PF_XA + R_AAA] = 3 * C_RWKV + R_DECAY + np.arange(R_AAA)
    pf[PF_XG:PF_XG + R_GATE] = 3 * C_RWKV + R_DECAY + R_AAA + np.arange(R_GATE)
    pf[PF_MISC + MISC_F:PF_MISC + MISC_F + H_FOX] = fox0 + 3 * C_FOX + np.arange(H_FOX)
    nsa_gate0 = nsa0 + C_NSA + 6 * G_NSA * HEAD_DIM
    pf[PF_MISC + MISC_GATE:PF_MISC + MISC_GATE + 3 * H_NSA] = nsa_gate0 + np.arange(3 * H_NSA)

    pb = np.full(PB_COLS, -1, np.int64)
    scale = np.ones(PB_COLS, np.float32)
    d = np.arange(HEAD_DIM)
    for h in range(H_FOX):
        pb[PB_FQ + h * LANE + d] = fox0 + h * HEAD_DIM + d
        scale[PB_FQ + h * LANE + d] = HEAD_DIM ** -0.5
        pb[PB_FK + h * LANE + d] = fox0 + C_FOX + h * HEAD_DIM + d
    pb[PB_FV:PB_FV + C_FOX] = fox0 + 2 * C_FOX + np.arange(C_FOX)
    for h in range(H_NSA):
        pb[PB_NQ + h * LANE + d] = nsa0 + h * HEAD_DIM + d
        scale[PB_NQ + h * LANE + d] = HEAD_DIM ** -0.5
    ckv = G_NSA * HEAD_DIM
    kc0 = nsa0 + C_NSA
    pb[PB_KC:PB_KC + ckv] = kc0 + np.arange(ckv)
    pb[PB_VC:PB_VC + ckv] = kc0 + ckv + np.arange(ckv)
    for n, base in enumerate((PB_KS, PB_VS, PB_KW, PB_VW)):
        for g in range(G_NSA):
            pb[base + g * LANE + d] = kc0 + (2 + n) * ckv + g * HEAD_DIM + d
    return pf, pb, scale


def _pad_rows(w, rows):
    return jnp.concatenate([w, jnp.zeros((rows - w.shape[0],) + w.shape[1:], w.dtype)], axis=0)


def _compress_w1(w1):
    hid = w1.shape[1]
    w = w1.reshape(2, D_CMP, HEAD_DIM, hid)
    out = jnp.zeros((G_NSA, 2, D_CMP, G_NSA, HEAD_DIM, hid), w1.dtype)
    for g in range(G_NSA):
        out = out.at[g, :, :, g].set(w)
    return out.reshape(G_NSA, 2, D_CMP * G_NSA * HEAD_DIM, hid).astype(BF16)


def _compress_pe(pe):
    half = pe.reshape(2, 1, D_CMP, 1, HEAD_DIM)
    return jnp.broadcast_to(half, (2, 8, D_CMP, G_NSA, HEAD_DIM)).reshape(2, 8, D_CMP * G_NSA * HEAD_DIM)


def _fox_place():
    place = np.zeros((3, LANE, H_FOX * LANE), np.float32)
    for j in range(3):
        for h in range(H_FOX):
            place[j, MISC_F + h, h * LANE + HEAD_DIM + j] = 1.0
    return jnp.asarray(place, BF16)


def _overlap_t(S):
    nc = S // D_CMP
    nsb = S // L_SLC
    c0 = np.arange(nc) * D_CMP
    c1 = c0 + L_CMP - 1
    s0 = np.arange(nsb) * L_SLC
    ov = (c0[None, :] <= s0[:, None] + L_SLC - 1) & (c1[None, :] >= s0[:, None])
    ov[:, nc - 1] = False
    return jnp.asarray(ov.astype(np.float32), BF16)


def kernel(x, c, ada_w, ada_b, norm_g, w_in, rwkv_mu, rwkv_w0, rwkv_w_up, rwkv_a0, rwkv_a_up, rwkv_g_up, rwkv_k_k, rwkv_k_a, rwkv_r_k, rwkv_ln_w, rwkv_ln_b, fox_b_f, nsa_pe_k, nsa_pe_v, nsa_ck_w1, nsa_ck_w2, nsa_cv_w1, nsa_cv_w2, rel_bias, w_out, ffn_up, ffn_conv_w, ffn_conv_b, ffn_down):
    B, S, D = x.shape
    L = w_in.shape[0]
    tq_nsa = 128
    assert S % 512 == 0 and S // L_SLC <= HEAD_DIM and D == 1024

    pf_idx, pb_idx, pb_scale = _column_maps()
    w_ext = jnp.concatenate([w_in, jnp.zeros((L, D, 1), w_in.dtype)], axis=2)
    w_pf = jnp.take(w_ext, jnp.asarray(pf_idx), axis=2).astype(BF16)
    w_pb = (jnp.take(w_ext, jnp.asarray(pb_idx), axis=2) * pb_scale).astype(BF16)
    mu_ext = jnp.concatenate([rwkv_mu, jnp.zeros((L, 1), F32)], axis=1)
    mu_pf = jnp.take(mu_ext, jnp.asarray(pf_idx[:PF_RWKV]), axis=1)

    mod_all = _mod_call(c, ada_w, ada_b).reshape(L, B, 6, 1, D)
    bias_c, bias_n = _bias_tables(rel_bias, S, tq_nsa)
    place = _fox_place()
    overlap_t = _overlap_t(S)

    x2d = x.reshape(B * S, D)
    for l in range(L):
        sh_m, sc_m, g_m, sh_f, sc_f, g_f = (mod_all[l, :, j] for j in range(6))
        pf = _inproj_call(x2d, sc_m, sh_m, norm_g[l, 0], w_pf[l], F32, S)
        pb = _inproj_call(x2d, sc_m, sh_m, norm_g[l, 0], w_pb[l], BF16, S)
        pf3 = pf.reshape(B, S, PF_COLS)
        pb3 = pb.reshape(B, S, PB_COLS)

        ry, mg, bonus, gate = _rwkv_local_call(
            pf, mu_pf[l], rwkv_w0[l], _pad_rows(rwkv_w_up[l], LANE).astype(BF16), rwkv_a0[l],
            _pad_rows(rwkv_a_up[l], LANE).astype(BF16), _pad_rows(rwkv_g_up[l], LANE).astype(BF16),
            rwkv_k_k[l], rwkv_k_a[l], rwkv_r_k[l], S)
        ya = _rwkv_scan_call(ry, mg, bonus, gate, rwkv_ln_w[l], rwkv_ln_b[l], B, S)

        b_f_row = jnp.zeros((1, LANE), F32).at[0, MISC_F:MISC_F + H_FOX].set(fox_b_f[l])
        kaug = _fox_prep_call(pf3, pb3, b_f_row, place)
        yb = _fox_attn_call(pb3, kaug)

        kc_flat = pb3[:, :, PB_KC:PB_KC + LANE].reshape(B, S // D_CMP, D_CMP * LANE)
        vc_flat = pb3[:, :, PB_VC:PB_VC + LANE].reshape(B, S // D_CMP, D_CMP * LANE)
        w2pad = lambda w: jnp.concatenate([w, jnp.zeros_like(w)], axis=1).astype(BF16)
        kcmp, vcmpT = _nsa_compress_call(
            kc_flat, vc_flat, _compress_w1(nsa_ck_w1[l]), _compress_w1(nsa_cv_w1[l]),
            w2pad(nsa_ck_w2[l]), nsa_cv_w2[l].T.astype(BF16),
            _compress_pe(nsa_pe_k[l]), _compress_pe(nsa_pe_v[l]))
        ocmp, neg = _nsa_select_call(pb3, kcmp, vcmpT, bias_c, overlap_t, tq_nsa)
        yc = _nsa_attn_call(pb3, neg, ocmp, pf3, bias_n, tq_nsa)

        x2d = _outproj_call(x2d, ya.reshape(B * S, C_RWKV), yb.reshape(B * S, C_FOX),
                            yc.reshape(B * S, C_NSA), w_out[l].astype(BF16), g_m, norm_g[l, 1], S)
        x2d = _ffn_call(x2d, sc_f, sh_f, g_f, norm_g[l, 2], norm_g[l, 3], ffn_up[l].astype(BF16),
                        ffn_conv_w[l], ffn_conv_b[l], ffn_down[l].astype(BF16), S)
    return x2d.reshape(B, S, D)
```

```python
import functools
import math

import numpy as np
import jax
import jax.numpy as jnp
from jax import lax
from jax.experimental import pallas as pl
from jax.experimental.pallas import tpu as pltpu

F32 = jnp.float32
BF16 = jnp.bfloat16

HEAD_DIM = 64
H_RWKV = 4
C_RWKV = H_RWKV * HEAD_DIM
H_FOX = 6
C_FOX = H_FOX * HEAD_DIM
H_NSA = 6
C_NSA = H_NSA * HEAD_DIM
G_NSA = 2
HPG = H_NSA // G_NSA
R_DECAY = 32
R_AAA = 32
R_GATE = 64
L_CMP = 32
D_CMP = 16
CMP_HID = 128
L_SLC = 64
N_SLC = 16
N_LOCAL = 2
WINDOW = 512
NUM_BUCKETS = 32
MAX_DISTANCE = 128
CONV_W = 3
RMS_EPS = 1e-6
GN_EPS = 64e-5
NEG_INF = -1e30
FORCE_SCORE = 1e4

LANE = 128
CHUNK = 64
VMEM_LIMIT = 56 * 1024 * 1024

PF_R, PF_K, PF_V, PF_XW, PF_XA, PF_XG, PF_MISC = 0, 256, 512, 768, 896, 1024, 1152
PF_RWKV = 1152
PF_COLS = 1280
MISC_F = 0
MISC_GATE = 8
PB_FQ, PB_FK, PB_NQ = 0, 768, 1536
PB_FV, PB_KC, PB_VC = 2304, 2688, 2816
PB_KS, PB_VS, PB_KW, PB_VW = 3072, 3328, 3584, 3840
PB_COLS = 4096


def _bucket_lower_bounds():
    n = np.arange(0, 4 * MAX_DISTANCE, dtype=np.int64)
    max_exact = NUM_BUCKETS // 2
    nf = np.maximum(n, 1).astype(np.float32)
    large = max_exact + (np.log(nf / np.float32(max_exact)) / np.float32(math.log(MAX_DISTANCE / max_exact))
                         * np.float32(NUM_BUCKETS - max_exact)).astype(np.int32)
    large = np.minimum(large, NUM_BUCKETS - 1)
    bucket = np.where(n < max_exact, n, large)
    return [int(np.argmax(bucket >= b)) for b in range(NUM_BUCKETS)]


BUCKET_LB = _bucket_lower_bounds()
BIAS_CONST_DIST = BUCKET_LB[NUM_BUCKETS - 1]


def _cparams(sem, vmem=None):
    return pltpu.CompilerParams(dimension_semantics=sem, vmem_limit_bytes=vmem or VMEM_LIMIT)


def _dot(a, b):
    return jnp.dot(a, b, preferred_element_type=F32)


def _dot_nt(a, b):
    return lax.dot_general(a, b, (((1,), (1,)), ((), ())), preferred_element_type=F32)


def _split_bf16(x, n):
    parts, r = [], x
    for i in range(n):
        p = r.astype(BF16)
        parts.append(p)
        if i + 1 < n:
            r = r - p.astype(F32)
    return parts


def _dot_hp(a, b, nt=False):
    f = _dot_nt if nt else _dot
    ah, al = _split_bf16(a, 2)
    bh, bl = _split_bf16(b, 2)
    return f(ah, bh) + (f(ah, bl) + f(al, bh))


def _dot_exact_lhs(a_bf16, b, n):
    out = None
    for p in _split_bf16(b, n):
        t = _dot(a_bf16, p)
        out = t if out is None else out + t
    return out


def _softplus(x):
    return jnp.maximum(x, 0.0) + jnp.log(1.0 + jnp.exp(-jnp.abs(x)))


def _sigmoid(x):
    return 1.0 / (1.0 + jnp.exp(-x))


def _rmsnorm(x, g):
    return x * lax.rsqrt(jnp.mean(x * x, axis=-1, keepdims=True) + RMS_EPS) * g


def _mod_kernel(c_ref, w_ref, b_ref, o_ref):
    c = c_ref[...]
    s = (c * _sigmoid(c)).astype(BF16)
    o_ref[0] = _dot(s, w_ref[0].astype(BF16)) + b_ref[0]


def _mod_call(c, ada_w, ada_b):
    L, D, N = ada_w.shape
    B = c.shape[0]
    tn = 1536
    return pl.pallas_call(
        _mod_kernel,
        out_shape=jax.ShapeDtypeStruct((L, B, N), F32),
        grid=(L, N // tn),
        in_specs=[pl.BlockSpec((B, D), lambda l, j: (0, 0)),
                  pl.BlockSpec((1, D, tn), lambda l, j: (l, 0, j)),
                  pl.BlockSpec((1, 1, tn), lambda l, j: (l, 0, j))],
        out_specs=pl.BlockSpec((1, B, tn), lambda l, j: (l, 0, j)),
        compiler_params=_cparams(("arbitrary", "arbitrary")),
        name="adaln_mod",
    )(c, ada_w, ada_b.reshape(L, 1, N))


def _inproj_kernel(x_ref, sc_ref, sh_ref, g_ref, w_ref, o_ref, *, n_chunk):
    h = _rmsnorm(x_ref[...], g_ref[...]) * (1.0 + sc_ref[0]) + sh_ref[0]
    h = h.astype(BF16)
    n = o_ref.shape[1]
    for n0 in range(0, n, n_chunk):
        n1 = min(n, n0 + n_chunk)
        o_ref[:, n0:n1] = _dot(h, w_ref[:, n0:n1]).astype(o_ref.dtype)


def _inproj_call(x2d, sc, sh, g, w, out_dtype, S, tm=512):
    N, D = x2d.shape
    C = w.shape[1]
    per = S // tm
    return pl.pallas_call(
        functools.partial(_inproj_kernel, n_chunk=512),
        out_shape=jax.ShapeDtypeStruct((N, C), out_dtype),
        grid=(N // tm,),
        in_specs=[pl.BlockSpec((tm, D), lambda i: (i, 0)),
                  pl.BlockSpec((1, 1, D), lambda i: (i // per, 0, 0)),
                  pl.BlockSpec((1, 1, D), lambda i: (i // per, 0, 0)),
                  pl.BlockSpec((1, D), lambda i: (0, 0)),
                  pl.BlockSpec((D, C), lambda i: (0, 0))],
        out_specs=pl.BlockSpec((tm, C), lambda i: (i, 0)),
        compiler_params=_cparams(("arbitrary",)),
        name="in_proj",
    )(x2d, sc, sh, g.reshape(1, D), w)


def _rwkv_local_kernel(p_ref, halo_ref, mu_ref, w0_ref, wup_ref, a0_ref, aup_ref, gup_ref,
                       kk_ref, ka_ref, rk_ref,
                       ry_ref, mg_ref, bonus_ref, gate_ref, *, tr, rows_per_seq):
    C = C_RWKV
    i = pl.program_id(0)
    first = (i * tr) % rows_per_seq == 0
    p = p_ref[:, :PF_RWKV]
    row = lax.broadcasted_iota(jnp.int32, (tr, 1), 0)
    prev_last = jnp.where(first, 0.0, halo_ref[7:8, :PF_RWKV])
    prev = jnp.where(row == 0, prev_last, pltpu.roll(p, 1, 0))
    ps = p + (prev - p) * mu_ref[...]
    r = ps[:, PF_R:PF_R + C]
    k = ps[:, PF_K:PF_K + C]
    v = ps[:, PF_V:PF_V + C]
    xw = ps[:, PF_XW:PF_XW + LANE]
    xa = ps[:, PF_XA:PF_XA + LANE]
    xg = ps[:, PF_XG:PF_XG + LANE]

    wl = w0_ref[...] + _dot(jnp.tanh(xw).astype(BF16), wup_ref[...])
    lw = -jnp.exp(-_softplus(-wl) - 0.5)
    a = _sigmoid(a0_ref[...] + _dot(xa.astype(BF16), aup_ref[...]))
    gate_ref[...] = _dot(_sigmoid(xg).astype(BF16), gup_ref[...])

    li = lax.broadcasted_iota(jnp.int32, (C, C), 0) // HEAD_DIM
    lj = lax.broadcasted_iota(jnp.int32, (C, C), 1) // HEAD_DIM
    same_head = li == lj
    head_ones = jnp.where(same_head, 1.0, 0.0).astype(BF16)

    kk = k * kk_ref[...]
    nrm = jnp.sqrt(_dot_exact_lhs_t(kk * kk, head_ones))
    kk = kk / jnp.maximum(nrm, 1e-12)
    k2 = k * (1.0 + (a - 1.0) * ka_ref[...])
    bonus_ref[...] = _dot_exact_lhs_t(r * k2 * rk_ref[...], head_ones) * v
    avec = -kk
    bvec = kk * a

    nch = tr // CHUNK
    ti = lax.broadcasted_iota(jnp.int32, (tr, tr), 0)
    tj = lax.broadcasted_iota(jnp.int32, (tr, tr), 1)
    same_chunk = ti // CHUNK == tj // CHUNK
    tri_incl = jnp.where(same_chunk & (tj <= ti), 1.0, 0.0).astype(BF16)
    chunk_ones = jnp.where(same_chunk, 1.0, 0.0).astype(BF16)
    Lc = _dot_exact_lhs(tri_incl, lw, 3)
    Lend = _dot_exact_lhs(chunk_ones, lw, 3)
    e_cur = jnp.exp(Lc)
    e_inv = jnp.exp(-Lc)
    e_end = jnp.exp(Lend - Lc)
    At = avec * jnp.exp(Lc - lw)
    Rt = r * e_cur
    Bt = bvec * e_inv
    Kt = k2 * e_inv
    bh_t = (bvec * e_end).T
    kh_t = (k2 * e_end).T
    pc = jnp.exp(Lend)

    pairs = [(c, h) for c in range(nch) for h in range(H_RWKV)]
    blk = lambda x: jnp.stack([x[c * CHUNK:(c + 1) * CHUNK, h * HEAD_DIM:(h + 1) * HEAD_DIM]
                               for c, h in pairs]).astype(BF16)
    blk_t = lambda x: jnp.stack([x[h * HEAD_DIM:(h + 1) * HEAD_DIM, c * CHUNK:(c + 1) * CHUNK]
                                 for c, h in pairs]).astype(BF16)
    bmm = lambda a, b: jnp.einsum('nij,njk->nik', a.astype(BF16), b.astype(BF16), preferred_element_type=F32)
    bmm_nt = lambda a, b: jnp.einsum('nid,nkd->nik', a, b, preferred_element_type=F32)

    ci = lax.broadcasted_iota(jnp.int32, (CHUNK, CHUNK), 0)
    cj = lax.broadcasted_iota(jnp.int32, (CHUNK, CHUNK), 1)
    lower_strict = cj < ci
    lower_incl = cj <= ci
    a_b, r_b, b_b, k_b, v_b = blk(At), blk(Rt), blk(Bt), blk(Kt), blk(v)
    ar = jnp.concatenate([a_b, r_b], axis=1)
    ab = bmm_nt(ar, b_b)
    ak = bmm_nt(ar, k_b)
    n_mat = jnp.where(lower_strict, ab[:, :CHUNK], 0.0)
    a_ak = jnp.where(lower_strict, ak[:, :CHUNK], 0.0)
    a_rb = jnp.where(lower_incl, ab[:, CHUNK:], 0.0)
    a_rk = jnp.where(lower_incl, ak[:, CHUNK:], 0.0)
    av = bmm(jnp.concatenate([a_ak, a_rk], axis=1), v_b)
    x = jnp.concatenate([a_b.astype(F32), av[:, :CHUNK]], axis=2)
    x = x + bmm(n_mat, x)
    for _ in range(5):
        n_mat = bmm(n_mat, n_mat)
        x = x + bmm(n_mat, x)
    corr = bmm(a_rb, x)
    rbar = r_b.astype(F32) + corr[:, :, :HEAD_DIM]
    y0 = av[:, CHUNK:] + corr[:, :, HEAD_DIM:]
    m_mat = bmm(blk_t(bh_t), x)
    g_add = bmm(blk_t(kh_t), v_b)
    eye = ci == cj
    zero = jnp.zeros((HEAD_DIM, HEAD_DIM), F32)
    for c in range(nch):
        sl = slice(c * CHUNK, (c + 1) * CHUNK)
        ns = [c * H_RWKV + h for h in range(H_RWKV)]
        ry_ref[sl, :C] = jnp.concatenate([rbar[n] for n in ns], axis=1)
        ry_ref[sl, C:] = jnp.concatenate([y0[n] for n in ns], axis=1)
        for h, n in enumerate(ns):
            hs = slice(h * HEAD_DIM, (h + 1) * HEAD_DIM)
            pc_h = pc[c * CHUNK:c * CHUNK + 1, hs]
            m_h = jnp.where(eye, pc_h, 0.0) + m_mat[n, :, :HEAD_DIM]
            g_h = m_mat[n, :, HEAD_DIM:] + g_add[n]
            mg_ref[c, hs, :C] = jnp.concatenate([m_h if j == h else zero for j in range(H_RWKV)], axis=1)
            mg_ref[c, hs, C:] = jnp.concatenate([g_h if j == h else zero for j in range(H_RWKV)], axis=1)


def _dot_exact_lhs_t(x, ones_bf16):
    xh, xl = _split_bf16(x, 2)
    return _dot(xh, ones_bf16) + _dot(xl, ones_bf16)


def _rwkv_local_call(pf, mu, w0, wup, a0, aup, gup, k_k, k_a, r_k, S, tr=256):
    N = pf.shape[0]
    C = C_RWKV
    row = lambda a: a.reshape(1, -1)
    full = lambda a: pl.BlockSpec(a.shape, lambda i: (0,) * a.ndim)
    args = [row(mu), row(w0), wup, row(a0), aup, gup, row(k_k), row(k_a), row(r_k)]
    return pl.pallas_call(
        functools.partial(_rwkv_local_kernel, tr=tr, rows_per_seq=S),
        out_shape=(jax.ShapeDtypeStruct((N, 2 * C), F32),
                   jax.ShapeDtypeStruct((N // CHUNK, C, 2 * C), F32),
                   jax.ShapeDtypeStruct((N, C), F32),
                   jax.ShapeDtypeStruct((N, C), F32)),
        grid=(N // tr,),
        in_specs=[pl.BlockSpec((tr, PF_COLS), lambda i: (i, 0)),
                  pl.BlockSpec((8, PF_COLS), lambda i: (jnp.maximum(i * (tr // 8) - 1, 0), 0))]
                 + [full(a) for a in args],
        out_specs=(pl.BlockSpec((tr, 2 * C), lambda i: (i, 0)),
                   pl.BlockSpec((tr // CHUNK, C, 2 * C), lambda i: (i, 0, 0)),
                   pl.BlockSpec((tr, C), lambda i: (i, 0)),
                   pl.BlockSpec((tr, C), lambda i: (i, 0))),
        compiler_params=_cparams(("arbitrary",)),
        name="rwkv_local",
    )(pf, pf, *args)


def _rwkv_scan_kernel(ry_ref, mg_ref, bonus_ref, gate_ref, lnw_ref, lnb_ref, o_ref, s_scr):
    C = C_RWKV
    B = ry_ref.shape[0]

    @pl.when(pl.program_id(0) == 0)
    def _():
        s_scr[...] = jnp.zeros_like(s_scr)

    li = lax.broadcasted_iota(jnp.int32, (C, C), 0) // HEAD_DIM
    lj = lax.broadcasted_iota(jnp.int32, (C, C), 1) // HEAD_DIM
    head_avg = jnp.where(li == lj, 1.0 / HEAD_DIM, 0.0).astype(BF16)
    for b in range(B):
        s0 = s_scr[b]
        y = _dot_hp(ry_ref[b, :, :C], s0) + ry_ref[b, :, C:]
        s_scr[b] = _dot_hp(mg_ref[b, 0, :, :C], s0) + mg_ref[b, 0, :, C:]
        mean = _dot_exact_lhs_t(y, head_avg)
        d = y - mean
        var = _dot_exact_lhs_t(d * d, head_avg)
        yn = d * lax.rsqrt(var + GN_EPS) * lnw_ref[...] + lnb_ref[...]
        o_ref[b] = ((yn + bonus_ref[b]) * gate_ref[b]).astype(o_ref.dtype)


def _rwkv_scan_call(ry, mg, bonus, gate, ln_w, ln_b, B, S):
    C = C_RWKV
    nc = S // CHUNK
    return pl.pallas_call(
        _rwkv_scan_kernel,
        out_shape=jax.ShapeDtypeStruct((B, S, C), BF16),
        grid=(nc,),
        in_specs=[pl.BlockSpec((B, CHUNK, 2 * C), lambda c: (0, c, 0)),
                  pl.BlockSpec((B, 1, C, 2 * C), lambda c: (0, c, 0, 0)),
                  pl.BlockSpec((B, CHUNK, C), lambda c: (0, c, 0)),
                  pl.BlockSpec((B, CHUNK, C), lambda c: (0, c, 0)),
                  pl.BlockSpec((1, C), lambda c: (0, 0)),
                  pl.BlockSpec((1, C), lambda c: (0, 0))],
        out_specs=pl.BlockSpec((B, CHUNK, C), lambda c: (0, c, 0)),
        scratch_shapes=[pltpu.VMEM((B, C, C), F32)],
        compiler_params=_cparams(("arbitrary",)),
        name="rwkv_scan",
    )(ry.reshape(B, S, 2 * C), mg.reshape(B, nc, C, 2 * C), bonus.reshape(B, S, C),
      gate.reshape(B, S, C), ln_w.reshape(1, C), ln_b.reshape(1, C))


def _online_update(carry, s, v):
    m, l, acc = carry
    m_new = jnp.maximum(m, jnp.max(s, axis=-1, keepdims=True))
    alpha = jnp.exp(m - m_new)
    p = jnp.exp(s - m_new)
    l = alpha * l + jnp.sum(p, axis=-1, keepdims=True)
    acc = alpha * acc + _dot(p.astype(BF16), v)
    return m_new, l, acc


def _attn_init(rows):
    return (jnp.full((rows, 1), NEG_INF, F32), jnp.zeros((rows, 1), F32), jnp.zeros((rows, LANE), F32))


def _online_update_t(carry, s_t, v_t):
    m, l, acc = carry
    m_new = jnp.maximum(m, jnp.max(s_t, axis=0, keepdims=True))
    alpha = jnp.exp(m - m_new)
    p = jnp.exp(s_t - m_new)
    l = alpha * l + jnp.sum(p, axis=0, keepdims=True)
    acc = alpha * acc + _dot(v_t, p.astype(BF16))
    return m_new, l, acc


def _attn_init_t(d, cols):
    return (jnp.full((1, cols), NEG_INF, F32), jnp.zeros((1, cols), F32), jnp.zeros((d, cols), F32))


def _online_update_multi(carries, ss, vts):
    n = range(len(ss))
    ms = [jnp.maximum(carries[i][0], jnp.max(ss[i], axis=0, keepdims=True)) for i in n]
    ps = [jnp.exp(ss[i] - ms[i]) for i in n]
    pvs = [_dot(vts[i], ps[i].astype(BF16)) for i in n]
    out = []
    for i in n:
        m, l, acc = carries[i]
        alpha = jnp.exp(m - ms[i])
        out.append((ms[i], alpha * l + jnp.sum(ps[i], axis=0, keepdims=True), alpha * acc + pvs[i]))
    return tuple(out)


def _fox_prep_kernel(misc_ref, k_ref, v_ref, bf_ref, place_ref, o_ref, vt_ref, carry_scr, *, t, tk):
    @pl.when(pl.program_id(1) == 0)
    def _():
        carry_scr[...] = jnp.zeros_like(carry_scr)

    for hp in range(H_FOX // 2):
        v_t = v_ref[0, :, hp * LANE:(hp + 1) * LANE].astype(F32).T
        for j in range(t // tk):
            for hl in range(2):
                vt_ref[0, 2 * hp + hl, j] = v_t[hl * HEAD_DIM:(hl + 1) * HEAD_DIM,
                                                j * tk:(j + 1) * tk].astype(BF16)

    lf = -_softplus(-(misc_ref[0] + bf_ref[...]))
    ti = lax.broadcasted_iota(jnp.int32, (t, t), 0)
    tj = lax.broadcasted_iota(jnp.int32, (t, t), 1)
    tri = jnp.where(tj <= ti, 1.0, 0.0).astype(BF16)
    cum = _dot_exact_lhs(tri, lf, 3) + carry_scr[...]
    carry_scr[...] = cum[t - 1:t]
    parts = _split_bf16(-cum, 3)
    out = k_ref[0].astype(F32)
    for j, part in enumerate(parts):
        out = out + _dot(part, place_ref[j])
    o_ref[0] = out.astype(BF16)


FOX_TK = 256


def _fox_prep_call(pf3, pb3, b_f_row, place, t=512):
    B, S, _ = pf3.shape
    W = H_FOX * LANE
    tk = FOX_TK
    return pl.pallas_call(
        functools.partial(_fox_prep_kernel, t=t, tk=tk),
        out_shape=(jax.ShapeDtypeStruct((B, S, W), BF16),
                   jax.ShapeDtypeStruct((B, H_FOX, S // tk, HEAD_DIM, tk), BF16)),
        grid=(B, S // t),
        in_specs=[pl.BlockSpec((1, t, LANE), lambda b, i: (b, i, PF_MISC // LANE)),
                  pl.BlockSpec((1, t, W), lambda b, i: (b, i, PB_FK // W)),
                  pl.BlockSpec((1, t, C_FOX), lambda b, i: (b, i, PB_FV // C_FOX)),
                  pl.BlockSpec((1, LANE), lambda b, i: (0, 0)),
                  pl.BlockSpec((3, LANE, W), lambda b, i: (0, 0, 0))],
        out_specs=(pl.BlockSpec((1, t, W), lambda b, i: (b, i, 0)),
                   pl.BlockSpec((1, H_FOX, t // tk, HEAD_DIM, tk), lambda b, i: (b, 0, i, 0, 0))),
        scratch_shapes=[pltpu.VMEM((1, LANE), F32)],
        compiler_params=_cparams(("arbitrary", "arbitrary")),
        name="fox_prep",
    )(pf3, pb3, pb3, b_f_row, place)


def _fox_attn_kernel(q_ref, k_ref, vt_ref, o_ref, *, tq, tk):
    qi = pl.program_id(1)
    lane = lax.broadcasted_iota(jnp.int32, (1, LANE), 1)
    ones3 = jnp.where((lane >= HEAD_DIM) & (lane < HEAD_DIM + 3), 1.0, 0.0).astype(BF16)
    krow = lax.broadcasted_iota(jnp.int32, (tk, tq), 0)
    qcol = lax.broadcasted_iota(jnp.int32, (tk, tq), 1)
    per = tq // tk
    heads = range(H_FOX)
    qs = [q_ref[0, :, h * LANE:(h + 1) * LANE] + ones3 for h in heads]

    def logits(j, h):
        start = pl.multiple_of(j * tk, tk)
        return _dot_nt(k_ref[0, pl.ds(start, tk), h * LANE:(h + 1) * LANE], qs[h])

    carries = tuple(_attn_init_t(HEAD_DIM, tq) for _ in heads)
    for d in range(per):
        j = qi * per + d
        ss = [jnp.where(krow + d * tk <= qcol, logits(j, h), NEG_INF) for h in heads]
        carries = _online_update_multi(carries, ss, [vt_ref[0, h, j] for h in heads])

    def body(j, carries):
        return _online_update_multi(carries, [logits(j, h) for h in heads], [vt_ref[0, h, j] for h in heads])

    carries = lax.fori_loop(0, qi * per, body, carries)
    outs = [acc / l for (_, l, acc) in carries]
    for hp in range(H_FOX // 2):
        pair = jnp.concatenate([outs[2 * hp], outs[2 * hp + 1]], axis=0)
        o_ref[0, :, hp * LANE:(hp + 1) * LANE] = pair.T.astype(o_ref.dtype)


def _fox_attn_call(pb3, kaug, v_t, tq=256):
    B, S, _ = pb3.shape
    tk = FOX_TK
    W = H_FOX * LANE
    return pl.pallas_call(
        functools.partial(_fox_attn_kernel, tq=tq, tk=tk),
        out_shape=jax.ShapeDtypeStruct((B, S, C_FOX), BF16),
        grid=(B, S // tq),
        in_specs=[pl.BlockSpec((1, tq, W), lambda b, i: (b, i, PB_FQ // W)),
                  pl.BlockSpec((1, S, W), lambda b, i: (b, 0, 0)),
                  pl.BlockSpec((1, H_FOX, S // tk, HEAD_DIM, tk), lambda b, i: (b, 0, 0, 0, 0))],
        out_specs=pl.BlockSpec((1, tq, C_FOX), lambda b, i: (b, i, 0)),
        compiler_params=_cparams(("arbitrary", "arbitrary")),
        name="fox_attn",
    )(pb3, kaug, v_t)


def _bias_of_dist(n, tab_ref, h):
    val = jnp.zeros(n.shape, F32) + tab_ref[0, h]
    for b in range(1, NUM_BUCKETS):
        val = jnp.where(n >= BUCKET_LB[b], tab_ref[b, h], val)
    return val - tab_ref[NUM_BUCKETS - 1, h]


def _bias_cmp_kernel(tab_ref, o_ref, *, tt):
    t0 = pl.program_id(0) * tt
    nc = o_ref.shape[1]
    c = lax.broadcasted_iota(jnp.int32, (nc, tt), 0)
    t = lax.broadcasted_iota(jnp.int32, (nc, tt), 1) + t0
    n = jnp.maximum(t - (c * D_CMP + L_CMP - 1), 0)
    for h in range(H_NSA):
        o_ref[h] = _bias_of_dist(n, tab_ref, h)


def _bias_near_kernel(tab_ref, o_ref, *, tq):
    i = lax.broadcasted_iota(jnp.int32, (tq, tq), 0)
    j = lax.broadcasted_iota(jnp.int32, (tq, tq), 1)
    for g in range(G_NSA):
        for near in range(2):
            n = jnp.maximum(i - j + near * tq, 0)
            for h in range(HPG):
                o_ref[g, near, h * tq:(h + 1) * tq, :] = _bias_of_dist(n, tab_ref, g * HPG + h)


def _bias_tables(rel_bias, S, tq):
    nc = S // D_CMP
    tt = 512
    smem = pl.BlockSpec(memory_space=pltpu.SMEM)
    bias_c = pl.pallas_call(
        functools.partial(_bias_cmp_kernel, tt=tt),
        out_shape=jax.ShapeDtypeStruct((H_NSA, nc, S), F32),
        grid=(S // tt,),
        in_specs=[smem],
        out_specs=pl.BlockSpec((H_NSA, nc, tt), lambda i: (0, 0, i)),
        compiler_params=_cparams(("arbitrary",)),
        name="nsa_bias_cmp",
    )(rel_bias)
    bias_n = pl.pallas_call(
        functools.partial(_bias_near_kernel, tq=tq),
        out_shape=jax.ShapeDtypeStruct((G_NSA, 2, HPG * tq, tq), F32),
        in_specs=[smem],
        name="nsa_bias_near",
    )(rel_bias)
    return bias_c, bias_n


def _gelu_tanh(x):
    return 0.5 * x * (1.0 + jnp.tanh(math.sqrt(2.0 / math.pi) * (x + 0.044715 * (x * x * x))))


def _nsa_compress_kernel(kc_ref, vc_ref, wk1_ref, wv1_ref, wk2_ref, wv2_ref, pek_ref, pev_ref,
                         kcmp_ref, vcmpT_ref):
    nc = kc_ref.shape[1]
    for g in range(G_NSA):
        for src_ref, w1_ref, w2_ref, pe_ref, is_k in ((kc_ref, wk1_ref, wk2_ref, pek_ref, True),
                                                      (vc_ref, wv1_ref, wv2_ref, pev_ref, False)):
            ch = src_ref[0]
            p1 = _dot(ch, w1_ref[g, 0])
            p2 = _dot(ch, w1_ref[g, 1])
            pec = (_dot(pe_ref[0], w1_ref[g, 0].astype(F32)) + _dot(pe_ref[1], w1_ref[g, 1].astype(F32)))[0:1]
            hid = p1 + pltpu.roll(p2, nc - 1, 0) + pec
            act = _gelu_tanh(hid).astype(BF16)
            if is_k:
                kcmp_ref[0, g] = _dot(act, w2_ref[...]).astype(BF16)
            else:
                vcmpT_ref[0, g] = _dot_nt(w2_ref[...], act).astype(BF16)


def _nsa_compress_call(kc_flat, vc_flat, wk1, wv1, wk2, wv2, pek, pev):
    B, nc, W = kc_flat.shape
    full = lambda a: pl.BlockSpec(a.shape, lambda b: (0,) * a.ndim)
    return pl.pallas_call(
        _nsa_compress_kernel,
        out_shape=(jax.ShapeDtypeStruct((B, G_NSA, nc, LANE), BF16),
                   jax.ShapeDtypeStruct((B, G_NSA, HEAD_DIM, nc), BF16)),
        grid=(B,),
        in_specs=[pl.BlockSpec((1, nc, W), lambda b: (b, 0, 0)),
                  pl.BlockSpec((1, nc, W), lambda b: (b, 0, 0)),
                  full(wk1), full(wv1), full(wk2), full(wv2), full(pek), full(pev)],
        out_specs=(pl.BlockSpec((1, G_NSA, nc, LANE), lambda b: (b, 0, 0, 0)),
                   pl.BlockSpec((1, G_NSA, HEAD_DIM, nc), lambda b: (b, 0, 0, 0))),
        compiler_params=_cparams(("arbitrary",)),
        name="nsa_compress",
    )(kc_flat, vc_flat, wk1, wv1, wk2, wv2, pek, pev)


def _nsa_select_kernel(q_ref, kcmp_ref, vcmpT_ref, bias_ref, ov_ref, ocmp_ref, neg_ref, *, tq):
    qi = pl.program_id(1)
    nc = kcmp_ref.shape[2]
    nsb = ov_ref.shape[0]
    t = lax.broadcasted_iota(jnp.int32, (1, tq), 1) + qi * tq
    cidx = lax.broadcasted_iota(jnp.int32, (nc, 1), 0)
    valid_c = (cidx * D_CMP + L_CMP - 1) <= t
    jf = lax.broadcasted_iota(jnp.int32, (nsb, tq), 0).astype(F32)
    jb = lax.broadcasted_iota(jnp.int32, (nsb, 1), 0)
    back = t // L_SLC - jb
    valid_b = back >= 0
    forced = (jb == 0) | (valid_b & (back < N_LOCAL))
    o_t = []
    for g in range(G_NSA):
        imp = jnp.zeros((nsb, tq), F32)
        for h in range(HPG):
            hh = g * HPG + h
            q = q_ref[0, :, hh * LANE:(hh + 1) * LANE]
            s = _dot_nt(kcmp_ref[0, g], q) + bias_ref[hh]
            s = jnp.where(valid_c, s, NEG_INF)
            m = jnp.max(s, axis=0, keepdims=True)
            p = jnp.where(valid_c, jnp.exp(s - m), 0.0)
            l = jnp.sum(p, axis=0, keepdims=True)
            p = (p / jnp.maximum(l, 1e-30)).astype(BF16)
            o_t.append(_dot(vcmpT_ref[0, g], p))
            imp = imp + _dot(ov_ref[...], p)
        score = jnp.where(valid_b, jnp.where(forced, FORCE_SCORE, imp), -1.0)
        sel = jnp.zeros((nsb, tq), F32)
        for _ in range(N_SLC):
            mx = jnp.max(score, axis=0, keepdims=True)
            first = jnp.min(jnp.where(score == mx, jf, float(nsb)), axis=0, keepdims=True)
            pick = jf == first
            sel = jnp.where(pick & (mx >= 0.0), 1.0, sel)
            score = jnp.where(pick, -2.0, score)
        neg = jnp.where(sel > 0.0, 0.0, NEG_INF)
        pieces = [jnp.zeros((HEAD_DIM, tq), F32), neg]
        if nsb < HEAD_DIM:
            pieces.append(jnp.zeros((HEAD_DIM - nsb, tq), F32))
        neg_ref[0, g] = jnp.concatenate(pieces, axis=0).T.astype(BF16)
    for pair in range(H_NSA // 2):
        both = jnp.concatenate([o_t[2 * pair], o_t[2 * pair + 1]], axis=0)
        ocmp_ref[0, :, pair * LANE:(pair + 1) * LANE] = both.T


def _nsa_select_call(pb3, kcmp, vcmpT, bias_c, overlap_t, tq):
    B, S, _ = pb3.shape
    nc = kcmp.shape[2]
    W = H_NSA * LANE
    return pl.pallas_call(
        functools.partial(_nsa_select_kernel, tq=tq),
        out_shape=(jax.ShapeDtypeStruct((B, S, C_NSA), F32),
                   jax.ShapeDtypeStruct((B, G_NSA, S, LANE), BF16)),
        grid=(B, S // tq),
        in_specs=[pl.BlockSpec((1, tq, W), lambda b, i: (b, i, PB_NQ // W)),
                  pl.BlockSpec((1, G_NSA, nc, LANE), lambda b, i: (b, 0, 0, 0)),
                  pl.BlockSpec((1, G_NSA, HEAD_DIM, nc), lambda b, i: (b, 0, 0, 0)),
                  pl.BlockSpec((H_NSA, nc, tq), lambda b, i: (0, 0, i)),
                  pl.BlockSpec(overlap_t.shape, lambda b, i: (0, 0))],
        out_specs=(pl.BlockSpec((1, tq, C_NSA), lambda b, i: (b, i, 0)),
                   pl.BlockSpec((1, G_NSA, tq, LANE), lambda b, i: (b, 0, i, 0))),
        compiler_params=_cparams(("arbitrary", "arbitrary")),
        name="nsa_select",
    )(pb3, kcmp, vcmpT, bias_c, overlap_t)


def _nsa_attn_kernel(q_ref, neg_ref, ks_ref, vs_ref, kw_ref, vw_ref, ocmp_ref, misc_ref, bias_ref,
                     o_ref, kaug_scr, *, tq):
    qi = pl.program_id(1)
    S = ks_ref.shape[1]
    rows = HPG * tq

    @pl.when(qi == 0)
    def _():
        srow = lax.broadcasted_iota(jnp.int32, (S, LANE), 0)
        slane = lax.broadcasted_iota(jnp.int32, (S, LANE), 1)
        onehot = jnp.where(slane == HEAD_DIM + srow // L_SLC, 1.0, 0.0).astype(BF16)
        for g in range(G_NSA):
            kaug_scr[g] = ks_ref[0, :, g * LANE:(g + 1) * LANE] + onehot

    ri = lax.broadcasted_iota(jnp.int32, (rows, tq), 0) % tq
    cj = lax.broadcasted_iota(jnp.int32, (rows, tq), 1)
    causal = cj <= ri
    lane = lax.broadcasted_iota(jnp.int32, (1, LANE), 1)
    gates = _sigmoid(misc_ref[0])
    ocmp = ocmp_ref[0]
    prev = jnp.maximum(qi - 1, 0)
    has_prev = qi >= 1
    heads = []
    for g in range(G_NSA):
        qa = jnp.concatenate(
            [q_ref[0, :, (g * HPG + h) * LANE:(g * HPG + h + 1) * LANE] + neg_ref[0, g] for h in range(HPG)],
            axis=0)
        gl = slice(g * LANE, (g + 1) * LANE)

        def sel_tile(j, g=g, qa=qa, gl=gl):
            start = pl.multiple_of(j * tq, tq)
            return _dot_nt(qa, kaug_scr[g, pl.ds(start, tq), :]), vs_ref[0, pl.ds(start, tq), gl]

        def win_tile(j, qa=qa, gl=gl):
            start = pl.multiple_of(j * tq, tq)
            return _dot_nt(qa, kw_ref[0, pl.ds(start, tq), gl]), vw_ref[0, pl.ds(start, tq), gl]

        s, vt = sel_tile(qi)
        carry = _online_update(_attn_init(rows), jnp.where(causal, s + bias_ref[g, 0], NEG_INF), vt)
        s, vt = sel_tile(prev)
        carry = _online_update(carry, jnp.where(has_prev, s + bias_ref[g, 1], NEG_INF), vt)

        def body(j, carry, sel_tile=sel_tile):
            s, vt = sel_tile(j)
            return _online_update(carry, s, vt)

        m, l, acc = lax.fori_loop(0, prev, body, carry)
        o_slc = acc / l

        s, vt = win_tile(qi)
        carry = _online_update(_attn_init(rows), jnp.where(causal, s + bias_ref[g, 0], NEG_INF), vt)
        s, vt = win_tile(prev)
        carry = _online_update(carry, jnp.where(has_prev, s + bias_ref[g, 1], NEG_INF), vt)
        n_win = WINDOW // tq
        for d in range(2, n_win + 1):
            j = qi - d
            s, vt = win_tile(jnp.maximum(j, 0))
            ok = j >= 0
            if d == n_win:
                s = jnp.where(ok & (cj > ri), s, NEG_INF)
            else:
                s = jnp.where(ok, s, NEG_INF)
            carry = _online_update(carry, s, vt)
        m, l, acc = carry
        o_win = acc / l

        for h in range(HPG):
            hh = g * HPG + h
            base = MISC_GATE + hh * 3
            rs = slice(h * tq, (h + 1) * tq)
            oc = ocmp[:, hh * HEAD_DIM:(hh + 1) * HEAD_DIM]
            o = (gates[:, base:base + 1] * oc + gates[:, base + 1:base + 2] * o_slc[rs, :HEAD_DIM]
                 + gates[:, base + 2:base + 3] * o_win[rs, :HEAD_DIM])
            heads.append(o)
    o_ref[0] = jnp.concatenate(heads, axis=1).astype(o_ref.dtype)


def _nsa_attn_call(pb3, neg, ocmp, pf3, bias_n, tq):
    B, S, _ = pb3.shape
    W = H_NSA * LANE
    kv = lambda col: pl.BlockSpec((1, S, 2 * LANE), lambda b, i: (b, 0, col // (2 * LANE)))
    return pl.pallas_call(
        functools.partial(_nsa_attn_kernel, tq=tq),
        out_shape=jax.ShapeDtypeStruct((B, S, C_NSA), BF16),
        grid=(B, S // tq),
        in_specs=[pl.BlockSpec((1, tq, W), lambda b, i: (b, i, PB_NQ // W)),
                  pl.BlockSpec((1, G_NSA, tq, LANE), lambda b, i: (b, 0, i, 0)),
                  kv(PB_KS), kv(PB_VS), kv(PB_KW), kv(PB_VW),
                  pl.BlockSpec((1, tq, C_NSA), lambda b, i: (b, i, 0)),
                  pl.BlockSpec((1, tq, LANE), lambda b, i: (b, i, PF_MISC // LANE)),
                  pl.BlockSpec(bias_n.shape, lambda b, i: (0, 0, 0, 0))],
        out_specs=pl.BlockSpec((1, tq, C_NSA), lambda b, i: (b, i, 0)),
        scratch_shapes=[pltpu.VMEM((G_NSA, S, LANE), BF16)],
        compiler_params=_cparams(("arbitrary", "arbitrary")),
        name="nsa_attn",
    )(pb3, neg, pb3, pb3, pb3, pb3, ocmp, pf3, bias_n)


def _outproj_kernel(x_ref, ya_ref, yb_ref, yc_ref, w_ref, gm_ref, g_ref, o_ref):
    ca, cb = ya_ref.shape[1], yb_ref.shape[1]
    y = (_dot(ya_ref[...], w_ref[:ca]) + _dot(yb_ref[...], w_ref[ca:ca + cb])
         + _dot(yc_ref[...], w_ref[ca + cb:]))
    o_ref[...] = x_ref[...] + gm_ref[0] * _rmsnorm(y, g_ref[...])


def _outproj_call(x2d, ya, yb, yc, w, gm, g, S, tm=512):
    N, D = x2d.shape
    per = S // tm
    rows = lambda a: pl.BlockSpec((tm, a.shape[1]), lambda i: (i, 0))
    return pl.pallas_call(
        _outproj_kernel,
        out_shape=jax.ShapeDtypeStruct((N, D), F32),
        grid=(N // tm,),
        in_specs=[rows(x2d), rows(ya), rows(yb), rows(yc),
                  pl.BlockSpec(w.shape, lambda i: (0, 0)),
                  pl.BlockSpec((1, 1, D), lambda i: (i // per, 0, 0)),
                  pl.BlockSpec((1, D), lambda i: (0, 0))],
        out_specs=pl.BlockSpec((tm, D), lambda i: (i, 0)),
        compiler_params=_cparams(("arbitrary",)),
        name="out_proj",
    )(x2d, ya, yb, yc, w, gm, g.reshape(1, D))


def _ffn_kernel(x_ref, halo_ref, sc_ref, sh_ref, gf_ref, g2_ref, g3_ref, wg_ref, wv_ref,
                cwg_ref, cwv_ref, cbg_ref, cbv_ref, wd_ref, o_ref, h_scr, acc_scr, *, tm, rows_per_seq):
    i = pl.program_id(0)
    f = pl.program_id(1)

    @pl.when(f == 0)
    def _():
        xe = jnp.concatenate([halo_ref[...], x_ref[...]], axis=0)
        h = _rmsnorm(xe, g2_ref[...]) * (1.0 + sc_ref[0]) + sh_ref[0]
        row = lax.broadcasted_iota(jnp.int32, (tm + 8, 1), 0)
        first = (i * tm) % rows_per_seq == 0
        h_scr[...] = jnp.where((row < 8) & first, 0.0, h).astype(BF16)
        acc_scr[...] = jnp.zeros_like(acc_scr)

    h = h_scr[...]

    def conv(w_ref, cw_ref, cb_ref):
        u = _dot(h, w_ref[...])
        y = (cw_ref[2:3] * u + cw_ref[1:2] * pltpu.roll(u, 1, 0) + cw_ref[0:1] * pltpu.roll(u, 2, 0)
             + cb_ref[...])
        return y[8:]

    gate = conv(wg_ref, cwg_ref, cbg_ref)
    val = conv(wv_ref, cwv_ref, cbv_ref)
    act = (gate * _sigmoid(gate) * val).astype(BF16)
    acc_scr[...] += _dot(act, wd_ref[...])

    @pl.when(f == pl.num_programs(1) - 1)
    def _():
        o_ref[...] = x_ref[...] + gf_ref[0] * _rmsnorm(acc_scr[...], g3_ref[...])


def _ffn_call(x2d, sc, sh, gf, g2, g3, w_up, conv_w, conv_b, w_down, S, tm=512, tf=1408):
    N, D = x2d.shape
    F = w_down.shape[0]
    nf = F // tf
    per = S // tm
    mod = pl.BlockSpec((1, 1, D), lambda i, f: (i // per, 0, 0))
    vec = pl.BlockSpec((1, D), lambda i, f: (0, 0))
    cb = conv_b.reshape(1, 2 * F)
    return pl.pallas_call(
        functools.partial(_ffn_kernel, tm=tm, rows_per_seq=S),
        out_shape=jax.ShapeDtypeStruct((N, D), F32),
        grid=(N // tm, nf),
        in_specs=[pl.BlockSpec((tm, D), lambda i, f: (i, 0)),
                  pl.BlockSpec((8, D), lambda i, f: (jnp.maximum(i * (tm // 8) - 1, 0), 0)),
                  mod, mod, mod, vec, vec,
                  pl.BlockSpec((D, tf), lambda i, f: (0, f)),
                  pl.BlockSpec((D, tf), lambda i, f: (0, nf + f)),
                  pl.BlockSpec((CONV_W, tf), lambda i, f: (0, f)),
                  pl.BlockSpec((CONV_W, tf), lambda i, f: (0, nf + f)),
                  pl.BlockSpec((1, tf), lambda i, f: (0, f)),
                  pl.BlockSpec((1, tf), lambda i, f: (0, nf + f)),
                  pl.BlockSpec((tf, D), lambda i, f: (f, 0))],
        out_specs=pl.BlockSpec((tm, D), lambda i, f: (i, 0)),
        scratch_shapes=[pltpu.VMEM((tm + 8, D), BF16), pltpu.VMEM((tm, D), F32)],
        compiler_params=_cparams(("arbitrary", "arbitrary")),
        name="conv_ffn",
    )(x2d, x2d, sc, sh, gf, g2.reshape(1, D), g3.reshape(1, D), w_up, w_up, conv_w, conv_w, cb, cb, w_down)


def _column_maps():
    n_rwkv = 3 * C_RWKV + R_DECAY + R_AAA + R_GATE
    n_fox = 3 * C_FOX + H_FOX
    fox0 = n_rwkv
    nsa0 = n_rwkv + n_fox
    pf = np.full(PF_COLS, -1, np.int64)
    pf[PF_R:PF_R + 3 * C_RWKV] = np.arange(3 * C_RWKV)
    pf[PF_XW:PF_XW + R_DECAY] = 3 * C_RWKV + np.arange(R_DECAY)
    pf[PF_XA:PF_XA + R_AAA] = 3 * C_RWKV + R_DECAY + np.arange(R_AAA)
    pf[PF_XG:PF_XG + R_GATE] = 3 * C_RWKV + R_DECAY + R_AAA + np.arange(R_GATE)
    pf[PF_MISC + MISC_F:PF_MISC + MISC_F + H_FOX] = fox0 + 3 * C_FOX + np.arange(H_FOX)
    nsa_gate0 = nsa0 + C_NSA + 6 * G_NSA * HEAD_DIM
    pf[PF_MISC + MISC_GATE:PF_MISC + MISC_GATE + 3 * H_NSA] = nsa_gate0 + np.arange(3 * H_NSA)

    pb = np.full(PB_COLS, -1, np.int64)
    scale = np.ones(PB_COLS, np.float32)
    d = np.arange(HEAD_DIM)
    for h in range(H_FOX):
        pb[PB_FQ + h * LANE + d] = fox0 + h * HEAD_DIM + d
        scale[PB_FQ + h * LANE + d] = HEAD_DIM ** -0.5
        pb[PB_FK + h * LANE + d] = fox0 + C_FOX + h * HEAD_DIM + d
    pb[PB_FV:PB_FV + C_FOX] = fox0 + 2 * C_FOX + np.arange(C_FOX)
    for h in range(H_NSA):
        pb[PB_NQ + h * LANE + d] = nsa0 + h * HEAD_DIM + d
        scale[PB_NQ + h * LANE + d] = HEAD_DIM ** -0.5
    ckv = G_NSA * HEAD_DIM
    kc0 = nsa0 + C_NSA
    pb[PB_KC:PB_KC + ckv] = kc0 + np.arange(ckv)
    pb[PB_VC:PB_VC + ckv] = kc0 + ckv + np.arange(ckv)
    for n, base in enumerate((PB_KS, PB_VS, PB_KW, PB_VW)):
        for g in range(G_NSA):
            pb[base + g * LANE + d] = kc0 + (2 + n) * ckv + g * HEAD_DIM + d
    return pf, pb, scale


def _pad_rows(w, rows):
    return jnp.concatenate([w, jnp.zeros((rows - w.shape[0],) + w.shape[1:], w.dtype)], axis=0)


def _compress_w1(w1):
    hid = w1.shape[1]
    w = w1.reshape(2, D_CMP, HEAD_DIM, hid)
    out = jnp.zeros((G_NSA, 2, D_CMP, G_NSA, HEAD_DIM, hid), w1.dtype)
    for g in range(G_NSA):
        out = out.at[g, :, :, g].set(w)
    return out.reshape(G_NSA, 2, D_CMP * G_NSA * HEAD_DIM, hid).astype(BF16)


def _compress_pe(pe):
    half = pe.reshape(2, 1, D_CMP, 1, HEAD_DIM)
    return jnp.broadcast_to(half, (2, 8, D_CMP, G_NSA, HEAD_DIM)).reshape(2, 8, D_CMP * G_NSA * HEAD_DIM)


def _fox_place():
    place = np.zeros((3, LANE, H_FOX * LANE), np.float32)
    for j in range(3):
        for h in range(H_FOX):
            place[j, MISC_F + h, h * LANE + HEAD_DIM + j] = 1.0
    return jnp.asarray(place, BF16)


def _overlap_t(S):
    nc = S // D_CMP
    nsb = S // L_SLC
    c0 = np.arange(nc) * D_CMP
    c1 = c0 + L_CMP - 1
    s0 = np.arange(nsb) * L_SLC
    ov = (c0[None, :] <= s0[:, None] + L_SLC - 1) & (c1[None, :] >= s0[:, None])
    ov[:, nc - 1] = False
    return jnp.asarray(ov.astype(np.float32), BF16)


def kernel(x, c, ada_w, ada_b, norm_g, w_in, rwkv_mu, rwkv_w0, rwkv_w_up, rwkv_a0, rwkv_a_up, rwkv_g_up, rwkv_k_k, rwkv_k_a, rwkv_r_k, rwkv_ln_w, rwkv_ln_b, fox_b_f, nsa_pe_k, nsa_pe_v, nsa_ck_w1, nsa_ck_w2, nsa_cv_w1, nsa_cv_w2, rel_bias, w_out, ffn_up, ffn_conv_w, ffn_conv_b, ffn_down):
    B, S, D = x.shape
    L = w_in.shape[0]
    tq_nsa = 128
    assert S % 512 == 0 and S // L_SLC <= HEAD_DIM and D == 1024

    pf_idx, pb_idx, pb_scale = _column_maps()
    w_ext = jnp.concatenate([w_in, jnp.zeros((L, D, 1), w_in.dtype)], axis=2)
    w_pf = jnp.take(w_ext, jnp.asarray(pf_idx), axis=2).astype(BF16)
    w_pb = (jnp.take(w_ext, jnp.asarray(pb_idx), axis=2) * pb_scale).astype(BF16)
    mu_ext = jnp.concatenate([rwkv_mu, jnp.zeros((L, 1), F32)], axis=1)
    mu_pf = jnp.take(mu_ext, jnp.asarray(pf_idx[:PF_RWKV]), axis=1)

    mod_all = _mod_call(c, ada_w, ada_b).reshape(L, B, 6, 1, D)
    bias_c, bias_n = _bias_tables(rel_bias, S, tq_nsa)
    place = _fox_place()
    overlap_t = _overlap_t(S)

    x2d = x.reshape(B * S, D)
    for l in range(L):
        sh_m, sc_m, g_m, sh_f, sc_f, g_f = (mod_all[l, :, j] for j in range(6))
        pf = _inproj_call(x2d, sc_m, sh_m, norm_g[l, 0], w_pf[l], F32, S)
        pb = _inproj_call(x2d, sc_m, sh_m, norm_g[l, 0], w_pb[l], BF16, S)
        pf3 = pf.reshape(B, S, PF_COLS)
        pb3 = pb.reshape(B, S, PB_COLS)

        ry, mg, bonus, gate = _rwkv_local_call(
            pf, mu_pf[l], rwkv_w0[l], _pad_rows(rwkv_w_up[l], LANE).astype(BF16), rwkv_a0[l],
            _pad_rows(rwkv_a_up[l], LANE).astype(BF16), _pad_rows(rwkv_g_up[l], LANE).astype(BF16),
            rwkv_k_k[l], rwkv_k_a[l], rwkv_r_k[l], S)
        ya = _rwkv_scan_call(ry, mg, bonus, gate, rwkv_ln_w[l], rwkv_ln_b[l], B, S)

        b_f_row = jnp.zeros((1, LANE), F32).at[0, MISC_F:MISC_F + H_FOX].set(fox_b_f[l])
        kaug, fox_vt = _fox_prep_call(pf3, pb3, b_f_row, place)
        yb = _fox_attn_call(pb3, kaug, fox_vt)

        kc_flat = pb3[:, :, PB_KC:PB_KC + LANE].reshape(B, S // D_CMP, D_CMP * LANE)
        vc_flat = pb3[:, :, PB_VC:PB_VC + LANE].reshape(B, S // D_CMP, D_CMP * LANE)
        w2pad = lambda w: jnp.concatenate([w, jnp.zeros_like(w)], axis=1).astype(BF16)
        kcmp, vcmpT = _nsa_compress_call(
            kc_flat, vc_flat, _compress_w1(nsa_ck_w1[l]), _compress_w1(nsa_cv_w1[l]),
            w2pad(nsa_ck_w2[l]), nsa_cv_w2[l].T.astype(BF16),
            _compress_pe(nsa_pe_k[l]), _compress_pe(nsa_pe_v[l]))
        ocmp, neg = _nsa_select_call(pb3, kcmp, vcmpT, bias_c, overlap_t, tq_nsa)
        yc = _nsa_attn_call(pb3, neg, ocmp, pf3, bias_n, tq_nsa)

        x2d = _outproj_call(x2d, ya.reshape(B * S, C_RWKV), yb.reshape(B * S, C_FOX),
                            yc.reshape(B * S, C_NSA), w_out[l].astype(BF16), g_m, norm_g[l, 1], S)
        x2d = _ffn_call(x2d, sc_f, sh_f, g_f, norm_g[l, 2], norm_g[l, 3], ffn_up[l].astype(BF16),
                        ffn_conv_w[l], ffn_conv_b[l], ffn_down[l].astype(BF16), S)
    return x2d.reshape(B, S, D)
```

```python
import functools
import math

import numpy as np
import jax
import jax.numpy as jnp
from jax import lax
from jax.experimental import pallas as pl
from jax.experimental.pallas import tpu as pltpu

F32 = jnp.float32
BF16 = jnp.bfloat16

HEAD_DIM = 64
H_RWKV = 4
C_RWKV = H_RWKV * HEAD_DIM
H_FOX = 6
C_FOX = H_FOX * HEAD_DIM
H_NSA = 6
C_NSA = H_NSA * HEAD_DIM
G_NSA = 2
HPG = H_NSA // G_NSA
R_DECAY = 32
R_AAA = 32
R_GATE = 64
L_CMP = 32
D_CMP = 16
CMP_HID = 128
L_SLC = 64
N_SLC = 16
N_LOCAL = 2
WINDOW = 512
NUM_BUCKETS = 32
MAX_DISTANCE = 128
CONV_W = 3
RMS_EPS = 1e-6
GN_EPS = 64e-5
NEG_INF = -1e30
FORCE_SCORE = 1e4

LANE = 128
CHUNK = 64
VMEM_LIMIT = 56 * 1024 * 1024

PF_R, PF_K, PF_V, PF_XW, PF_XA, PF_XG, PF_MISC = 0, 256, 512, 768, 896, 1024, 1152
PF_RWKV = 1152
PF_COLS = 1280
MISC_F = 0
MISC_GATE = 8
PB_FQ, PB_FK, PB_NQ = 0, 768, 1536
PB_FV, PB_KC, PB_VC = 2304, 2688, 2816
PB_KS, PB_VS, PB_KW, PB_VW = 3072, 3328, 3584, 3840
PB_COLS = 4096


def _bucket_lower_bounds():
    n = np.arange(0, 4 * MAX_DISTANCE, dtype=np.int64)
    max_exact = NUM_BUCKETS // 2
    nf = np.maximum(n, 1).astype(np.float32)
    large = max_exact + (np.log(nf / np.float32(max_exact)) / np.float32(math.log(MAX_DISTANCE / max_exact))
                         * np.float32(NUM_BUCKETS - max_exact)).astype(np.int32)
    large = np.minimum(large, NUM_BUCKETS - 1)
    bucket = np.where(n < max_exact, n, large)
    return [int(np.argmax(bucket >= b)) for b in range(NUM_BUCKETS)]


BUCKET_LB = _bucket_lower_bounds()
BIAS_CONST_DIST = BUCKET_LB[NUM_BUCKETS - 1]


def _cparams(sem, vmem=None):
    return pltpu.CompilerParams(dimension_semantics=sem, vmem_limit_bytes=vmem or VMEM_LIMIT)


def _dot(a, b):
    return jnp.dot(a, b, preferred_element_type=F32)


def _dot_nt(a, b):
    return lax.dot_general(a, b, (((1,), (1,)), ((), ())), preferred_element_type=F32)


def _split_bf16(x, n):
    parts, r = [], x
    for i in range(n):
        p = r.astype(BF16)
        parts.append(p)
        if i + 1 < n:
            r = r - p.astype(F32)
    return parts


def _dot_hp(a, b, nt=False):
    f = _dot_nt if nt else _dot
    ah, al = _split_bf16(a, 2)
    bh, bl = _split_bf16(b, 2)
    return f(ah, bh) + (f(ah, bl) + f(al, bh))


def _dot_exact_lhs(a_bf16, b, n):
    out = None
    for p in _split_bf16(b, n):
        t = _dot(a_bf16, p)
        out = t if out is None else out + t
    return out


def _softplus(x):
    return jnp.maximum(x, 0.0) + jnp.log(1.0 + jnp.exp(-jnp.abs(x)))


def _sigmoid(x):
    return 1.0 / (1.0 + jnp.exp(-x))


def _rmsnorm(x, g):
    return x * lax.rsqrt(jnp.mean(x * x, axis=-1, keepdims=True) + RMS_EPS) * g


def _mod_kernel(c_ref, w_ref, b_ref, o_ref):
    c = c_ref[...]
    s = (c * _sigmoid(c)).astype(BF16)
    o_ref[0] = _dot(s, w_ref[0].astype(BF16)) + b_ref[0]


def _mod_call(c, ada_w, ada_b):
    L, D, N = ada_w.shape
    B = c.shape[0]
    tn = 1536
    return pl.pallas_call(
        _mod_kernel,
        out_shape=jax.ShapeDtypeStruct((L, B, N), F32),
        grid=(L, N // tn),
        in_specs=[pl.BlockSpec((B, D), lambda l, j: (0, 0)),
                  pl.BlockSpec((1, D, tn), lambda l, j: (l, 0, j)),
                  pl.BlockSpec((1, 1, tn), lambda l, j: (l, 0, j))],
        out_specs=pl.BlockSpec((1, B, tn), lambda l, j: (l, 0, j)),
        compiler_params=_cparams(("arbitrary", "arbitrary")),
        name="adaln_mod",
    )(c, ada_w, ada_b.reshape(L, 1, N))


def _inproj_kernel(x_ref, sc_ref, sh_ref, g_ref, w_ref, o_ref, *, n_chunk):
    h = _rmsnorm(x_ref[...], g_ref[...]) * (1.0 + sc_ref[0]) + sh_ref[0]
    h = h.astype(BF16)
    n = o_ref.shape[1]
    for n0 in range(0, n, n_chunk):
        n1 = min(n, n0 + n_chunk)
        o_ref[:, n0:n1] = _dot(h, w_ref[:, n0:n1]).astype(o_ref.dtype)


def _inproj_call(x2d, sc, sh, g, w, out_dtype, S, tm=512):
    N, D = x2d.shape
    C = w.shape[1]
    per = S // tm
    return pl.pallas_call(
        functools.partial(_inproj_kernel, n_chunk=512),
        out_shape=jax.ShapeDtypeStruct((N, C), out_dtype),
        grid=(N // tm,),
        in_specs=[pl.BlockSpec((tm, D), lambda i: (i, 0)),
                  pl.BlockSpec((1, 1, D), lambda i: (i // per, 0, 0)),
                  pl.BlockSpec((1, 1, D), lambda i: (i // per, 0, 0)),
                  pl.BlockSpec((1, D), lambda i: (0, 0)),
                  pl.BlockSpec((D, C), lambda i: (0, 0))],
        out_specs=pl.BlockSpec((tm, C), lambda i: (i, 0)),
        compiler_params=_cparams(("arbitrary",)),
        name="in_proj",
    )(x2d, sc, sh, g.reshape(1, D), w)


def _rwkv_local_kernel(p_ref, halo_ref, mu_ref, w0_ref, wup_ref, a0_ref, aup_ref, gup_ref,
                       kk_ref, ka_ref, rk_ref,
                       ry_ref, mg_ref, bonus_ref, gate_ref, *, tr, rows_per_seq):
    C = C_RWKV
    i = pl.program_id(0)
    first = (i * tr) % rows_per_seq == 0
    p = p_ref[:, :PF_RWKV]
    row = lax.broadcasted_iota(jnp.int32, (tr, 1), 0)
    prev_last = jnp.where(first, 0.0, halo_ref[7:8, :PF_RWKV])
    prev = jnp.where(row == 0, prev_last, pltpu.roll(p, 1, 0))
    ps = p + (prev - p) * mu_ref[...]
    r = ps[:, PF_R:PF_R + C]
    k = ps[:, PF_K:PF_K + C]
    v = ps[:, PF_V:PF_V + C]
    xw = ps[:, PF_XW:PF_XW + LANE]
    xa = ps[:, PF_XA:PF_XA + LANE]
    xg = ps[:, PF_XG:PF_XG + LANE]

    wl = w0_ref[...] + _dot(jnp.tanh(xw).astype(BF16), wup_ref[...])
    lw = -jnp.exp(-_softplus(-wl) - 0.5)
    a = _sigmoid(a0_ref[...] + _dot(xa.astype(BF16), aup_ref[...]))
    gate_ref[...] = _dot(_sigmoid(xg).astype(BF16), gup_ref[...])

    li = lax.broadcasted_iota(jnp.int32, (C, C), 0) // HEAD_DIM
    lj = lax.broadcasted_iota(jnp.int32, (C, C), 1) // HEAD_DIM
    same_head = li == lj
    head_ones = jnp.where(same_head, 1.0, 0.0).astype(BF16)

    kk = k * kk_ref[...]
    nrm = jnp.sqrt(_dot_exact_lhs_t(kk * kk, head_ones))
    kk = kk / jnp.maximum(nrm, 1e-12)
    k2 = k * (1.0 + (a - 1.0) * ka_ref[...])
    bonus_ref[...] = _dot_exact_lhs_t(r * k2 * rk_ref[...], head_ones) * v
    avec = -kk
    bvec = kk * a

    nch = tr // CHUNK
    ti = lax.broadcasted_iota(jnp.int32, (tr, tr), 0)
    tj = lax.broadcasted_iota(jnp.int32, (tr, tr), 1)
    same_chunk = ti // CHUNK == tj // CHUNK
    tri_incl = jnp.where(same_chunk & (tj <= ti), 1.0, 0.0).astype(BF16)
    chunk_ones = jnp.where(same_chunk, 1.0, 0.0).astype(BF16)
    Lc = _dot_exact_lhs(tri_incl, lw, 3)
    Lend = _dot_exact_lhs(chunk_ones, lw, 3)
    e_cur = jnp.exp(Lc)
    e_inv = jnp.exp(-Lc)
    e_end = jnp.exp(Lend - Lc)
    At = avec * jnp.exp(Lc - lw)
    Rt = r * e_cur
    Bt = bvec * e_inv
    Kt = k2 * e_inv
    bh_t = (bvec * e_end).T
    kh_t = (k2 * e_end).T
    pc = jnp.exp(Lend)

    pairs = [(c, h) for c in range(nch) for h in range(H_RWKV)]
    blk = lambda x: jnp.stack([x[c * CHUNK:(c + 1) * CHUNK, h * HEAD_DIM:(h + 1) * HEAD_DIM]
                               for c, h in pairs]).astype(BF16)
    blk_t = lambda x: jnp.stack([x[h * HEAD_DIM:(h + 1) * HEAD_DIM, c * CHUNK:(c + 1) * CHUNK]
                                 for c, h in pairs]).astype(BF16)
    bmm = lambda a, b: jnp.einsum('nij,njk->nik', a.astype(BF16), b.astype(BF16), preferred_element_type=F32)
    bmm_nt = lambda a, b: jnp.einsum('nid,nkd->nik', a, b, preferred_element_type=F32)

    ci = lax.broadcasted_iota(jnp.int32, (CHUNK, CHUNK), 0)
    cj = lax.broadcasted_iota(jnp.int32, (CHUNK, CHUNK), 1)
    lower_strict = cj < ci
    lower_incl = cj <= ci
    a_b, r_b, b_b, k_b, v_b = blk(At), blk(Rt), blk(Bt), blk(Kt), blk(v)
    ar = jnp.concatenate([a_b, r_b], axis=1)
    ab = bmm_nt(ar, b_b)
    ak = bmm_nt(ar, k_b)
    n_mat = jnp.where(lower_strict, ab[:, :CHUNK], 0.0)
    a_ak = jnp.where(lower_strict, ak[:, :CHUNK], 0.0)
    a_rb = jnp.where(lower_incl, ab[:, CHUNK:], 0.0)
    a_rk = jnp.where(lower_incl, ak[:, CHUNK:], 0.0)
    av = bmm(jnp.concatenate([a_ak, a_rk], axis=1), v_b)
    x = jnp.concatenate([a_b.astype(F32), av[:, :CHUNK]], axis=2)
    x = x + bmm(n_mat, x)
    for _ in range(5):
        n_mat = bmm(n_mat, n_mat)
        x = x + bmm(n_mat, x)
    corr = bmm(a_rb, x)
    rbar = r_b.astype(F32) + corr[:, :, :HEAD_DIM]
    y0 = av[:, CHUNK:] + corr[:, :, HEAD_DIM:]
    m_mat = bmm(blk_t(bh_t), x)
    g_add = bmm(blk_t(kh_t), v_b)
    eye = ci == cj
    zero = jnp.zeros((HEAD_DIM, HEAD_DIM), F32)
    for c in range(nch):
        sl = slice(c * CHUNK, (c + 1) * CHUNK)
        ns = [c * H_RWKV + h for h in range(H_RWKV)]
        ry_ref[sl, :C] = jnp.concatenate([rbar[n] for n in ns], axis=1)
        ry_ref[sl, C:] = jnp.concatenate([y0[n] for n in ns], axis=1)
        for h, n in enumerate(ns):
            hs = slice(h * HEAD_DIM, (h + 1) * HEAD_DIM)
            pc_h = pc[c * CHUNK:c * CHUNK + 1, hs]
            m_h = jnp.where(eye, pc_h, 0.0) + m_mat[n, :, :HEAD_DIM]
            g_h = m_mat[n, :, HEAD_DIM:] + g_add[n]
            mg_ref[c, hs, :C] = jnp.concatenate([m_h if j == h else zero for j in range(H_RWKV)], axis=1)
            mg_ref[c, hs, C:] = jnp.concatenate([g_h if j == h else zero for j in range(H_RWKV)], axis=1)


def _dot_exact_lhs_t(x, ones_bf16):
    xh, xl = _split_bf16(x, 2)
    return _dot(xh, ones_bf16) + _dot(xl, ones_bf16)


def _rwkv_local_call(pf, mu, w0, wup, a0, aup, gup, k_k, k_a, r_k, S, tr=256):
    N = pf.shape[0]
    C = C_RWKV
    row = lambda a: a.reshape(1, -1)
    full = lambda a: pl.BlockSpec(a.shape, lambda i: (0,) * a.ndim)
    args = [row(mu), row(w0), wup, row(a0), aup, gup, row(k_k), row(k_a), row(r_k)]
    return pl.pallas_call(
        functools.partial(_rwkv_local_kernel, tr=tr, rows_per_seq=S),
        out_shape=(jax.ShapeDtypeStruct((N, 2 * C), F32),
                   jax.ShapeDtypeStruct((N // CHUNK, C, 2 * C), F32),
                   jax.ShapeDtypeStruct((N, C), F32),
                   jax.ShapeDtypeStruct((N, C), F32)),
        grid=(N // tr,),
        in_specs=[pl.BlockSpec((tr, PF_COLS), lambda i: (i, 0)),
                  pl.BlockSpec((8, PF_COLS), lambda i: (jnp.maximum(i * (tr // 8) - 1, 0), 0))]
                 + [full(a) for a in args],
        out_specs=(pl.BlockSpec((tr, 2 * C), lambda i: (i, 0)),
                   pl.BlockSpec((tr // CHUNK, C, 2 * C), lambda i: (i, 0, 0)),
                   pl.BlockSpec((tr, C), lambda i: (i, 0)),
                   pl.BlockSpec((tr, C), lambda i: (i, 0))),
        compiler_params=_cparams(("arbitrary",)),
        name="rwkv_local",
    )(pf, pf, *args)


def _rwkv_scan_kernel(ry_ref, mg_ref, bonus_ref, gate_ref, lnw_ref, lnb_ref, o_ref, s_scr):
    C = C_RWKV
    B = ry_ref.shape[0]

    @pl.when(pl.program_id(0) == 0)
    def _():
        s_scr[...] = jnp.zeros_like(s_scr)

    li = lax.broadcasted_iota(jnp.int32, (C, C), 0) // HEAD_DIM
    lj = lax.broadcasted_iota(jnp.int32, (C, C), 1) // HEAD_DIM
    head_avg = jnp.where(li == lj, 1.0 / HEAD_DIM, 0.0).astype(BF16)
    for b in range(B):
        s0 = s_scr[b]
        y = _dot_hp(ry_ref[b, :, :C], s0) + ry_ref[b, :, C:]
        s_scr[b] = _dot_hp(mg_ref[b, 0, :, :C], s0) + mg_ref[b, 0, :, C:]
        mean = _dot_exact_lhs_t(y, head_avg)
        d = y - mean
        var = _dot_exact_lhs_t(d * d, head_avg)
        yn = d * lax.rsqrt(var + GN_EPS) * lnw_ref[...] + lnb_ref[...]
        o_ref[b] = ((yn + bonus_ref[b]) * gate_ref[b]).astype(o_ref.dtype)


def _rwkv_scan_call(ry, mg, bonus, gate, ln_w, ln_b, B, S):
    C = C_RWKV
    nc = S // CHUNK
    return pl.pallas_call(
        _rwkv_scan_kernel,
        out_shape=jax.ShapeDtypeStruct((B, S, C), BF16),
        grid=(nc,),
        in_specs=[pl.BlockSpec((B, CHUNK, 2 * C), lambda c: (0, c, 0)),
                  pl.BlockSpec((B, 1, C, 2 * C), lambda c: (0, c, 0, 0)),
                  pl.BlockSpec((B, CHUNK, C), lambda c: (0, c, 0)),
                  pl.BlockSpec((B, CHUNK, C), lambda c: (0, c, 0)),
                  pl.BlockSpec((1, C), lambda c: (0, 0)),
                  pl.BlockSpec((1, C), lambda c: (0, 0))],
        out_specs=pl.BlockSpec((B, CHUNK, C), lambda c: (0, c, 0)),
        scratch_shapes=[pltpu.VMEM((B, C, C), F32)],
        compiler_params=_cparams(("arbitrary",)),
        name="rwkv_scan",
    )(ry.reshape(B, S, 2 * C), mg.reshape(B, nc, C, 2 * C), bonus.reshape(B, S, C),
      gate.reshape(B, S, C), ln_w.reshape(1, C), ln_b.reshape(1, C))


def _online_update(carry, s, v):
    m, l, acc = carry
    m_new = jnp.maximum(m, jnp.max(s, axis=-1, keepdims=True))
    alpha = jnp.exp(m - m_new)
    p = jnp.exp(s - m_new)
    l = alpha * l + jnp.sum(p, axis=-1, keepdims=True)
    acc = alpha * acc + _dot(p.astype(BF16), v)
    return m_new, l, acc


def _attn_init(rows):
    return (jnp.full((rows, 1), NEG_INF, F32), jnp.zeros((rows, 1), F32), jnp.zeros((rows, LANE), F32))


def _online_update_t(carry, s_t, v_t):
    m, l, acc = carry
    m_new = jnp.maximum(m, jnp.max(s_t, axis=0, keepdims=True))
    alpha = jnp.exp(m - m_new)
    p = jnp.exp(s_t - m_new)
    l = alpha * l + jnp.sum(p, axis=0, keepdims=True)
    acc = alpha * acc + _dot(v_t, p.astype(BF16))
    return m_new, l, acc


def _attn_init_t(d, cols):
    return (jnp.full((1, cols), NEG_INF, F32), jnp.zeros((1, cols), F32), jnp.zeros((d, cols), F32))


def _online_update_multi(carries, ss, vts):
    n = range(len(ss))
    ms = [jnp.maximum(carries[i][0], jnp.max(ss[i], axis=0, keepdims=True)) for i in n]
    ps = [jnp.exp(ss[i] - ms[i]) for i in n]
    pvs = [_dot(vts[i], ps[i].astype(BF16)) for i in n]
    out = []
    for i in n:
        m, l, acc = carries[i]
        alpha = jnp.exp(m - ms[i])
        out.append((ms[i], alpha * l + jnp.sum(ps[i], axis=0, keepdims=True), alpha * acc + pvs[i]))
    return tuple(out)


def _fox_prep_kernel(misc_ref, k_ref, v_ref, bf_ref, place_ref, o_ref, vt_ref, carry_scr, *, t, tk):
    @pl.when(pl.program_id(1) == 0)
    def _():
        carry_scr[...] = jnp.zeros_like(carry_scr)

    for hp in range(H_FOX // 2):
        v_t = v_ref[0, :, hp * LANE:(hp + 1) * LANE].astype(F32).T
        for j in range(t // tk):
            for hl in range(2):
                vt_ref[0, 2 * hp + hl, j] = v_t[hl * HEAD_DIM:(hl + 1) * HEAD_DIM,
                                                j * tk:(j + 1) * tk].astype(BF16)

    lf = -_softplus(-(misc_ref[0] + bf_ref[...]))
    ti = lax.broadcasted_iota(jnp.int32, (t, t), 0)
    tj = lax.broadcasted_iota(jnp.int32, (t, t), 1)
    tri = jnp.where(tj <= ti, 1.0, 0.0).astype(BF16)
    cum = _dot_exact_lhs(tri, lf, 3) + carry_scr[...]
    carry_scr[...] = cum[t - 1:t]
    parts = _split_bf16(-cum, 3)
    out = k_ref[0].astype(F32)
    for j, part in enumerate(parts):
        out = out + _dot(part, place_ref[j])
    o_ref[0] = out.astype(BF16)


FOX_TK = 256


def _fox_prep_call(pf3, pb3, b_f_row, place, t=512):
    B, S, _ = pf3.shape
    W = H_FOX * LANE
    tk = FOX_TK
    return pl.pallas_call(
        functools.partial(_fox_prep_kernel, t=t, tk=tk),
        out_shape=(jax.ShapeDtypeStruct((B, S, W), BF16),
                   jax.ShapeDtypeStruct((B, H_FOX, S // tk, HEAD_DIM, tk), BF16)),
        grid=(B, S // t),
        in_specs=[pl.BlockSpec((1, t, LANE), lambda b, i: (b, i, PF_MISC // LANE)),
                  pl.BlockSpec((1, t, W), lambda b, i: (b, i, PB_FK // W)),
                  pl.BlockSpec((1, t, C_FOX), lambda b, i: (b, i, PB_FV // C_FOX)),
                  pl.BlockSpec((1, LANE), lambda b, i: (0, 0)),
                  pl.BlockSpec((3, LANE, W), lambda b, i: (0, 0, 0))],
        out_specs=(pl.BlockSpec((1, t, W), lambda b, i: (b, i, 0)),
                   pl.BlockSpec((1, H_FOX, t // tk, HEAD_DIM, tk), lambda b, i: (b, 0, i, 0, 0))),
        scratch_shapes=[pltpu.VMEM((1, LANE), F32)],
        compiler_params=_cparams(("arbitrary", "arbitrary")),
        name="fox_prep",
    )(pf3, pb3, pb3, b_f_row, place)


def _fox_attn_kernel(q_ref, k_ref, vt_ref, o_ref, *, tq, tk):
    qi = pl.program_id(1)
    lane = lax.broadcasted_iota(jnp.int32, (1, LANE), 1)
    ones3 = jnp.where((lane >= HEAD_DIM) & (lane < HEAD_DIM + 3), 1.0, 0.0).astype(BF16)
    krow = lax.broadcasted_iota(jnp.int32, (tk, tq), 0)
    qcol = lax.broadcasted_iota(jnp.int32, (tk, tq), 1)
    per = tq // tk
    heads = range(H_FOX)
    qs = [q_ref[0, :, h * LANE:(h + 1) * LANE] + ones3 for h in heads]

    def logits(j, h):
        start = pl.multiple_of(j * tk, tk)
        return _dot_nt(k_ref[0, pl.ds(start, tk), h * LANE:(h + 1) * LANE], qs[h])

    carries = tuple(_attn_init_t(HEAD_DIM, tq) for _ in heads)
    for d in range(per):
        j = qi * per + d
        ss = [jnp.where(krow + d * tk <= qcol, logits(j, h), NEG_INF) for h in heads]
        carries = _online_update_multi(carries, ss, [vt_ref[0, h, j] for h in heads])

    def body(j, carries):
        return _online_update_multi(carries, [logits(j, h) for h in heads], [vt_ref[0, h, j] for h in heads])

    carries = lax.fori_loop(0, qi * per, body, carries)
    outs = [acc / l for (_, l, acc) in carries]
    for hp in range(H_FOX // 2):
        pair = jnp.concatenate([outs[2 * hp], outs[2 * hp + 1]], axis=0)
        o_ref[0, :, hp * LANE:(hp + 1) * LANE] = pair.T.astype(o_ref.dtype)


def _fox_attn_call(pb3, kaug, v_t, tq=256):
    B, S, _ = pb3.shape
    tk = FOX_TK
    W = H_FOX * LANE
    return pl.pallas_call(
        functools.partial(_fox_attn_kernel, tq=tq, tk=tk),
        out_shape=jax.ShapeDtypeStruct((B, S, C_FOX), BF16),
        grid=(B, S // tq),
        in_specs=[pl.BlockSpec((1, tq, W), lambda b, i: (b, i, PB_FQ // W)),
                  pl.BlockSpec((1, S, W), lambda b, i: (b, 0, 0)),
                  pl.BlockSpec((1, H_FOX, S // tk, HEAD_DIM, tk), lambda b, i: (b, 0, 0, 0, 0))],
        out_specs=pl.BlockSpec((1, tq, C_FOX), lambda b, i: (b, i, 0)),
        compiler_params=_cparams(("arbitrary", "arbitrary")),
        name="fox_attn",
    )(pb3, kaug, v_t)


def _bias_of_dist(n, tab_ref, h):
    val = jnp.zeros(n.shape, F32) + tab_ref[0, h]
    for b in range(1, NUM_BUCKETS):
        val = jnp.where(n >= BUCKET_LB[b], tab_ref[b, h], val)
    return val - tab_ref[NUM_BUCKETS - 1, h]


def _bias_cmp_kernel(tab_ref, o_ref, *, tt):
    t0 = pl.program_id(0) * tt
    nc = o_ref.shape[1]
    c = lax.broadcasted_iota(jnp.int32, (nc, tt), 0)
    t = lax.broadcasted_iota(jnp.int32, (nc, tt), 1) + t0
    n = jnp.maximum(t - (c * D_CMP + L_CMP - 1), 0)
    for h in range(H_NSA):
        o_ref[h] = _bias_of_dist(n, tab_ref, h)


def _bias_near_kernel(tab_ref, o_ref, *, tq):
    j = lax.broadcasted_iota(jnp.int32, (tq, tq), 0)
    i = lax.broadcasted_iota(jnp.int32, (tq, tq), 1)
    for g in range(G_NSA):
        for near in range(2):
            n = jnp.maximum(i - j + near * tq, 0)
            for h in range(HPG):
                o_ref[g, near, :, h * tq:(h + 1) * tq] = _bias_of_dist(n, tab_ref, g * HPG + h)


def _bias_tables(rel_bias, S, tq):
    nc = S // D_CMP
    tt = 512
    smem = pl.BlockSpec(memory_space=pltpu.SMEM)
    bias_c = pl.pallas_call(
        functools.partial(_bias_cmp_kernel, tt=tt),
        out_shape=jax.ShapeDtypeStruct((H_NSA, nc, S), F32),
        grid=(S // tt,),
        in_specs=[smem],
        out_specs=pl.BlockSpec((H_NSA, nc, tt), lambda i: (0, 0, i)),
        compiler_params=_cparams(("arbitrary",)),
        name="nsa_bias_cmp",
    )(rel_bias)
    bias_n = pl.pallas_call(
        functools.partial(_bias_near_kernel, tq=tq),
        out_shape=jax.ShapeDtypeStruct((G_NSA, 2, tq, HPG * tq), F32),
        in_specs=[smem],
        name="nsa_bias_near",
    )(rel_bias)
    return bias_c, bias_n


def _gelu_tanh(x):
    return 0.5 * x * (1.0 + jnp.tanh(math.sqrt(2.0 / math.pi) * (x + 0.044715 * (x * x * x))))


def _nsa_compress_kernel(kc_ref, vc_ref, wk1_ref, wv1_ref, wk2_ref, wv2_ref, pek_ref, pev_ref,
                         kcmp_ref, vcmpT_ref):
    nc = kc_ref.shape[1]
    for g in range(G_NSA):
        for src_ref, w1_ref, w2_ref, pe_ref, is_k in ((kc_ref, wk1_ref, wk2_ref, pek_ref, True),
                                                      (vc_ref, wv1_ref, wv2_ref, pev_ref, False)):
            ch = src_ref[0]
            p1 = _dot(ch, w1_ref[g, 0])
            p2 = _dot(ch, w1_ref[g, 1])
            pec = (_dot(pe_ref[0], w1_ref[g, 0].astype(F32)) + _dot(pe_ref[1], w1_ref[g, 1].astype(F32)))[0:1]
            hid = p1 + pltpu.roll(p2, nc - 1, 0) + pec
            act = _gelu_tanh(hid).astype(BF16)
            if is_k:
                kcmp_ref[0, g] = _dot(act, w2_ref[...]).astype(BF16)
            else:
                vcmpT_ref[0, g] = _dot_nt(w2_ref[...], act).astype(BF16)


def _nsa_compress_call(kc_flat, vc_flat, wk1, wv1, wk2, wv2, pek, pev):
    B, nc, W = kc_flat.shape
    full = lambda a: pl.BlockSpec(a.shape, lambda b: (0,) * a.ndim)
    return pl.pallas_call(
        _nsa_compress_kernel,
        out_shape=(jax.ShapeDtypeStruct((B, G_NSA, nc, LANE), BF16),
                   jax.ShapeDtypeStruct((B, G_NSA, HEAD_DIM, nc), BF16)),
        grid=(B,),
        in_specs=[pl.BlockSpec((1, nc, W), lambda b: (b, 0, 0)),
                  pl.BlockSpec((1, nc, W), lambda b: (b, 0, 0)),
                  full(wk1), full(wv1), full(wk2), full(wv2), full(pek), full(pev)],
        out_specs=(pl.BlockSpec((1, G_NSA, nc, LANE), lambda b: (b, 0, 0, 0)),
                   pl.BlockSpec((1, G_NSA, HEAD_DIM, nc), lambda b: (b, 0, 0, 0))),
        compiler_params=_cparams(("arbitrary",)),
        name="nsa_compress",
    )(kc_flat, vc_flat, wk1, wv1, wk2, wv2, pek, pev)


def _nsa_select_kernel(q_ref, kcmp_ref, vcmpT_ref, bias_ref, ov_ref, ocmp_ref, neg_ref, *, tq):
    qi = pl.program_id(1)
    nc = kcmp_ref.shape[2]
    nsb = ov_ref.shape[0]
    t = lax.broadcasted_iota(jnp.int32, (1, tq), 1) + qi * tq
    cidx = lax.broadcasted_iota(jnp.int32, (nc, 1), 0)
    valid_c = (cidx * D_CMP + L_CMP - 1) <= t
    jf = lax.broadcasted_iota(jnp.int32, (nsb, tq), 0).astype(F32)
    jb = lax.broadcasted_iota(jnp.int32, (nsb, 1), 0)
    back = t // L_SLC - jb
    valid_b = back >= 0
    forced = (jb == 0) | (valid_b & (back < N_LOCAL))
    for g in range(G_NSA):
        imp = jnp.zeros((nsb, tq), F32)
        for h in range(HPG):
            hh = g * HPG + h
            q = q_ref[0, :, hh * LANE:(hh + 1) * LANE]
            s = _dot_nt(kcmp_ref[0, g], q) + bias_ref[hh]
            s = jnp.where(valid_c, s, NEG_INF)
            m = jnp.max(s, axis=0, keepdims=True)
            p = jnp.where(valid_c, jnp.exp(s - m), 0.0)
            l = jnp.sum(p, axis=0, keepdims=True)
            p = (p / jnp.maximum(l, 1e-30)).astype(BF16)
            ocmp_ref[0, hh] = _dot(vcmpT_ref[0, g], p)
            imp = imp + _dot(ov_ref[...], p)
        score = jnp.where(valid_b, jnp.where(forced, FORCE_SCORE, imp), -1.0)
        sel = jnp.zeros((nsb, tq), F32)
        for _ in range(N_SLC):
            mx = jnp.max(score, axis=0, keepdims=True)
            first = jnp.min(jnp.where(score == mx, jf, float(nsb)), axis=0, keepdims=True)
            pick = jf == first
            sel = jnp.where(pick & (mx >= 0.0), 1.0, sel)
            score = jnp.where(pick, -2.0, score)
        neg = jnp.where(sel > 0.0, 0.0, NEG_INF)
        pieces = [jnp.zeros((HEAD_DIM, tq), F32), neg]
        if nsb < HEAD_DIM:
            pieces.append(jnp.zeros((HEAD_DIM - nsb, tq), F32))
        neg_ref[0, g] = jnp.concatenate(pieces, axis=0).T.astype(BF16)


def _nsa_select_call(pb3, kcmp, vcmpT, bias_c, overlap_t, tq):
    B, S, _ = pb3.shape
    nc = kcmp.shape[2]
    W = H_NSA * LANE
    return pl.pallas_call(
        functools.partial(_nsa_select_kernel, tq=tq),
        out_shape=(jax.ShapeDtypeStruct((B, H_NSA, HEAD_DIM, S), F32),
                   jax.ShapeDtypeStruct((B, G_NSA, S, LANE), BF16)),
        grid=(B, S // tq),
        in_specs=[pl.BlockSpec((1, tq, W), lambda b, i: (b, i, PB_NQ // W)),
                  pl.BlockSpec((1, G_NSA, nc, LANE), lambda b, i: (b, 0, 0, 0)),
                  pl.BlockSpec((1, G_NSA, HEAD_DIM, nc), lambda b, i: (b, 0, 0, 0)),
                  pl.BlockSpec((H_NSA, nc, tq), lambda b, i: (0, 0, i)),
                  pl.BlockSpec(overlap_t.shape, lambda b, i: (0, 0))],
        out_specs=(pl.BlockSpec((1, H_NSA, HEAD_DIM, tq), lambda b, i: (b, 0, 0, i)),
                   pl.BlockSpec((1, G_NSA, tq, LANE), lambda b, i: (b, 0, i, 0))),
        compiler_params=_cparams(("arbitrary", "arbitrary")),
        name="nsa_select",
    )(pb3, kcmp, vcmpT, bias_c, overlap_t)


def _nsa_attn_kernel(q_ref, neg_ref, ks_ref, vs_ref, kw_ref, vw_ref, ocmp_ref, misc_ref, bias_ref,
                     o_ref, kaug_scr, vst_scr, vwt_scr, *, tq):
    qi = pl.program_id(1)
    S = ks_ref.shape[1]
    cols = HPG * tq
    n_tiles = S // tq
    assert WINDOW == 2 * tq

    @pl.when(qi == 0)
    def _():
        srow = lax.broadcasted_iota(jnp.int32, (tq, LANE), 0)
        slane = lax.broadcasted_iota(jnp.int32, (tq, LANE), 1)

        def fill(t, carry):
            start = pl.multiple_of(t * tq, tq)
            onehot = jnp.where(slane == HEAD_DIM + (srow + t * tq) // L_SLC, 1.0, 0.0).astype(BF16)
            for g in range(G_NSA):
                gl = slice(g * LANE, (g + 1) * LANE)
                kaug_scr[g, pl.ds(start, tq), :] = ks_ref[0, pl.ds(start, tq), gl] + onehot
                vst_scr[g, t] = vs_ref[0, pl.ds(start, tq), gl].astype(F32).T[:HEAD_DIM].astype(BF16)
                vwt_scr[g, t] = vw_ref[0, pl.ds(start, tq), gl].astype(F32).T[:HEAD_DIM].astype(BF16)
            return carry

        lax.fori_loop(0, n_tiles, fill, 0)

    jk = lax.broadcasted_iota(jnp.int32, (tq, cols), 0)
    ic = lax.broadcasted_iota(jnp.int32, (tq, cols), 1) % tq
    causal = jk <= ic
    in_window = jk > ic
    prev = jnp.maximum(qi - 1, 0)
    prev2 = jnp.maximum(qi - 2, 0)
    has_prev = qi >= 1
    has_prev2 = qi >= 2
    groups = range(G_NSA)
    qa = [jnp.concatenate([q_ref[0, :, (g * HPG + h) * LANE:(g * HPG + h + 1) * LANE] + neg_ref[0, g]
                           for h in range(HPG)], axis=0) for g in groups]

    def sel_logits(j, g):
        start = pl.multiple_of(j * tq, tq)
        return _dot_nt(kaug_scr[g, pl.ds(start, tq), :], qa[g])

    def win_logits(j, g):
        start = pl.multiple_of(j * tq, tq)
        return _dot_nt(kw_ref[0, pl.ds(start, tq), g * LANE:(g + 1) * LANE], qa[g])

    carries = tuple(_attn_init_t(HEAD_DIM, cols) for _ in range(2 * G_NSA))
    ss = ([jnp.where(causal, sel_logits(qi, g) + bias_ref[g, 0], NEG_INF) for g in groups]
          + [jnp.where(causal, win_logits(qi, g) + bias_ref[g, 0], NEG_INF) for g in groups])
    carries = _online_update_multi(carries, ss, [vst_scr[g, qi] for g in groups] + [vwt_scr[g, qi] for g in groups])
    ss = ([jnp.where(has_prev, sel_logits(prev, g) + bias_ref[g, 1], NEG_INF) for g in groups]
          + [jnp.where(has_prev, win_logits(prev, g) + bias_ref[g, 1], NEG_INF) for g in groups])
    carries = _online_update_multi(carries, ss,
                                   [vst_scr[g, prev] for g in groups] + [vwt_scr[g, prev] for g in groups])
    ss = [jnp.where(has_prev2 & in_window, win_logits(prev2, g), NEG_INF) for g in groups]
    win = _online_update_multi(carries[G_NSA:], ss, [vwt_scr[g, prev2] for g in groups])

    def body(j, c):
        return _online_update_multi(c, [sel_logits(j, g) for g in groups], [vst_scr[g, j] for g in groups])

    sel = lax.fori_loop(0, prev, body, carries[:G_NSA])

    gates = _sigmoid(misc_ref[0].T)
    outs = []
    for g in groups:
        o_slc = sel[g][2] / sel[g][1]
        o_win = win[g][2] / win[g][1]
        for h in range(HPG):
            hh = g * HPG + h
            base = MISC_GATE + hh * 3
            cs = slice(h * tq, (h + 1) * tq)
            outs.append(gates[base:base + 1] * ocmp_ref[0, hh] + gates[base + 1:base + 2] * o_slc[:, cs]
                        + gates[base + 2:base + 3] * o_win[:, cs])
    for pair in range(H_NSA // 2):
        both = jnp.concatenate([outs[2 * pair], outs[2 * pair + 1]], axis=0)
        o_ref[0, :, pair * LANE:(pair + 1) * LANE] = both.T.astype(o_ref.dtype)


def _nsa_attn_call(pb3, neg, ocmp, pf3, bias_n, tq):
    B, S, _ = pb3.shape
    W = H_NSA * LANE
    kv = lambda col: pl.BlockSpec((1, S, 2 * LANE), lambda b, i: (b, 0, col // (2 * LANE)))
    return pl.pallas_call(
        functools.partial(_nsa_attn_kernel, tq=tq),
        out_shape=jax.ShapeDtypeStruct((B, S, C_NSA), BF16),
        grid=(B, S // tq),
        in_specs=[pl.BlockSpec((1, tq, W), lambda b, i: (b, i, PB_NQ // W)),
                  pl.BlockSpec((1, G_NSA, tq, LANE), lambda b, i: (b, 0, i, 0)),
                  kv(PB_KS), kv(PB_VS), kv(PB_KW), kv(PB_VW),
                  pl.BlockSpec((1, H_NSA, HEAD_DIM, tq), lambda b, i: (b, 0, 0, i)),
                  pl.BlockSpec((1, tq, LANE), lambda b, i: (b, i, PF_MISC // LANE)),
                  pl.BlockSpec(bias_n.shape, lambda b, i: (0, 0, 0, 0))],
        out_specs=pl.BlockSpec((1, tq, C_NSA), lambda b, i: (b, i, 0)),
        scratch_shapes=[pltpu.VMEM((G_NSA, S, LANE), BF16),
                        pltpu.VMEM((G_NSA, S // tq, HEAD_DIM, tq), BF16),
                        pltpu.VMEM((G_NSA, S // tq, HEAD_DIM, tq), BF16)],
        compiler_params=_cparams(("arbitrary", "arbitrary")),
        name="nsa_attn",
    )(pb3, neg, pb3, pb3, pb3, pb3, ocmp, pf3, bias_n)


def _outproj_kernel(x_ref, ya_ref, yb_ref, yc_ref, w_ref, gm_ref, g_ref, o_ref):
    ca, cb = ya_ref.shape[1], yb_ref.shape[1]
    y = (_dot(ya_ref[...], w_ref[:ca]) + _dot(yb_ref[...], w_ref[ca:ca + cb])
         + _dot(yc_ref[...], w_ref[ca + cb:]))
    o_ref[...] = x_ref[...] + gm_ref[0] * _rmsnorm(y, g_ref[...])


def _outproj_call(x2d, ya, yb, yc, w, gm, g, S, tm=512):
    N, D = x2d.shape
    per = S // tm
    rows = lambda a: pl.BlockSpec((tm, a.shape[1]), lambda i: (i, 0))
    return pl.pallas_call(
        _outproj_kernel,
        out_shape=jax.ShapeDtypeStruct((N, D), F32),
        grid=(N // tm,),
        in_specs=[rows(x2d), rows(ya), rows(yb), rows(yc),
                  pl.BlockSpec(w.shape, lambda i: (0, 0)),
                  pl.BlockSpec((1, 1, D), lambda i: (i // per, 0, 0)),
                  pl.BlockSpec((1, D), lambda i: (0, 0))],
        out_specs=pl.BlockSpec((tm, D), lambda i: (i, 0)),
        compiler_params=_cparams(("arbitrary",)),
        name="out_proj",
    )(x2d, ya, yb, yc, w, gm, g.reshape(1, D))


def _ffn_kernel(x_ref, halo_ref, sc_ref, sh_ref, gf_ref, g2_ref, g3_ref, wg_ref, wv_ref,
                cwg_ref, cwv_ref, cbg_ref, cbv_ref, wd_ref, o_ref, h_scr, acc_scr, *, tm, rows_per_seq):
    i = pl.program_id(0)
    f = pl.program_id(1)

    @pl.when(f == 0)
    def _():
        xe = jnp.concatenate([halo_ref[...], x_ref[...]], axis=0)
        h = _rmsnorm(xe, g2_ref[...]) * (1.0 + sc_ref[0]) + sh_ref[0]
        row = lax.broadcasted_iota(jnp.int32, (tm + 8, 1), 0)
        first = (i * tm) % rows_per_seq == 0
        h_scr[...] = jnp.where((row < 8) & first, 0.0, h).astype(BF16)
        acc_scr[...] = jnp.zeros_like(acc_scr)

    h = h_scr[...]

    def conv(w_ref, cw_ref, cb_ref):
        u = _dot(h, w_ref[...])
        y = (cw_ref[2:3] * u + cw_ref[1:2] * pltpu.roll(u, 1, 0) + cw_ref[0:1] * pltpu.roll(u, 2, 0)
             + cb_ref[...])
        return y[8:]

    gate = conv(wg_ref, cwg_ref, cbg_ref)
    val = conv(wv_ref, cwv_ref, cbv_ref)
    act = (gate * _sigmoid(gate) * val).astype(BF16)
    acc_scr[...] += _dot(act, wd_ref[...])

    @pl.when(f == pl.num_programs(1) - 1)
    def _():
        o_ref[...] = x_ref[...] + gf_ref[0] * _rmsnorm(acc_scr[...], g3_ref[...])


def _ffn_call(x2d, sc, sh, gf, g2, g3, w_up, conv_w, conv_b, w_down, S, tm=512, tf=1408):
    N, D = x2d.shape
    F = w_down.shape[0]
    nf = F // tf
    per = S // tm
    mod = pl.BlockSpec((1, 1, D), lambda i, f: (i // per, 0, 0))
    vec = pl.BlockSpec((1, D), lambda i, f: (0, 0))
    cb = conv_b.reshape(1, 2 * F)
    return pl.pallas_call(
        functools.partial(_ffn_kernel, tm=tm, rows_per_seq=S),
        out_shape=jax.ShapeDtypeStruct((N, D), F32),
        grid=(N // tm, nf),
        in_specs=[pl.BlockSpec((tm, D), lambda i, f: (i, 0)),
                  pl.BlockSpec((8, D), lambda i, f: (jnp.maximum(i * (tm // 8) - 1, 0), 0)),
                  mod, mod, mod, vec, vec,
                  pl.BlockSpec((D, tf), lambda i, f: (0, f)),
                  pl.BlockSpec((D, tf), lambda i, f: (0, nf + f)),
                  pl.BlockSpec((CONV_W, tf), lambda i, f: (0, f)),
                  pl.BlockSpec((CONV_W, tf), lambda i, f: (0, nf + f)),
                  pl.BlockSpec((1, tf), lambda i, f: (0, f)),
                  pl.BlockSpec((1, tf), lambda i, f: (0, nf + f)),
                  pl.BlockSpec((tf, D), lambda i, f: (f, 0))],
        out_specs=pl.BlockSpec((tm, D), lambda i, f: (i, 0)),
        scratch_shapes=[pltpu.VMEM((tm + 8, D), BF16), pltpu.VMEM((tm, D), F32)],
        compiler_params=_cparams(("arbitrary", "arbitrary")),
        name="conv_ffn",
    )(x2d, x2d, sc, sh, gf, g2.reshape(1, D), g3.reshape(1, D), w_up, w_up, conv_w, conv_w, cb, cb, w_down)


def _column_maps():
    n_rwkv = 3 * C_RWKV + R_DECAY + R_AAA + R_GATE
    n_fox = 3 * C_FOX + H_FOX
    fox0 = n_rwkv
    nsa0 = n_rwkv + n_fox
    pf = np.full(PF_COLS, -1, np.int64)
    pf[PF_R:PF_R + 3 * C_RWKV] = np.arange(3 * C_RWKV)
    pf[PF_XW:PF_XW + R_DECAY] = 3 * C_RWKV + np.arange(R_DECAY)
    pf[PF_XA:PF_XA + R_AAA] = 3 * C_RWKV + R_DECAY + np.arange(R_AAA)
    pf[PF_XG:PF_XG + R_GATE] = 3 * C_RWKV + R_DECAY + R_AAA + np.arange(R_GATE)
    pf[PF_MISC + MISC_F:PF_MISC + MISC_F + H_FOX] = fox0 + 3 * C_FOX + np.arange(H_FOX)
    nsa_gate0 = nsa0 + C_NSA + 6 * G_NSA * HEAD_DIM
    pf[PF_MISC + MISC_GATE:PF_MISC + MISC_GATE + 3 * H_NSA] = nsa_gate0 + np.arange(3 * H_NSA)

    pb = np.full(PB_COLS, -1, np.int64)
    scale = np.ones(PB_COLS, np.float32)
    d = np.arange(HEAD_DIM)
    for h in range(H_FOX):
        pb[PB_FQ + h * LANE + d] = fox0 + h * HEAD_DIM + d
        scale[PB_FQ + h * LANE + d] = HEAD_DIM ** -0.5
        pb[PB_FK + h * LANE + d] = fox0 + C_FOX + h * HEAD_DIM + d
    pb[PB_FV:PB_FV + C_FOX] = fox0 + 2 * C_FOX + np.arange(C_FOX)
    for h in range(H_NSA):
        pb[PB_NQ + h * LANE + d] = nsa0 + h * HEAD_DIM + d
        scale[PB_NQ + h * LANE + d] = HEAD_DIM ** -0.5
    ckv = G_NSA * HEAD_DIM
    kc0 = nsa0 + C_NSA
    pb[PB_KC:PB_KC + ckv] = kc0 + np.arange(ckv)
    pb[PB_VC:PB_VC + ckv] = kc0 + ckv + np.arange(ckv)
    for n, base in enumerate((PB_KS, PB_VS, PB_KW, PB_VW)):
        for g in range(G_NSA):
            pb[base + g * LANE + d] = kc0 + (2 + n) * ckv + g * HEAD_DIM + d
    return pf, pb, scale


def _pad_rows(w, rows):
    return jnp.concatenate([w, jnp.zeros((rows - w.shape[0],) + w.shape[1:], w.dtype)], axis=0)


def _compress_w1(w1):
    hid = w1.shape[1]
    w = w1.reshape(2, D_CMP, HEAD_DIM, hid)
    out = jnp.zeros((G_NSA, 2, D_CMP, G_NSA, HEAD_DIM, hid), w1.dtype)
    for g in range(G_NSA):
        out = out.at[g, :, :, g].set(w)
    return out.reshape(G_NSA, 2, D_CMP * G_NSA * HEAD_DIM, hid).astype(BF16)


def _compress_pe(pe):
    half = pe.reshape(2, 1, D_CMP, 1, HEAD_DIM)
    return jnp.broadcast_to(half, (2, 8, D_CMP, G_NSA, HEAD_DIM)).reshape(2, 8, D_CMP * G_NSA * HEAD_DIM)


def _fox_place():
    place = np.zeros((3, LANE, H_FOX * LANE), np.float32)
    for j in range(3):
        for h in range(H_FOX):
            place[j, MISC_F + h, h * LANE + HEAD_DIM + j] = 1.0
    return jnp.asarray(place, BF16)


def _overlap_t(S):
    nc = S // D_CMP
    nsb = S // L_SLC
    c0 = np.arange(nc) * D_CMP
    c1 = c0 + L_CMP - 1
    s0 = np.arange(nsb) * L_SLC
    ov = (c0[None, :] <= s0[:, None] + L_SLC - 1) & (c1[None, :] >= s0[:, None])
    ov[:, nc - 1] = False
    return jnp.asarray(ov.astype(np.float32), BF16)


def kernel(x, c, ada_w, ada_b, norm_g, w_in, rwkv_mu, rwkv_w0, rwkv_w_up, rwkv_a0, rwkv_a_up, rwkv_g_up, rwkv_k_k, rwkv_k_a, rwkv_r_k, rwkv_ln_w, rwkv_ln_b, fox_b_f, nsa_pe_k, nsa_pe_v, nsa_ck_w1, nsa_ck_w2, nsa_cv_w1, nsa_cv_w2, rel_bias, w_out, ffn_up, ffn_conv_w, ffn_conv_b, ffn_down):
    B, S, D = x.shape
    L = w_in.shape[0]
    tq_sel = 128
    tq_nsa = WINDOW // 2
    assert S % 512 == 0 and S // L_SLC <= HEAD_DIM and D == 1024

    pf_idx, pb_idx, pb_scale = _column_maps()
    w_ext = jnp.concatenate([w_in, jnp.zeros((L, D, 1), w_in.dtype)], axis=2)
    w_pf = jnp.take(w_ext, jnp.asarray(pf_idx), axis=2).astype(BF16)
    w_pb = (jnp.take(w_ext, jnp.asarray(pb_idx), axis=2) * pb_scale).astype(BF16)
    mu_ext = jnp.concatenate([rwkv_mu, jnp.zeros((L, 1), F32)], axis=1)
    mu_pf = jnp.take(mu_ext, jnp.asarray(pf_idx[:PF_RWKV]), axis=1)

    mod_all = _mod_call(c, ada_w, ada_b).reshape(L, B, 6, 1, D)
    bias_c, bias_n = _bias_tables(rel_bias, S, tq_nsa)
    place = _fox_place()
    overlap_t = _overlap_t(S)

    x2d = x.reshape(B * S, D)
    for l in range(L):
        sh_m, sc_m, g_m, sh_f, sc_f, g_f = (mod_all[l, :, j] for j in range(6))
        pf = _inproj_call(x2d, sc_m, sh_m, norm_g[l, 0], w_pf[l], F32, S)
        pb = _inproj_call(x2d, sc_m, sh_m, norm_g[l, 0], w_pb[l], BF16, S)
        pf3 = pf.reshape(B, S, PF_COLS)
        pb3 = pb.reshape(B, S, PB_COLS)

        ry, mg, bonus, gate = _rwkv_local_call(
            pf, mu_pf[l], rwkv_w0[l], _pad_rows(rwkv_w_up[l], LANE).astype(BF16), rwkv_a0[l],
            _pad_rows(rwkv_a_up[l], LANE).astype(BF16), _pad_rows(rwkv_g_up[l], LANE).astype(BF16),
            rwkv_k_k[l], rwkv_k_a[l], rwkv_r_k[l], S)
        ya = _rwkv_scan_call(ry, mg, bonus, gate, rwkv_ln_w[l], rwkv_ln_b[l], B, S)

        b_f_row = jnp.zeros((1, LANE), F32).at[0, MISC_F:MISC_F + H_FOX].set(fox_b_f[l])
        kaug, fox_vt = _fox_prep_call(pf3, pb3, b_f_row, place)
        yb = _fox_attn_call(pb3, kaug, fox_vt)

        kc_flat = pb3[:, :, PB_KC:PB_KC + LANE].reshape(B, S // D_CMP, D_CMP * LANE)
        vc_flat = pb3[:, :, PB_VC:PB_VC + LANE].reshape(B, S // D_CMP, D_CMP * LANE)
        w2pad = lambda w: jnp.concatenate([w, jnp.zeros_like(w)], axis=1).astype(BF16)
        kcmp, vcmpT = _nsa_compress_call(
            kc_flat, vc_flat, _compress_w1(nsa_ck_w1[l]), _compress_w1(nsa_cv_w1[l]),
            w2pad(nsa_ck_w2[l]), nsa_cv_w2[l].T.astype(BF16),
            _compress_pe(nsa_pe_k[l]), _compress_pe(nsa_pe_v[l]))
        ocmp, neg = _nsa_select_call(pb3, kcmp, vcmpT, bias_c, overlap_t, tq_sel)
        yc = _nsa_attn_call(pb3, neg, ocmp, pf3, bias_n, tq_nsa)

        x2d = _outproj_call(x2d, ya.reshape(B * S, C_RWKV), yb.reshape(B * S, C_FOX),
                            yc.reshape(B * S, C_NSA), w_out[l].astype(BF16), g_m, norm_g[l, 1], S)
        x2d = _ffn_call(x2d, sc_f, sh_f, g_f, norm_g[l, 2], norm_g[l, 3], ffn_up[l].astype(BF16),
                        ffn_conv_w[l], ffn_conv_b[l], ffn_down[l].astype(BF16), S)
    return x2d.reshape(B, S, D)
```

```python
import functools
import math

import numpy as np
import jax
import jax.numpy as jnp
from jax import lax
from jax.experimental import pallas as pl
from jax.experimental.pallas import tpu as pltpu

F32 = jnp.float32
BF16 = jnp.bfloat16

HEAD_DIM = 64
H_RWKV = 4
C_RWKV = H_RWKV * HEAD_DIM
H_FOX = 6
C_FOX = H_FOX * HEAD_DIM
H_NSA = 6
C_NSA = H_NSA * HEAD_DIM
G_NSA = 2
HPG = H_NSA // G_NSA
R_DECAY = 32
R_AAA = 32
R_GATE = 64
L_CMP = 32
D_CMP = 16
CMP_HID = 128
L_SLC = 64
N_SLC = 16
N_LOCAL = 2
WINDOW = 512
NUM_BUCKETS = 32
MAX_DISTANCE = 128
CONV_W = 3
RMS_EPS = 1e-6
GN_EPS = 64e-5
NEG_INF = -1e30
FORCE_SCORE = 1e4

LANE = 128
CHUNK = 64
VMEM_LIMIT = 56 * 1024 * 1024

PF_R, PF_K, PF_V, PF_XW, PF_XA, PF_XG, PF_MISC = 0, 256, 512, 768, 896, 1024, 1152
PF_RWKV = 1152
PF_COLS = 1280
MISC_F = 0
MISC_GATE = 8
PB_FQ, PB_FK, PB_NQ = 0, 768, 1536
PB_FV, PB_KC, PB_VC = 2304, 2688, 2816
PB_KS, PB_VS, PB_KW, PB_VW = 3072, 3328, 3584, 3840
PB_COLS = 4096


def _bucket_lower_bounds():
    n = np.arange(0, 4 * MAX_DISTANCE, dtype=np.int64)
    max_exact = NUM_BUCKETS // 2
    nf = np.maximum(n, 1).astype(np.float32)
    large = max_exact + (np.log(nf / np.float32(max_exact)) / np.float32(math.log(MAX_DISTANCE / max_exact))
                         * np.float32(NUM_BUCKETS - max_exact)).astype(np.int32)
    large = np.minimum(large, NUM_BUCKETS - 1)
    bucket = np.where(n < max_exact, n, large)
    return [int(np.argmax(bucket >= b)) for b in range(NUM_BUCKETS)]


BUCKET_LB = _bucket_lower_bounds()
BIAS_CONST_DIST = BUCKET_LB[NUM_BUCKETS - 1]


def _cparams(sem, vmem=None):
    return pltpu.CompilerParams(dimension_semantics=sem, vmem_limit_bytes=vmem or VMEM_LIMIT)


def _dot(a, b):
    return jnp.dot(a, b, preferred_element_type=F32)


def _dot_nt(a, b):
    return lax.dot_general(a, b, (((1,), (1,)), ((), ())), preferred_element_type=F32)


def _split_bf16(x, n):
    parts, r = [], x
    for i in range(n):
        p = r.astype(BF16)
        parts.append(p)
        if i + 1 < n:
            r = r - p.astype(F32)
    return parts


def _dot_hp(a, b, nt=False):
    f = _dot_nt if nt else _dot
    ah, al = _split_bf16(a, 2)
    bh, bl = _split_bf16(b, 2)
    return f(ah, bh) + (f(ah, bl) + f(al, bh))


def _dot_exact_lhs(a_bf16, b, n):
    out = None
    for p in _split_bf16(b, n):
        t = _dot(a_bf16, p)
        out = t if out is None else out + t
    return out


def _softplus(x):
    return jnp.maximum(x, 0.0) + jnp.log(1.0 + jnp.exp(-jnp.abs(x)))


def _sigmoid(x):
    return 1.0 / (1.0 + jnp.exp(-x))


def _rmsnorm(x, g):
    return x * lax.rsqrt(jnp.mean(x * x, axis=-1, keepdims=True) + RMS_EPS) * g


def _mod_kernel(c_ref, w_ref, b_ref, o_ref):
    c = c_ref[...]
    s = (c * _sigmoid(c)).astype(BF16)
    o_ref[0] = _dot(s, w_ref[0].astype(BF16)) + b_ref[0]


def _mod_call(c, ada_w, ada_b):
    L, D, N = ada_w.shape
    B = c.shape[0]
    tn = 1536
    return pl.pallas_call(
        _mod_kernel,
        out_shape=jax.ShapeDtypeStruct((L, B, N), F32),
        grid=(L, N // tn),
        in_specs=[pl.BlockSpec((B, D), lambda l, j: (0, 0)),
                  pl.BlockSpec((1, D, tn), lambda l, j: (l, 0, j)),
                  pl.BlockSpec((1, 1, tn), lambda l, j: (l, 0, j))],
        out_specs=pl.BlockSpec((1, B, tn), lambda l, j: (l, 0, j)),
        compiler_params=_cparams(("arbitrary", "arbitrary")),
        name="adaln_mod",
    )(c, ada_w, ada_b.reshape(L, 1, N))


def _inproj_kernel(x_ref, sc_ref, sh_ref, g_ref, w_ref, o_ref, *, n_chunk):
    h = _rmsnorm(x_ref[...], g_ref[...]) * (1.0 + sc_ref[0]) + sh_ref[0]
    h = h.astype(BF16)
    n = o_ref.shape[1]
    for n0 in range(0, n, n_chunk):
        n1 = min(n, n0 + n_chunk)
        o_ref[:, n0:n1] = _dot(h, w_ref[:, n0:n1]).astype(o_ref.dtype)


def _inproj_call(x2d, sc, sh, g, w, out_dtype, S, tm=512):
    N, D = x2d.shape
    C = w.shape[1]
    per = S // tm
    return pl.pallas_call(
        functools.partial(_inproj_kernel, n_chunk=512),
        out_shape=jax.ShapeDtypeStruct((N, C), out_dtype),
        grid=(N // tm,),
        in_specs=[pl.BlockSpec((tm, D), lambda i: (i, 0)),
                  pl.BlockSpec((1, 1, D), lambda i: (i // per, 0, 0)),
                  pl.BlockSpec((1, 1, D), lambda i: (i // per, 0, 0)),
                  pl.BlockSpec((1, D), lambda i: (0, 0)),
                  pl.BlockSpec((D, C), lambda i: (0, 0))],
        out_specs=pl.BlockSpec((tm, C), lambda i: (i, 0)),
        compiler_params=_cparams(("arbitrary",)),
        name="in_proj",
    )(x2d, sc, sh, g.reshape(1, D), w)


def _rwkv_local_kernel(p_ref, halo_ref, mu_ref, w0_ref, wup_ref, a0_ref, aup_ref, gup_ref,
                       kk_ref, ka_ref, rk_ref,
                       ry_ref, mg_ref, bonus_ref, gate_ref, *, tr, rows_per_seq):
    C = C_RWKV
    i = pl.program_id(0)
    first = (i * tr) % rows_per_seq == 0
    p = p_ref[:, :PF_RWKV]
    row = lax.broadcasted_iota(jnp.int32, (tr, 1), 0)
    prev_last = jnp.where(first, 0.0, halo_ref[7:8, :PF_RWKV])
    prev = jnp.where(row == 0, prev_last, pltpu.roll(p, 1, 0))
    ps = p + (prev - p) * mu_ref[...]
    r = ps[:, PF_R:PF_R + C]
    k = ps[:, PF_K:PF_K + C]
    v = ps[:, PF_V:PF_V + C]
    xw = ps[:, PF_XW:PF_XW + LANE]
    xa = ps[:, PF_XA:PF_XA + LANE]
    xg = ps[:, PF_XG:PF_XG + LANE]

    wl = w0_ref[...] + _dot(jnp.tanh(xw).astype(BF16), wup_ref[...])
    lw = -jnp.exp(-_softplus(-wl) - 0.5)
    a = _sigmoid(a0_ref[...] + _dot(xa.astype(BF16), aup_ref[...]))
    gate_ref[...] = _dot(_sigmoid(xg).astype(BF16), gup_ref[...])

    li = lax.broadcasted_iota(jnp.int32, (C, C), 0) // HEAD_DIM
    lj = lax.broadcasted_iota(jnp.int32, (C, C), 1) // HEAD_DIM
    same_head = li == lj
    head_ones = jnp.where(same_head, 1.0, 0.0).astype(BF16)

    kk = k * kk_ref[...]
    nrm = jnp.sqrt(_dot_exact_lhs_t(kk * kk, head_ones))
    kk = kk / jnp.maximum(nrm, 1e-12)
    k2 = k * (1.0 + (a - 1.0) * ka_ref[...])
    bonus_ref[...] = _dot_exact_lhs_t(r * k2 * rk_ref[...], head_ones) * v
    avec = -kk
    bvec = kk * a

    nch = tr // CHUNK
    ti = lax.broadcasted_iota(jnp.int32, (tr, tr), 0)
    tj = lax.broadcasted_iota(jnp.int32, (tr, tr), 1)
    same_chunk = ti // CHUNK == tj // CHUNK
    tri_incl = jnp.where(same_chunk & (tj <= ti), 1.0, 0.0).astype(BF16)
    chunk_ones = jnp.where(same_chunk, 1.0, 0.0).astype(BF16)
    Lc = _dot_exact_lhs(tri_incl, lw, 3)
    Lend = _dot_exact_lhs(chunk_ones, lw, 3)
    e_cur = jnp.exp(Lc)
    e_inv = jnp.exp(-Lc)
    e_end = jnp.exp(Lend - Lc)
    At = avec * jnp.exp(Lc - lw)
    Rt = r * e_cur
    Bt = bvec * e_inv
    Kt = k2 * e_inv
    bh_t = (bvec * e_end).T
    kh_t = (k2 * e_end).T
    pc = jnp.exp(Lend)

    pairs = [(c, h) for c in range(nch) for h in range(H_RWKV)]
    blk = lambda x: jnp.stack([x[c * CHUNK:(c + 1) * CHUNK, h * HEAD_DIM:(h + 1) * HEAD_DIM]
                               for c, h in pairs]).astype(BF16)
    blk_t = lambda x: jnp.stack([x[h * HEAD_DIM:(h + 1) * HEAD_DIM, c * CHUNK:(c + 1) * CHUNK]
                                 for c, h in pairs]).astype(BF16)
    bmm = lambda a, b: jnp.einsum('nij,njk->nik', a.astype(BF16), b.astype(BF16), preferred_element_type=F32)
    bmm_nt = lambda a, b: jnp.einsum('nid,nkd->nik', a, b, preferred_element_type=F32)

    ci = lax.broadcasted_iota(jnp.int32, (CHUNK, CHUNK), 0)
    cj = lax.broadcasted_iota(jnp.int32, (CHUNK, CHUNK), 1)
    lower_strict = cj < ci
    lower_incl = cj <= ci
    a_b, r_b, b_b, k_b, v_b = blk(At), blk(Rt), blk(Bt), blk(Kt), blk(v)
    ar = jnp.concatenate([a_b, r_b], axis=1)
    ab = bmm_nt(ar, b_b)
    ak = bmm_nt(ar, k_b)
    n_mat = jnp.where(lower_strict, ab[:, :CHUNK], 0.0)
    a_ak = jnp.where(lower_strict, ak[:, :CHUNK], 0.0)
    a_rb = jnp.where(lower_incl, ab[:, CHUNK:], 0.0)
    a_rk = jnp.where(lower_incl, ak[:, CHUNK:], 0.0)
    av = bmm(jnp.concatenate([a_ak, a_rk], axis=1), v_b)
    x = jnp.concatenate([a_b.astype(F32), av[:, :CHUNK]], axis=2)
    x = x + bmm(n_mat, x)
    for _ in range(5):
        n_mat = bmm(n_mat, n_mat)
        x = x + bmm(n_mat, x)
    corr = bmm(a_rb, x)
    rbar = r_b.astype(F32) + corr[:, :, :HEAD_DIM]
    y0 = av[:, CHUNK:] + corr[:, :, HEAD_DIM:]
    m_mat = bmm(blk_t(bh_t), x)
    g_add = bmm(blk_t(kh_t), v_b)
    eye = ci == cj
    zero = jnp.zeros((HEAD_DIM, HEAD_DIM), F32)
    for c in range(nch):
        sl = slice(c * CHUNK, (c + 1) * CHUNK)
        ns = [c * H_RWKV + h for h in range(H_RWKV)]
        ry_ref[sl, :C] = jnp.concatenate([rbar[n] for n in ns], axis=1)
        ry_ref[sl, C:] = jnp.concatenate([y0[n] for n in ns], axis=1)
        for h, n in enumerate(ns):
            hs = slice(h * HEAD_DIM, (h + 1) * HEAD_DIM)
            pc_h = pc[c * CHUNK:c * CHUNK + 1, hs]
            m_h = jnp.where(eye, pc_h, 0.0) + m_mat[n, :, :HEAD_DIM]
            g_h = m_mat[n, :, HEAD_DIM:] + g_add[n]
            mg_ref[c, hs, :C] = jnp.concatenate([m_h if j == h else zero for j in range(H_RWKV)], axis=1)
            mg_ref[c, hs, C:] = jnp.concatenate([g_h if j == h else zero for j in range(H_RWKV)], axis=1)


def _dot_exact_lhs_t(x, ones_bf16):
    xh, xl = _split_bf16(x, 2)
    return _dot(xh, ones_bf16) + _dot(xl, ones_bf16)


def _rwkv_local_call(pf, mu, w0, wup, a0, aup, gup, k_k, k_a, r_k, S, tr=256):
    N = pf.shape[0]
    C = C_RWKV
    row = lambda a: a.reshape(1, -1)
    full = lambda a: pl.BlockSpec(a.shape, lambda i: (0,) * a.ndim)
    args = [row(mu), row(w0), wup, row(a0), aup, gup, row(k_k), row(k_a), row(r_k)]
    return pl.pallas_call(
        functools.partial(_rwkv_local_kernel, tr=tr, rows_per_seq=S),
        out_shape=(jax.ShapeDtypeStruct((N, 2 * C), F32),
                   jax.ShapeDtypeStruct((N // CHUNK, C, 2 * C), F32),
                   jax.ShapeDtypeStruct((N, C), F32),
                   jax.ShapeDtypeStruct((N, C), F32)),
        grid=(N // tr,),
        in_specs=[pl.BlockSpec((tr, PF_COLS), lambda i: (i, 0)),
                  pl.BlockSpec((8, PF_COLS), lambda i: (jnp.maximum(i * (tr // 8) - 1, 0), 0))]
                 + [full(a) for a in args],
        out_specs=(pl.BlockSpec((tr, 2 * C), lambda i: (i, 0)),
                   pl.BlockSpec((tr // CHUNK, C, 2 * C), lambda i: (i, 0, 0)),
                   pl.BlockSpec((tr, C), lambda i: (i, 0)),
                   pl.BlockSpec((tr, C), lambda i: (i, 0))),
        compiler_params=_cparams(("arbitrary",)),
        name="rwkv_local",
    )(pf, pf, *args)


def _rwkv_scan_kernel(ry_ref, mg_ref, bonus_ref, gate_ref, lnw_ref, lnb_ref, o_ref, s_scr):
    C = C_RWKV
    B = ry_ref.shape[0]

    @pl.when(pl.program_id(0) == 0)
    def _():
        s_scr[...] = jnp.zeros_like(s_scr)

    li = lax.broadcasted_iota(jnp.int32, (C, C), 0) // HEAD_DIM
    lj = lax.broadcasted_iota(jnp.int32, (C, C), 1) // HEAD_DIM
    head_avg = jnp.where(li == lj, 1.0 / HEAD_DIM, 0.0).astype(BF16)
    bs = range(B)
    s0 = [s_scr[b] for b in bs]
    ys = [_dot_hp(ry_ref[b, :, :C], s0[b]) + ry_ref[b, :, C:] for b in bs]
    for b in bs:
        s_scr[b] = _dot_hp(mg_ref[b, 0, :, :C], s0[b]) + mg_ref[b, 0, :, C:]
    means = [_dot_exact_lhs_t(ys[b], head_avg) for b in bs]
    ds = [ys[b] - means[b] for b in bs]
    vs = [_dot_exact_lhs_t(ds[b] * ds[b], head_avg) for b in bs]
    for b in bs:
        yn = ds[b] * lax.rsqrt(vs[b] + GN_EPS) * lnw_ref[...] + lnb_ref[...]
        o_ref[b] = ((yn + bonus_ref[b]) * gate_ref[b]).astype(o_ref.dtype)


def _rwkv_scan_call(ry, mg, bonus, gate, ln_w, ln_b, B, S):
    C = C_RWKV
    nc = S // CHUNK
    return pl.pallas_call(
        _rwkv_scan_kernel,
        out_shape=jax.ShapeDtypeStruct((B, S, C), BF16),
        grid=(nc,),
        in_specs=[pl.BlockSpec((B, CHUNK, 2 * C), lambda c: (0, c, 0)),
                  pl.BlockSpec((B, 1, C, 2 * C), lambda c: (0, c, 0, 0)),
                  pl.BlockSpec((B, CHUNK, C), lambda c: (0, c, 0)),
                  pl.BlockSpec((B, CHUNK, C), lambda c: (0, c, 0)),
                  pl.BlockSpec((1, C), lambda c: (0, 0)),
                  pl.BlockSpec((1, C), lambda c: (0, 0))],
        out_specs=pl.BlockSpec((B, CHUNK, C), lambda c: (0, c, 0)),
        scratch_shapes=[pltpu.VMEM((B, C, C), F32)],
        compiler_params=_cparams(("arbitrary",)),
        name="rwkv_scan",
    )(ry.reshape(B, S, 2 * C), mg.reshape(B, nc, C, 2 * C), bonus.reshape(B, S, C),
      gate.reshape(B, S, C), ln_w.reshape(1, C), ln_b.reshape(1, C))


def _online_update(carry, s, v):
    m, l, acc = carry
    m_new = jnp.maximum(m, jnp.max(s, axis=-1, keepdims=True))
    alpha = jnp.exp(m - m_new)
    p = jnp.exp(s - m_new)
    l = alpha * l + jnp.sum(p, axis=-1, keepdims=True)
    acc = alpha * acc + _dot(p.astype(BF16), v)
    return m_new, l, acc


def _attn_init(rows):
    return (jnp.full((rows, 1), NEG_INF, F32), jnp.zeros((rows, 1), F32), jnp.zeros((rows, LANE), F32))


def _online_update_t(carry, s_t, v_t):
    m, l, acc = carry
    m_new = jnp.maximum(m, jnp.max(s_t, axis=0, keepdims=True))
    alpha = jnp.exp(m - m_new)
    p = jnp.exp(s_t - m_new)
    l = alpha * l + jnp.sum(p, axis=0, keepdims=True)
    acc = alpha * acc + _dot(v_t, p.astype(BF16))
    return m_new, l, acc


def _attn_init_t(d, cols):
    return (jnp.full((1, cols), NEG_INF, F32), jnp.zeros((1, cols), F32), jnp.zeros((d, cols), F32))


def _online_update_multi(carries, ss, vts):
    n = range(len(ss))
    ms = [jnp.maximum(carries[i][0], jnp.max(ss[i], axis=0, keepdims=True)) for i in n]
    ps = [jnp.exp(ss[i] - ms[i]) for i in n]
    pvs = [_dot(vts[i], ps[i].astype(BF16)) for i in n]
    out = []
    for i in n:
        m, l, acc = carries[i]
        alpha = jnp.exp(m - ms[i])
        out.append((ms[i], alpha * l + jnp.sum(ps[i], axis=0, keepdims=True), alpha * acc + pvs[i]))
    return tuple(out)


def _fox_prep_kernel(misc_ref, k_ref, v_ref, bf_ref, place_ref, o_ref, vt_ref, carry_scr, *, t, tk):
    @pl.when(pl.program_id(1) == 0)
    def _():
        carry_scr[...] = jnp.zeros_like(carry_scr)

    for hp in range(H_FOX // 2):
        v_t = v_ref[0, :, hp * LANE:(hp + 1) * LANE].astype(F32).T
        for j in range(t // tk):
            for hl in range(2):
                vt_ref[0, 2 * hp + hl, j] = v_t[hl * HEAD_DIM:(hl + 1) * HEAD_DIM,
                                                j * tk:(j + 1) * tk].astype(BF16)

    lf = -_softplus(-(misc_ref[0] + bf_ref[...]))
    ti = lax.broadcasted_iota(jnp.int32, (t, t), 0)
    tj = lax.broadcasted_iota(jnp.int32, (t, t), 1)
    tri = jnp.where(tj <= ti, 1.0, 0.0).astype(BF16)
    cum = _dot_exact_lhs(tri, lf, 3) + carry_scr[...]
    carry_scr[...] = cum[t - 1:t]
    parts = _split_bf16(-cum, 3)
    out = k_ref[0].astype(F32)
    for j, part in enumerate(parts):
        out = out + _dot(part, place_ref[j])
    o_ref[0] = out.astype(BF16)


FOX_TK = 256


def _fox_prep_call(pf3, pb3, b_f_row, place, t=512):
    B, S, _ = pf3.shape
    W = H_FOX * LANE
    tk = FOX_TK
    return pl.pallas_call(
        functools.partial(_fox_prep_kernel, t=t, tk=tk),
        out_shape=(jax.ShapeDtypeStruct((B, S, W), BF16),
                   jax.ShapeDtypeStruct((B, H_FOX, S // tk, HEAD_DIM, tk), BF16)),
        grid=(B, S // t),
        in_specs=[pl.BlockSpec((1, t, LANE), lambda b, i: (b, i, PF_MISC // LANE)),
                  pl.BlockSpec((1, t, W), lambda b, i: (b, i, PB_FK // W)),
                  pl.BlockSpec((1, t, C_FOX), lambda b, i: (b, i, PB_FV // C_FOX)),
                  pl.BlockSpec((1, LANE), lambda b, i: (0, 0)),
                  pl.BlockSpec((3, LANE, W), lambda b, i: (0, 0, 0))],
        out_specs=(pl.BlockSpec((1, t, W), lambda b, i: (b, i, 0)),
                   pl.BlockSpec((1, H_FOX, t // tk, HEAD_DIM, tk), lambda b, i: (b, 0, i, 0, 0))),
        scratch_shapes=[pltpu.VMEM((1, LANE), F32)],
        compiler_params=_cparams(("arbitrary", "arbitrary")),
        name="fox_prep",
    )(pf3, pb3, pb3, b_f_row, place)


def _fox_attn_kernel(q_ref, k_ref, vt_ref, o_ref, *, tq, tk):
    qi = pl.program_id(1)
    lane = lax.broadcasted_iota(jnp.int32, (1, LANE), 1)
    ones3 = jnp.where((lane >= HEAD_DIM) & (lane < HEAD_DIM + 3), 1.0, 0.0).astype(BF16)
    krow = lax.broadcasted_iota(jnp.int32, (tk, tq), 0)
    qcol = lax.broadcasted_iota(jnp.int32, (tk, tq), 1)
    per = tq // tk
    heads = range(H_FOX)
    qs = [q_ref[0, :, h * LANE:(h + 1) * LANE] + ones3 for h in heads]

    def logits(j, h):
        start = pl.multiple_of(j * tk, tk)
        return _dot_nt(k_ref[0, pl.ds(start, tk), h * LANE:(h + 1) * LANE], qs[h])

    carries = tuple(_attn_init_t(HEAD_DIM, tq) for _ in heads)
    for d in range(per):
        j = qi * per + d
        ss = [jnp.where(krow + d * tk <= qcol, logits(j, h), NEG_INF) for h in heads]
        carries = _online_update_multi(carries, ss, [vt_ref[0, h, j] for h in heads])

    def body(j, carries):
        return _online_update_multi(carries, [logits(j, h) for h in heads], [vt_ref[0, h, j] for h in heads])

    carries = lax.fori_loop(0, qi * per, body, carries)
    outs = [acc / l for (_, l, acc) in carries]
    for hp in range(H_FOX // 2):
        pair = jnp.concatenate([outs[2 * hp], outs[2 * hp + 1]], axis=0)
        o_ref[0, :, hp * LANE:(hp + 1) * LANE] = pair.T.astype(o_ref.dtype)


def _fox_attn_call(pb3, kaug, v_t, tq=512):
    B, S, _ = pb3.shape
    tk = FOX_TK
    W = H_FOX * LANE
    return pl.pallas_call(
        functools.partial(_fox_attn_kernel, tq=tq, tk=tk),
        out_shape=jax.ShapeDtypeStruct((B, S, C_FOX), BF16),
        grid=(B, S // tq),
        in_specs=[pl.BlockSpec((1, tq, W), lambda b, i: (b, i, PB_FQ // W)),
                  pl.BlockSpec((1, S, W), lambda b, i: (b, 0, 0)),
                  pl.BlockSpec((1, H_FOX, S // tk, HEAD_DIM, tk), lambda b, i: (b, 0, 0, 0, 0))],
        out_specs=pl.BlockSpec((1, tq, C_FOX), lambda b, i: (b, i, 0)),
        compiler_params=_cparams(("arbitrary", "arbitrary")),
        name="fox_attn",
    )(pb3, kaug, v_t)


def _bias_of_dist(n, tab_ref, h):
    val = jnp.zeros(n.shape, F32) + tab_ref[0, h]
    for b in range(1, NUM_BUCKETS):
        val = jnp.where(n >= BUCKET_LB[b], tab_ref[b, h], val)
    return val - tab_ref[NUM_BUCKETS - 1, h]


def _bias_cmp_kernel(tab_ref, o_ref, *, tt):
    t0 = pl.program_id(0) * tt
    nc = o_ref.shape[1]
    c = lax.broadcasted_iota(jnp.int32, (nc, tt), 0)
    t = lax.broadcasted_iota(jnp.int32, (nc, tt), 1) + t0
    n = jnp.maximum(t - (c * D_CMP + L_CMP - 1), 0)
    for h in range(H_NSA):
        o_ref[h] = _bias_of_dist(n, tab_ref, h)


def _bias_near_kernel(tab_ref, o_ref, *, tq):
    j = lax.broadcasted_iota(jnp.int32, (tq, tq), 0)
    i = lax.broadcasted_iota(jnp.int32, (tq, tq), 1)
    for g in range(G_NSA):
        for near in range(2):
            n = jnp.maximum(i - j + near * tq, 0)
            for h in range(HPG):
                o_ref[g, near, :, h * tq:(h + 1) * tq] = _bias_of_dist(n, tab_ref, g * HPG + h)


def _bias_tables(rel_bias, S, tq):
    nc = S // D_CMP
    tt = 512
    smem = pl.BlockSpec(memory_space=pltpu.SMEM)
    bias_c = pl.pallas_call(
        functools.partial(_bias_cmp_kernel, tt=tt),
        out_shape=jax.ShapeDtypeStruct((H_NSA, nc, S), F32),
        grid=(S // tt,),
        in_specs=[smem],
        out_specs=pl.BlockSpec((H_NSA, nc, tt), lambda i: (0, 0, i)),
        compiler_params=_cparams(("arbitrary",)),
        name="nsa_bias_cmp",
    )(rel_bias)
    bias_n = pl.pallas_call(
        functools.partial(_bias_near_kernel, tq=tq),
        out_shape=jax.ShapeDtypeStruct((G_NSA, 2, tq, HPG * tq), F32),
        in_specs=[smem],
        name="nsa_bias_near",
    )(rel_bias)
    return bias_c, bias_n


def _gelu_tanh(x):
    return 0.5 * x * (1.0 + jnp.tanh(math.sqrt(2.0 / math.pi) * (x + 0.044715 * (x * x * x))))


def _nsa_compress_kernel(kc_ref, vc_ref, wk1_ref, wv1_ref, wk2_ref, wv2_ref, pek_ref, pev_ref,
                         kcmp_ref, vcmpT_ref):
    nc = kc_ref.shape[1]
    for g in range(G_NSA):
        for src_ref, w1_ref, w2_ref, pe_ref, is_k in ((kc_ref, wk1_ref, wk2_ref, pek_ref, True),
                                                      (vc_ref, wv1_ref, wv2_ref, pev_ref, False)):
            ch = src_ref[0]
            p1 = _dot(ch, w1_ref[g, 0])
            p2 = _dot(ch, w1_ref[g, 1])
            pec = (_dot(pe_ref[0], w1_ref[g, 0].astype(F32)) + _dot(pe_ref[1], w1_ref[g, 1].astype(F32)))[0:1]
            hid = p1 + pltpu.roll(p2, nc - 1, 0) + pec
            act = _gelu_tanh(hid).astype(BF16)
            if is_k:
                kcmp_ref[0, g] = _dot(act, w2_ref[...]).astype(BF16)
            else:
                vcmpT_ref[0, g] = _dot_nt(w2_ref[...], act).astype(BF16)


def _nsa_compress_call(kc_flat, vc_flat, wk1, wv1, wk2, wv2, pek, pev):
    B, nc, W = kc_flat.shape
    full = lambda a: pl.BlockSpec(a.shape, lambda b: (0,) * a.ndim)
    return pl.pallas_call(
        _nsa_compress_kernel,
        out_shape=(jax.ShapeDtypeStruct((B, G_NSA, nc, LANE), BF16),
                   jax.ShapeDtypeStruct((B, G_NSA, HEAD_DIM, nc), BF16)),
        grid=(B,),
        in_specs=[pl.BlockSpec((1, nc, W), lambda b: (b, 0, 0)),
                  pl.BlockSpec((1, nc, W), lambda b: (b, 0, 0)),
                  full(wk1), full(wv1), full(wk2), full(wv2), full(pek), full(pev)],
        out_specs=(pl.BlockSpec((1, G_NSA, nc, LANE), lambda b: (b, 0, 0, 0)),
                   pl.BlockSpec((1, G_NSA, HEAD_DIM, nc), lambda b: (b, 0, 0, 0))),
        compiler_params=_cparams(("arbitrary",)),
        name="nsa_compress",
    )(kc_flat, vc_flat, wk1, wv1, wk2, wv2, pek, pev)


def _nsa_select_kernel(q_ref, kcmp_ref, vcmpT_ref, bias_ref, ov_ref, ocmp_ref, neg_ref, *, tq):
    qi = pl.program_id(0)
    nc = kcmp_ref.shape[2]
    nsb = ov_ref.shape[0]
    t = lax.broadcasted_iota(jnp.int32, (1, tq), 1) + qi * tq
    cidx = lax.broadcasted_iota(jnp.int32, (nc, 1), 0)
    valid_c = (cidx * D_CMP + L_CMP - 1) <= t
    jf = lax.broadcasted_iota(jnp.int32, (nsb, tq), 0).astype(F32)
    jb = lax.broadcasted_iota(jnp.int32, (nsb, 1), 0)
    back = t // L_SLC - jb
    valid_b = back >= 0
    forced = (jb == 0) | (valid_b & (back < N_LOCAL))
    heads = range(H_NSA)
    groups = range(G_NSA)
    ss = [jnp.where(valid_c, _dot_nt(kcmp_ref[0, hh // HPG], q_ref[0, :, hh * LANE:(hh + 1) * LANE])
                    + bias_ref[hh], NEG_INF) for hh in heads]
    ms = [jnp.max(s, axis=0, keepdims=True) for s in ss]
    ps = [jnp.where(valid_c, jnp.exp(ss[hh] - ms[hh]), 0.0) for hh in heads]
    ls = [jnp.sum(p, axis=0, keepdims=True) for p in ps]
    ps = [(ps[hh] / jnp.maximum(ls[hh], 1e-30)).astype(BF16) for hh in heads]
    for hh in heads:
        ocmp_ref[0, hh] = _dot(vcmpT_ref[0, hh // HPG], ps[hh])
    imps = [_dot(ov_ref[...], ps[hh]) for hh in heads]
    group_imp = lambda g: functools.reduce(lambda a, b: a + b, imps[g * HPG:(g + 1) * HPG])
    scores = [jnp.where(valid_b, jnp.where(forced, FORCE_SCORE, group_imp(g)), -1.0) for g in groups]
    sels = [jnp.zeros((nsb, tq), F32) for _ in groups]
    for _ in range(N_SLC):
        mxs = [jnp.max(sc, axis=0, keepdims=True) for sc in scores]
        firsts = [jnp.min(jnp.where(scores[g] == mxs[g], jf, float(nsb)), axis=0, keepdims=True) for g in groups]
        picks = [jf == firsts[g] for g in groups]
        sels = [jnp.where(picks[g] & (mxs[g] >= 0.0), 1.0, sels[g]) for g in groups]
        scores = [jnp.where(picks[g], -2.0, scores[g]) for g in groups]
    for g in groups:
        neg = jnp.where(sels[g] > 0.0, 0.0, NEG_INF)
        pieces = [jnp.zeros((HEAD_DIM, tq), F32), neg]
        if nsb < HEAD_DIM:
            pieces.append(jnp.zeros((HEAD_DIM - nsb, tq), F32))
        neg_ref[0, g] = jnp.concatenate(pieces, axis=0).T.astype(BF16)


def _nsa_select_call(pb3, kcmp, vcmpT, bias_c, overlap_t, tq):
    B, S, _ = pb3.shape
    nc = kcmp.shape[2]
    W = H_NSA * LANE
    return pl.pallas_call(
        functools.partial(_nsa_select_kernel, tq=tq),
        out_shape=(jax.ShapeDtypeStruct((B, H_NSA, HEAD_DIM, S), F32),
                   jax.ShapeDtypeStruct((B, G_NSA, S, LANE), BF16)),
        grid=(S // tq, B),
        in_specs=[pl.BlockSpec((1, tq, W), lambda i, b: (b, i, PB_NQ // W)),
                  pl.BlockSpec((1, G_NSA, nc, LANE), lambda i, b: (b, 0, 0, 0)),
                  pl.BlockSpec((1, G_NSA, HEAD_DIM, nc), lambda i, b: (b, 0, 0, 0)),
                  pl.BlockSpec((H_NSA, nc, tq), lambda i, b: (0, 0, i)),
                  pl.BlockSpec(overlap_t.shape, lambda i, b: (0, 0))],
        out_specs=(pl.BlockSpec((1, H_NSA, HEAD_DIM, tq), lambda i, b: (b, 0, 0, i)),
                   pl.BlockSpec((1, G_NSA, tq, LANE), lambda i, b: (b, 0, i, 0))),
        compiler_params=_cparams(("arbitrary", "arbitrary")),
        name="nsa_select",
    )(pb3, kcmp, vcmpT, bias_c, overlap_t)


def _nsa_attn_kernel(q_ref, neg_ref, ks_ref, vs_ref, kw_ref, vw_ref, ocmp_ref, misc_ref, bias_ref,
                     o_ref, kaug_scr, vst_scr, vwt_scr, *, tq):
    qi = pl.program_id(1)
    S = ks_ref.shape[1]
    cols = HPG * tq
    n_tiles = S // tq
    assert WINDOW == 2 * tq

    @pl.when(qi == 0)
    def _():
        srow = lax.broadcasted_iota(jnp.int32, (tq, LANE), 0)
        slane = lax.broadcasted_iota(jnp.int32, (tq, LANE), 1)

        def fill(t, carry):
            start = pl.multiple_of(t * tq, tq)
            onehot = jnp.where(slane == HEAD_DIM + (srow + t * tq) // L_SLC, 1.0, 0.0).astype(BF16)
            for g in range(G_NSA):
                gl = slice(g * LANE, (g + 1) * LANE)
                kaug_scr[g, pl.ds(start, tq), :] = ks_ref[0, pl.ds(start, tq), gl] + onehot
                vst_scr[g, t] = vs_ref[0, pl.ds(start, tq), gl].astype(F32).T[:HEAD_DIM].astype(BF16)
                vwt_scr[g, t] = vw_ref[0, pl.ds(start, tq), gl].astype(F32).T[:HEAD_DIM].astype(BF16)
            return carry

        lax.fori_loop(0, n_tiles, fill, 0)

    jk = lax.broadcasted_iota(jnp.int32, (tq, cols), 0)
    ic = lax.broadcasted_iota(jnp.int32, (tq, cols), 1) % tq
    causal = jk <= ic
    in_window = jk > ic
    prev = jnp.maximum(qi - 1, 0)
    prev2 = jnp.maximum(qi - 2, 0)
    has_prev = qi >= 1
    has_prev2 = qi >= 2
    groups = range(G_NSA)
    qa = [jnp.concatenate([q_ref[0, :, (g * HPG + h) * LANE:(g * HPG + h + 1) * LANE] + neg_ref[0, g]
                           for h in range(HPG)], axis=0) for g in groups]

    def sel_logits(j, g):
        start = pl.multiple_of(j * tq, tq)
        return _dot_nt(kaug_scr[g, pl.ds(start, tq), :], qa[g])

    def win_logits(j, g):
        start = pl.multiple_of(j * tq, tq)
        return _dot_nt(kw_ref[0, pl.ds(start, tq), g * LANE:(g + 1) * LANE], qa[g])

    carries = tuple(_attn_init_t(HEAD_DIM, cols) for _ in range(2 * G_NSA))
    ss = ([jnp.where(causal, sel_logits(qi, g) + bias_ref[g, 0], NEG_INF) for g in groups]
          + [jnp.where(causal, win_logits(qi, g) + bias_ref[g, 0], NEG_INF) for g in groups])
    carries = _online_update_multi(carries, ss, [vst_scr[g, qi] for g in groups] + [vwt_scr[g, qi] for g in groups])
    ss = ([jnp.where(has_prev, sel_logits(prev, g) + bias_ref[g, 1], NEG_INF) for g in groups]
          + [jnp.where(has_prev, win_logits(prev, g) + bias_ref[g, 1], NEG_INF) for g in groups])
    carries = _online_update_multi(carries, ss,
                                   [vst_scr[g, prev] for g in groups] + [vwt_scr[g, prev] for g in groups])
    ss = [jnp.where(has_prev2 & in_window, win_logits(prev2, g), NEG_INF) for g in groups]
    win = _online_update_multi(carries[G_NSA:], ss, [vwt_scr[g, prev2] for g in groups])

    def body(j, c):
        return _online_update_multi(c, [sel_logits(j, g) for g in groups], [vst_scr[g, j] for g in groups])

    sel = lax.fori_loop(0, prev, body, carries[:G_NSA])

    gates = _sigmoid(misc_ref[0].T)
    outs = []
    for g in groups:
        o_slc = sel[g][2] / sel[g][1]
        o_win = win[g][2] / win[g][1]
        for h in range(HPG):
            hh = g * HPG + h
            base = MISC_GATE + hh * 3
            cs = slice(h * tq, (h + 1) * tq)
            outs.append(gates[base:base + 1] * ocmp_ref[0, hh] + gates[base + 1:base + 2] * o_slc[:, cs]
                        + gates[base + 2:base + 3] * o_win[:, cs])
    for pair in range(H_NSA // 2):
        both = jnp.concatenate([outs[2 * pair], outs[2 * pair + 1]], axis=0)
        o_ref[0, :, pair * LANE:(pair + 1) * LANE] = both.T.astype(o_ref.dtype)


def _nsa_attn_call(pb3, neg, ocmp, pf3, bias_n, tq):
    B, S, _ = pb3.shape
    W = H_NSA * LANE
    kv = lambda col: pl.BlockSpec((1, S, 2 * LANE), lambda b, i: (b, 0, col // (2 * LANE)))
    return pl.pallas_call(
        functools.partial(_nsa_attn_kernel, tq=tq),
        out_shape=jax.ShapeDtypeStruct((B, S, C_NSA), BF16),
        grid=(B, S // tq),
        in_specs=[pl.BlockSpec((1, tq, W), lambda b, i: (b, i, PB_NQ // W)),
                  pl.BlockSpec((1, G_NSA, tq, LANE), lambda b, i: (b, 0, i, 0)),
                  kv(PB_KS), kv(PB_VS), kv(PB_KW), kv(PB_VW),
                  pl.BlockSpec((1, H_NSA, HEAD_DIM, tq), lambda b, i: (b, 0, 0, i)),
                  pl.BlockSpec((1, tq, LANE), lambda b, i: (b, i, PF_MISC // LANE)),
                  pl.BlockSpec(bias_n.shape, lambda b, i: (0, 0, 0, 0))],
        out_specs=pl.BlockSpec((1, tq, C_NSA), lambda b, i: (b, i, 0)),
        scratch_shapes=[pltpu.VMEM((G_NSA, S, LANE), BF16),
                        pltpu.VMEM((G_NSA, S // tq, HEAD_DIM, tq), BF16),
                        pltpu.VMEM((G_NSA, S // tq, HEAD_DIM, tq), BF16)],
        compiler_params=_cparams(("arbitrary", "arbitrary")),
        name="nsa_attn",
    )(pb3, neg, pb3, pb3, pb3, pb3, ocmp, pf3, bias_n)


def _outproj_kernel(x_ref, ya_ref, yb_ref, yc_ref, w_ref, gm_ref, g_ref, o_ref):
    ca, cb = ya_ref.shape[1], yb_ref.shape[1]
    y = (_dot(ya_ref[...], w_ref[:ca]) + _dot(yb_ref[...], w_ref[ca:ca + cb])
         + _dot(yc_ref[...], w_ref[ca + cb:]))
    o_ref[...] = x_ref[...] + gm_ref[0] * _rmsnorm(y, g_ref[...])


def _outproj_call(x2d, ya, yb, yc, w, gm, g, S, tm=512):
    N, D = x2d.shape
    per = S // tm
    rows = lambda a: pl.BlockSpec((tm, a.shape[1]), lambda i: (i, 0))
    return pl.pallas_call(
        _outproj_kernel,
        out_shape=jax.ShapeDtypeStruct((N, D), F32),
        grid=(N // tm,),
        in_specs=[rows(x2d), rows(ya), rows(yb), rows(yc),
                  pl.BlockSpec(w.shape, lambda i: (0, 0)),
                  pl.BlockSpec((1, 1, D), lambda i: (i // per, 0, 0)),
                  pl.BlockSpec((1, D), lambda i: (0, 0))],
        out_specs=pl.BlockSpec((tm, D), lambda i: (i, 0)),
        compiler_params=_cparams(("arbitrary",)),
        name="out_proj",
    )(x2d, ya, yb, yc, w, gm, g.reshape(1, D))


def _ffn_kernel(x_ref, halo_ref, sc_ref, sh_ref, gf_ref, g2_ref, g3_ref, wg_ref, wv_ref,
                cwg_ref, cwv_ref, cbg_ref, cbv_ref, wd_ref, o_ref, h_scr, acc_scr, *, tm, rows_per_seq):
    i = pl.program_id(0)
    f = pl.program_id(1)

    @pl.when(f == 0)
    def _():
        xe = jnp.concatenate([halo_ref[...], x_ref[...]], axis=0)
        h = _rmsnorm(xe, g2_ref[...]) * (1.0 + sc_ref[0]) + sh_ref[0]
        row = lax.broadcasted_iota(jnp.int32, (tm + 8, 1), 0)
        first = (i * tm) % rows_per_seq == 0
        h_scr[...] = jnp.where((row < 8) & first, 0.0, h).astype(BF16)
        acc_scr[...] = jnp.zeros_like(acc_scr)

    h = h_scr[...]

    def conv(w_ref, cw_ref, cb_ref):
        u = _dot(h, w_ref[...])
        y = (cw_ref[2:3] * u + cw_ref[1:2] * pltpu.roll(u, 1, 0) + cw_ref[0:1] * pltpu.roll(u, 2, 0)
             + cb_ref[...])
        return y[8:]

    gate = conv(wg_ref, cwg_ref, cbg_ref)
    val = conv(wv_ref, cwv_ref, cbv_ref)
    act = (gate * _sigmoid(gate) * val).astype(BF16)
    acc_scr[...] += _dot(act, wd_ref[...])

    @pl.when(f == pl.num_programs(1) - 1)
    def _():
        o_ref[...] = x_ref[...] + gf_ref[0] * _rmsnorm(acc_scr[...], g3_ref[...])


def _ffn_call(x2d, sc, sh, gf, g2, g3, w_up, conv_w, conv_b, w_down, S, tm=512, tf=1408):
    N, D = x2d.shape
    F = w_down.shape[0]
    nf = F // tf
    per = S // tm
    mod = pl.BlockSpec((1, 1, D), lambda i, f: (i // per, 0, 0))
    vec = pl.BlockSpec((1, D), lambda i, f: (0, 0))
    cb = conv_b.reshape(1, 2 * F)
    return pl.pallas_call(
        functools.partial(_ffn_kernel, tm=tm, rows_per_seq=S),
        out_shape=jax.ShapeDtypeStruct((N, D), F32),
        grid=(N // tm, nf),
        in_specs=[pl.BlockSpec((tm, D), lambda i, f: (i, 0)),
                  pl.BlockSpec((8, D), lambda i, f: (jnp.maximum(i * (tm // 8) - 1, 0), 0)),
                  mod, mod, mod, vec, vec,
                  pl.BlockSpec((D, tf), lambda i, f: (0, f)),
                  pl.BlockSpec((D, tf), lambda i, f: (0, nf + f)),
                  pl.BlockSpec((CONV_W, tf), lambda i, f: (0, f)),
                  pl.BlockSpec((CONV_W, tf), lambda i, f: (0, nf + f)),
                  pl.BlockSpec((1, tf), lambda i, f: (0, f)),
                  pl.BlockSpec((1, tf), lambda i, f: (0, nf + f)),
                  pl.BlockSpec((tf, D), lambda i, f: (f, 0))],
        out_specs=pl.BlockSpec((tm, D), lambda i, f: (i, 0)),
        scratch_shapes=[pltpu.VMEM((tm + 8, D), BF16), pltpu.VMEM((tm, D), F32)],
        compiler_params=_cparams(("arbitrary", "arbitrary")),
        name="conv_ffn",
    )(x2d, x2d, sc, sh, gf, g2.reshape(1, D), g3.reshape(1, D), w_up, w_up, conv_w, conv_w, cb, cb, w_down)


def _column_maps():
    n_rwkv = 3 * C_RWKV + R_DECAY + R_AAA + R_GATE
    n_fox = 3 * C_FOX + H_FOX
    fox0 = n_rwkv
    nsa0 = n_rwkv + n_fox
    pf = np.full(PF_COLS, -1, np.int64)
    pf[PF_R:PF_R + 3 * C_RWKV] = np.arange(3 * C_RWKV)
    pf[PF_XW:PF_XW + R_DECAY] = 3 * C_RWKV + np.arange(R_DECAY)
    pf[PF_XA:PF_XA + R_AAA] = 3 * C_RWKV + R_DECAY + np.arange(R_AAA)
    pf[PF_XG:PF_XG + R_GATE] = 3 * C_RWKV + R_DECAY + R_AAA + np.arange(R_GATE)
    pf[PF_MISC + MISC_F:PF_MISC + MISC_F + H_FOX] = fox0 + 3 * C_FOX + np.arange(H_FOX)
    nsa_gate0 = nsa0 + C_NSA + 6 * G_NSA * HEAD_DIM
    pf[PF_MISC + MISC_GATE:PF_MISC + MISC_GATE + 3 * H_NSA] = nsa_gate0 + np.arange(3 * H_NSA)

    pb = np.full(PB_COLS, -1, np.int64)
    scale = np.ones(PB_COLS, np.float32)
    d = np.arange(HEAD_DIM)
    for h in range(H_FOX):
        pb[PB_FQ + h * LANE + d] = fox0 + h * HEAD_DIM + d
        scale[PB_FQ + h * LANE + d] = HEAD_DIM ** -0.5
        pb[PB_FK + h * LANE + d] = fox0 + C_FOX + h * HEAD_DIM + d
    pb[PB_FV:PB_FV + C_FOX] = fox0 + 2 * C_FOX + np.arange(C_FOX)
    for h in range(H_NSA):
        pb[PB_NQ + h * LANE + d] = nsa0 + h * HEAD_DIM + d
        scale[PB_NQ + h * LANE + d] = HEAD_DIM ** -0.5
    ckv = G_NSA * HEAD_DIM
    kc0 = nsa0 + C_NSA
    pb[PB_KC:PB_KC + ckv] = kc0 + np.arange(ckv)
    pb[PB_VC:PB_VC + ckv] = kc0 + ckv + np.arange(ckv)
    for n, base in enumerate((PB_KS, PB_VS, PB_KW, PB_VW)):
        for g in range(G_NSA):
            pb[base + g * LANE + d] = kc0 + (2 + n) * ckv + g * HEAD_DIM + d
    return pf, pb, scale


def _pad_rows(w, rows):
    return jnp.concatenate([w, jnp.zeros((rows - w.shape[0],) + w.shape[1:], w.dtype)], axis=0)


def _compress_w1(w1):
    hid = w1.shape[1]
    w = w1.reshape(2, D_CMP, HEAD_DIM, hid)
    out = jnp.zeros((G_NSA, 2, D_CMP, G_NSA, HEAD_DIM, hid), w1.dtype)
    for g in range(G_NSA):
        out = out.at[g, :, :, g].set(w)
    return out.reshape(G_NSA, 2, D_CMP * G_NSA * HEAD_DIM, hid).astype(BF16)


def _compress_pe(pe):
    half = pe.reshape(2, 1, D_CMP, 1, HEAD_DIM)
    return jnp.broadcast_to(half, (2, 8, D_CMP, G_NSA, HEAD_DIM)).reshape(2, 8, D_CMP * G_NSA * HEAD_DIM)


def _fox_place():
    place = np.zeros((3, LANE, H_FOX * LANE), np.float32)
    for j in range(3):
        for h in range(H_FOX):
            place[j, MISC_F + h, h * LANE + HEAD_DIM + j] = 1.0
    return jnp.asarray(place, BF16)


def _overlap_t(S):
    nc = S // D_CMP
    nsb = S // L_SLC
    c0 = np.arange(nc) * D_CMP
    c1 = c0 + L_CMP - 1
    s0 = np.arange(nsb) * L_SLC
    ov = (c0[None, :] <= s0[:, None] + L_SLC - 1) & (c1[None, :] >= s0[:, None])
    ov[:, nc - 1] = False
    return jnp.asarray(ov.astype(np.float32), BF16)


def kernel(x, c, ada_w, ada_b, norm_g, w_in, rwkv_mu, rwkv_w0, rwkv_w_up, rwkv_a0, rwkv_a_up, rwkv_g_up, rwkv_k_k, rwkv_k_a, rwkv_r_k, rwkv_ln_w, rwkv_ln_b, fox_b_f, nsa_pe_k, nsa_pe_v, nsa_ck_w1, nsa_ck_w2, nsa_cv_w1, nsa_cv_w2, rel_bias, w_out, ffn_up, ffn_conv_w, ffn_conv_b, ffn_down):
    B, S, D = x.shape
    L = w_in.shape[0]
    tq_sel = 256
    tq_nsa = WINDOW // 2
    assert S % 512 == 0 and S // L_SLC <= HEAD_DIM and D == 1024

    pf_idx, pb_idx, pb_scale = _column_maps()
    w_ext = jnp.concatenate([w_in, jnp.zeros((L, D, 1), w_in.dtype)], axis=2)
    w_pf = jnp.take(w_ext, jnp.asarray(pf_idx), axis=2).astype(BF16)
    w_pb = (jnp.take(w_ext, jnp.asarray(pb_idx), axis=2) * pb_scale).astype(BF16)
    mu_ext = jnp.concatenate([rwkv_mu, jnp.zeros((L, 1), F32)], axis=1)
    mu_pf = jnp.take(mu_ext, jnp.asarray(pf_idx[:PF_RWKV]), axis=1)

    mod_all = _mod_call(c, ada_w, ada_b).reshape(L, B, 6, 1, D)
    bias_c, bias_n = _bias_tables(rel_bias, S, tq_nsa)
    place = _fox_place()
    overlap_t = _overlap_t(S)

    x2d = x.reshape(B * S, D)
    for l in range(L):
        sh_m, sc_m, g_m, sh_f, sc_f, g_f = (mod_all[l, :, j] for j in range(6))
        pf = _inproj_call(x2d, sc_m, sh_m, norm_g[l, 0], w_pf[l], F32, S)
        pb = _inproj_call(x2d, sc_m, sh_m, norm_g[l, 0], w_pb[l], BF16, S)
        pf3 = pf.reshape(B, S, PF_COLS)
        pb3 = pb.reshape(B, S, PB_COLS)

        ry, mg, bonus, gate = _rwkv_local_call(
            pf, mu_pf[l], rwkv_w0[l], _pad_rows(rwkv_w_up[l], LANE).astype(BF16), rwkv_a0[l],
            _pad_rows(rwkv_a_up[l], LANE).astype(BF16), _pad_rows(rwkv_g_up[l], LANE).astype(BF16),
            rwkv_k_k[l], rwkv_k_a[l], rwkv_r_k[l], S)
        ya = _rwkv_scan_call(ry, mg, bonus, gate, rwkv_ln_w[l], rwkv_ln_b[l], B, S)

        b_f_row = jnp.zeros((1, LANE), F32).at[0, MISC_F:MISC_F + H_FOX].set(fox_b_f[l])
        kaug, fox_vt = _fox_prep_call(pf3, pb3, b_f_row, place)
        yb = _fox_attn_call(pb3, kaug, fox_vt)

        kc_flat = pb3[:, :, PB_KC:PB_KC + LANE].reshape(B, S // D_CMP, D_CMP * LANE)
        vc_flat = pb3[:, :, PB_VC:PB_VC + LANE].reshape(B, S // D_CMP, D_CMP * LANE)
        w2pad = lambda w: jnp.concatenate([w, jnp.zeros_like(w)], axis=1).astype(BF16)
        kcmp, vcmpT = _nsa_compress_call(
            kc_flat, vc_flat, _compress_w1(nsa_ck_w1[l]), _compress_w1(nsa_cv_w1[l]),
            w2pad(nsa_ck_w2[l]), nsa_cv_w2[l].T.astype(BF16),
            _compress_pe(nsa_pe_k[l]), _compress_pe(nsa_pe_v[l]))
        ocmp, neg = _nsa_select_call(pb3, kcmp, vcmpT, bias_c, overlap_t, tq_sel)
        yc = _nsa_attn_call(pb3, neg, ocmp, pf3, bias_n, tq_nsa)

        x2d = _outproj_call(x2d, ya.reshape(B * S, C_RWKV), yb.reshape(B * S, C_FOX),
                            yc.reshape(B * S, C_NSA), w_out[l].astype(BF16), g_m, norm_g[l, 1], S)
        x2d = _ffn_call(x2d, sc_f, sh_f, g_f, norm_g[l, 2], norm_g[l, 3], ffn_up[l].astype(BF16),
                        ffn_conv_w[l], ffn_conv_b[l], ffn_down[l].astype(BF16), S)
    return x2d.reshape(B, S, D)
```

```python
import functools
import math

import numpy as np
import jax
import jax.numpy as jnp
from jax import lax
from jax.experimental import pallas as pl
from jax.experimental.pallas import tpu as pltpu

F32 = jnp.float32
BF16 = jnp.bfloat16

HEAD_DIM = 64
H_RWKV = 4
C_RWKV = H_RWKV * HEAD_DIM
H_FOX = 6
C_FOX = H_FOX * HEAD_DIM
H_NSA = 6
C_NSA = H_NSA * HEAD_DIM
G_NSA = 2
HPG = H_NSA // G_NSA
R_DECAY = 32
R_AAA = 32
R_GATE = 64
L_CMP = 32
D_CMP = 16
CMP_HID = 128
L_SLC = 64
N_SLC = 16
N_LOCAL = 2
WINDOW = 512
NUM_BUCKETS = 32
MAX_DISTANCE = 128
CONV_W = 3
RMS_EPS = 1e-6
GN_EPS = 64e-5
NEG_INF = -1e30
FORCE_SCORE = 1e4

LANE = 128
CHUNK = 64
VMEM_LIMIT = 56 * 1024 * 1024

PF_R, PF_K, PF_V, PF_LORA, PF_MISC = 0, 256, 512, 768, 896
LORA_W, LORA_A, LORA_G = 0, 32, 64
PF_RWKV = 896
PF_COLS = 1024
MISC_F = 0
MISC_GATE = 8
PB_FQ, PB_FK, PB_FV, PB_NQ = 0, 384, 768, 1152
PB_KC, PB_VC, PB_KS, PB_VS, PB_KW, PB_VW = 1536, 1664, 1792, 1920, 2048, 2176
PB_COLS = 2304


def _bucket_lower_bounds():
    n = np.arange(0, 4 * MAX_DISTANCE, dtype=np.int64)
    max_exact = NUM_BUCKETS // 2
    nf = np.maximum(n, 1).astype(np.float32)
    large = max_exact + (np.log(nf / np.float32(max_exact)) / np.float32(math.log(MAX_DISTANCE / max_exact))
                         * np.float32(NUM_BUCKETS - max_exact)).astype(np.int32)
    large = np.minimum(large, NUM_BUCKETS - 1)
    bucket = np.where(n < max_exact, n, large)
    return [int(np.argmax(bucket >= b)) for b in range(NUM_BUCKETS)]


BUCKET_LB = _bucket_lower_bounds()
BIAS_CONST_DIST = BUCKET_LB[NUM_BUCKETS - 1]


def _cparams(sem, vmem=None):
    return pltpu.CompilerParams(dimension_semantics=sem, vmem_limit_bytes=vmem or VMEM_LIMIT)


def _dot(a, b):
    return jnp.dot(a, b, preferred_element_type=F32)


def _dot_nt(a, b):
    return lax.dot_general(a, b, (((1,), (1,)), ((), ())), preferred_element_type=F32)


def _split_bf16(x, n):
    parts, r = [], x
    for i in range(n):
        p = r.astype(BF16)
        parts.append(p)
        if i + 1 < n:
            r = r - p.astype(F32)
    return parts


def _dot_hp(a, b, nt=False):
    f = _dot_nt if nt else _dot
    ah, al = _split_bf16(a, 2)
    bh, bl = _split_bf16(b, 2)
    return f(ah, bh) + (f(ah, bl) + f(al, bh))


def _dot_exact_lhs(a_bf16, b, n):
    out = None
    for p in _split_bf16(b, n):
        t = _dot(a_bf16, p)
        out = t if out is None else out + t
    return out


def _softplus(x):
    return jnp.maximum(x, 0.0) + jnp.log(1.0 + jnp.exp(-jnp.abs(x)))


def _sigmoid(x):
    return 1.0 / (1.0 + jnp.exp(-x))


def _rmsnorm(x, g):
    return x * lax.rsqrt(jnp.mean(x * x, axis=-1, keepdims=True) + RMS_EPS) * g


def _mod_kernel(c_ref, w_ref, b_ref, o_ref):
    c = c_ref[...]
    s = (c * _sigmoid(c)).astype(BF16)
    o_ref[0] = _dot(s, w_ref[0].astype(BF16)) + b_ref[0]


def _mod_call(c, ada_w, ada_b):
    L, D, N = ada_w.shape
    B = c.shape[0]
    tn = 1536
    return pl.pallas_call(
        _mod_kernel,
        out_shape=jax.ShapeDtypeStruct((L, B, N), F32),
        grid=(L, N // tn),
        in_specs=[pl.BlockSpec((B, D), lambda l, j: (0, 0)),
                  pl.BlockSpec((1, D, tn), lambda l, j: (l, 0, j)),
                  pl.BlockSpec((1, 1, tn), lambda l, j: (l, 0, j))],
        out_specs=pl.BlockSpec((1, B, tn), lambda l, j: (l, 0, j)),
        compiler_params=_cparams(("arbitrary", "arbitrary")),
        name="adaln_mod",
    )(c, ada_w, ada_b.reshape(L, 1, N))


def _inproj_kernel(x_ref, sc_ref, sh_ref, g_ref, w_ref, o_ref, *, n_chunk):
    h = _rmsnorm(x_ref[...], g_ref[...]) * (1.0 + sc_ref[0]) + sh_ref[0]
    h = h.astype(BF16)
    n = o_ref.shape[1]
    for n0 in range(0, n, n_chunk):
        n1 = min(n, n0 + n_chunk)
        o_ref[:, n0:n1] = _dot(h, w_ref[:, n0:n1]).astype(o_ref.dtype)


def _inproj_call(x2d, sc, sh, g, w, out_dtype, S, tm=512):
    N, D = x2d.shape
    C = w.shape[1]
    per = S // tm
    return pl.pallas_call(
        functools.partial(_inproj_kernel, n_chunk=512),
        out_shape=jax.ShapeDtypeStruct((N, C), out_dtype),
        grid=(N // tm,),
        in_specs=[pl.BlockSpec((tm, D), lambda i: (i, 0)),
                  pl.BlockSpec((1, 1, D), lambda i: (i // per, 0, 0)),
                  pl.BlockSpec((1, 1, D), lambda i: (i // per, 0, 0)),
                  pl.BlockSpec((1, D), lambda i: (0, 0)),
                  pl.BlockSpec((D, C), lambda i: (0, 0))],
        out_specs=pl.BlockSpec((tm, C), lambda i: (i, 0)),
        compiler_params=_cparams(("arbitrary",)),
        name="in_proj",
    )(x2d, sc, sh, g.reshape(1, D), w)


def _rwkv_local_kernel(p_ref, halo_ref, mu_ref, w0_ref, wup_ref, a0_ref, aup_ref, gup_ref,
                       kk_ref, ka_ref, rk_ref,
                       ry_ref, mg_ref, bonus_ref, gate_ref, *, tr, rows_per_seq):
    C = C_RWKV
    i = pl.program_id(0)
    first = (i * tr) % rows_per_seq == 0
    p = p_ref[:, :PF_RWKV]
    row = lax.broadcasted_iota(jnp.int32, (tr, 1), 0)
    prev_last = jnp.where(first, 0.0, halo_ref[7:8, :PF_RWKV])
    prev = jnp.where(row == 0, prev_last, pltpu.roll(p, 1, 0))
    ps = p + (prev - p) * mu_ref[...]
    r = ps[:, PF_R:PF_R + C]
    k = ps[:, PF_K:PF_K + C]
    v = ps[:, PF_V:PF_V + C]
    lora = ps[:, PF_LORA:PF_LORA + LANE]
    wl = w0_ref[...] + _dot(jnp.tanh(lora).astype(BF16), wup_ref[...])
    lw = -jnp.exp(-_softplus(-wl) - 0.5)
    a = _sigmoid(a0_ref[...] + _dot(lora.astype(BF16), aup_ref[...]))
    gate_ref[...] = _dot(_sigmoid(lora).astype(BF16), gup_ref[...])

    li = lax.broadcasted_iota(jnp.int32, (C, C), 0) // HEAD_DIM
    lj = lax.broadcasted_iota(jnp.int32, (C, C), 1) // HEAD_DIM
    same_head = li == lj
    head_ones = jnp.where(same_head, 1.0, 0.0).astype(BF16)

    kk = k * kk_ref[...]
    nrm = jnp.sqrt(_dot_exact_lhs_t(kk * kk, head_ones))
    kk = kk / jnp.maximum(nrm, 1e-12)
    k2 = k * (1.0 + (a - 1.0) * ka_ref[...])
    bonus_ref[...] = _dot_exact_lhs_t(r * k2 * rk_ref[...], head_ones) * v
    avec = -kk
    bvec = kk * a

    nch = tr // CHUNK
    ti = lax.broadcasted_iota(jnp.int32, (tr, tr), 0)
    tj = lax.broadcasted_iota(jnp.int32, (tr, tr), 1)
    same_chunk = ti // CHUNK == tj // CHUNK
    tri_incl = jnp.where(same_chunk & (tj <= ti), 1.0, 0.0).astype(BF16)
    chunk_ones = jnp.where(same_chunk, 1.0, 0.0).astype(BF16)
    Lc = _dot_exact_lhs(tri_incl, lw, 3)
    Lend = _dot_exact_lhs(chunk_ones, lw, 3)
    e_cur = jnp.exp(Lc)
    e_inv = jnp.exp(-Lc)
    e_end = jnp.exp(Lend - Lc)
    At = avec * jnp.exp(Lc - lw)
    Rt = r * e_cur
    Bt = bvec * e_inv
    Kt = k2 * e_inv
    bh_t = (bvec * e_end).T
    kh_t = (k2 * e_end).T
    pc = jnp.exp(Lend)

    pairs = [(c, h) for c in range(nch) for h in range(H_RWKV)]
    blk = lambda x: jnp.stack([x[c * CHUNK:(c + 1) * CHUNK, h * HEAD_DIM:(h + 1) * HEAD_DIM]
                               for c, h in pairs]).astype(BF16)
    blk_t = lambda x: jnp.stack([x[h * HEAD_DIM:(h + 1) * HEAD_DIM, c * CHUNK:(c + 1) * CHUNK]
                                 for c, h in pairs]).astype(BF16)
    bmm = lambda a, b: jnp.einsum('nij,njk->nik', a.astype(BF16), b.astype(BF16), preferred_element_type=F32)
    bmm_nt = lambda a, b: jnp.einsum('nid,nkd->nik', a, b, preferred_element_type=F32)

    ci = lax.broadcasted_iota(jnp.int32, (CHUNK, CHUNK), 0)
    cj = lax.broadcasted_iota(jnp.int32, (CHUNK, CHUNK), 1)
    lower_strict = cj < ci
    lower_incl = cj <= ci
    a_b, r_b, b_b, k_b, v_b = blk(At), blk(Rt), blk(Bt), blk(Kt), blk(v)
    ar = jnp.concatenate([a_b, r_b], axis=1)
    ab = bmm_nt(ar, b_b)
    ak = bmm_nt(ar, k_b)
    n_mat = jnp.where(lower_strict, ab[:, :CHUNK], 0.0)
    a_ak = jnp.where(lower_strict, ak[:, :CHUNK], 0.0)
    a_rb = jnp.where(lower_incl, ab[:, CHUNK:], 0.0)
    a_rk = jnp.where(lower_incl, ak[:, CHUNK:], 0.0)
    av = bmm(jnp.concatenate([a_ak, a_rk], axis=1), v_b)
    x = jnp.concatenate([a_b.astype(F32), av[:, :CHUNK]], axis=2)
    x = x + bmm(n_mat, x)
    for _ in range(5):
        n_mat = bmm(n_mat, n_mat)
        x = x + bmm(n_mat, x)
    corr = bmm(a_rb, x)
    rbar = r_b.astype(F32) + corr[:, :, :HEAD_DIM]
    y0 = av[:, CHUNK:] + corr[:, :, HEAD_DIM:]
    m_mat = bmm(blk_t(bh_t), x)
    g_add = bmm(blk_t(kh_t), v_b)
    eye = ci == cj
    zero = jnp.zeros((HEAD_DIM, HEAD_DIM), F32)
    for c in range(nch):
        sl = slice(c * CHUNK, (c + 1) * CHUNK)
        ns = [c * H_RWKV + h for h in range(H_RWKV)]
        ry_ref[sl, :C] = jnp.concatenate([rbar[n] for n in ns], axis=1)
        ry_ref[sl, C:] = jnp.concatenate([y0[n] for n in ns], axis=1)
        for h, n in enumerate(ns):
            hs = slice(h * HEAD_DIM, (h + 1) * HEAD_DIM)
            pc_h = pc[c * CHUNK:c * CHUNK + 1, hs]
            m_h = jnp.where(eye, pc_h, 0.0) + m_mat[n, :, :HEAD_DIM]
            g_h = m_mat[n, :, HEAD_DIM:] + g_add[n]
            mg_ref[c, hs, :C] = jnp.concatenate([m_h if j == h else zero for j in range(H_RWKV)], axis=1)
            mg_ref[c, hs, C:] = jnp.concatenate([g_h if j == h else zero for j in range(H_RWKV)], axis=1)


def _dot_exact_lhs_t(x, ones_bf16):
    xh, xl = _split_bf16(x, 2)
    return _dot(xh, ones_bf16) + _dot(xl, ones_bf16)


def _rwkv_local_call(pf, mu, w0, wup, a0, aup, gup, k_k, k_a, r_k, S, tr=256):
    N = pf.shape[0]
    C = C_RWKV
    row = lambda a: a.reshape(1, -1)
    full = lambda a: pl.BlockSpec(a.shape, lambda i: (0,) * a.ndim)
    args = [row(mu), row(w0), wup, row(a0), aup, gup, row(k_k), row(k_a), row(r_k)]
    return pl.pallas_call(
        functools.partial(_rwkv_local_kernel, tr=tr, rows_per_seq=S),
        out_shape=(jax.ShapeDtypeStruct((N, 2 * C), F32),
                   jax.ShapeDtypeStruct((N // CHUNK, C, 2 * C), F32),
                   jax.ShapeDtypeStruct((N, C), F32),
                   jax.ShapeDtypeStruct((N, C), F32)),
        grid=(N // tr,),
        in_specs=[pl.BlockSpec((tr, PF_COLS), lambda i: (i, 0)),
                  pl.BlockSpec((8, PF_COLS), lambda i: (jnp.maximum(i * (tr // 8) - 1, 0), 0))]
                 + [full(a) for a in args],
        out_specs=(pl.BlockSpec((tr, 2 * C), lambda i: (i, 0)),
                   pl.BlockSpec((tr // CHUNK, C, 2 * C), lambda i: (i, 0, 0)),
                   pl.BlockSpec((tr, C), lambda i: (i, 0)),
                   pl.BlockSpec((tr, C), lambda i: (i, 0))),
        compiler_params=_cparams(("arbitrary",)),
        name="rwkv_local",
    )(pf, pf, *args)


def _rwkv_scan_kernel(ry_ref, mg_ref, bonus_ref, gate_ref, lnw_ref, lnb_ref, o_ref, s_scr):
    C = C_RWKV
    B = ry_ref.shape[0]

    @pl.when(pl.program_id(0) == 0)
    def _():
        s_scr[...] = jnp.zeros_like(s_scr)

    li = lax.broadcasted_iota(jnp.int32, (C, C), 0) // HEAD_DIM
    lj = lax.broadcasted_iota(jnp.int32, (C, C), 1) // HEAD_DIM
    head_avg = jnp.where(li == lj, 1.0 / HEAD_DIM, 0.0).astype(BF16)
    bs = range(B)
    s0 = [s_scr[b] for b in bs]
    ys = [_dot_hp(ry_ref[b, :, :C], s0[b]) + ry_ref[b, :, C:] for b in bs]
    for b in bs:
        s_scr[b] = _dot_hp(mg_ref[b, 0, :, :C], s0[b]) + mg_ref[b, 0, :, C:]
    means = [_dot_exact_lhs_t(ys[b], head_avg) for b in bs]
    ds = [ys[b] - means[b] for b in bs]
    vs = [_dot_exact_lhs_t(ds[b] * ds[b], head_avg) for b in bs]
    for b in bs:
        yn = ds[b] * lax.rsqrt(vs[b] + GN_EPS) * lnw_ref[...] + lnb_ref[...]
        o_ref[b] = ((yn + bonus_ref[b]) * gate_ref[b]).astype(o_ref.dtype)


def _rwkv_scan_call(ry, mg, bonus, gate, ln_w, ln_b, B, S):
    C = C_RWKV
    nc = S // CHUNK
    return pl.pallas_call(
        _rwkv_scan_kernel,
        out_shape=jax.ShapeDtypeStruct((B, S, C), BF16),
        grid=(nc,),
        in_specs=[pl.BlockSpec((B, CHUNK, 2 * C), lambda c: (0, c, 0)),
                  pl.BlockSpec((B, 1, C, 2 * C), lambda c: (0, c, 0, 0)),
                  pl.BlockSpec((B, CHUNK, C), lambda c: (0, c, 0)),
                  pl.BlockSpec((B, CHUNK, C), lambda c: (0, c, 0)),
                  pl.BlockSpec((1, C), lambda c: (0, 0)),
                  pl.BlockSpec((1, C), lambda c: (0, 0))],
        out_specs=pl.BlockSpec((B, CHUNK, C), lambda c: (0, c, 0)),
        scratch_shapes=[pltpu.VMEM((B, C, C), F32)],
        compiler_params=_cparams(("arbitrary",)),
        name="rwkv_scan",
    )(ry.reshape(B, S, 2 * C), mg.reshape(B, nc, C, 2 * C), bonus.reshape(B, S, C),
      gate.reshape(B, S, C), ln_w.reshape(1, C), ln_b.reshape(1, C))


def _online_update(carry, s, v):
    m, l, acc = carry
    m_new = jnp.maximum(m, jnp.max(s, axis=-1, keepdims=True))
    alpha = jnp.exp(m - m_new)
    p = jnp.exp(s - m_new)
    l = alpha * l + jnp.sum(p, axis=-1, keepdims=True)
    acc = alpha * acc + _dot(p.astype(BF16), v)
    return m_new, l, acc


def _attn_init(rows):
    return (jnp.full((rows, 1), NEG_INF, F32), jnp.zeros((rows, 1), F32), jnp.zeros((rows, LANE), F32))


def _online_update_t(carry, s_t, v_t):
    m, l, acc = carry
    m_new = jnp.maximum(m, jnp.max(s_t, axis=0, keepdims=True))
    alpha = jnp.exp(m - m_new)
    p = jnp.exp(s_t - m_new)
    l = alpha * l + jnp.sum(p, axis=0, keepdims=True)
    acc = alpha * acc + _dot(v_t, p.astype(BF16))
    return m_new, l, acc


def _attn_init_t(d, cols):
    return (jnp.full((1, cols), NEG_INF, F32), jnp.zeros((1, cols), F32), jnp.zeros((d, cols), F32))


def _online_update_multi(carries, ss, vts):
    n = range(len(ss))
    ms = [jnp.maximum(carries[i][0], jnp.max(ss[i], axis=0, keepdims=True)) for i in n]
    ps = [jnp.exp(ss[i] - ms[i]) for i in n]
    pvs = [_dot(vts[i], ps[i].astype(BF16)) for i in n]
    out = []
    for i in n:
        m, l, acc = carries[i]
        alpha = jnp.exp(m - ms[i])
        out.append((ms[i], alpha * l + jnp.sum(ps[i], axis=0, keepdims=True), alpha * acc + pvs[i]))
    return tuple(out)


def _fox_prep_kernel(misc_ref, q_ref, k_ref, v_ref, bf_ref, place_ref, spread_ref, ones_ref,
                     qa_ref, ka_ref, vt_ref, carry_scr, *, t, tk):
    @pl.when(pl.program_id(1) == 0)
    def _():
        carry_scr[...] = jnp.zeros_like(carry_scr)

    v_t = v_ref[0].astype(F32).T
    for h in range(H_FOX):
        for j in range(t // tk):
            vt_ref[0, h, j] = v_t[h * HEAD_DIM:(h + 1) * HEAD_DIM, j * tk:(j + 1) * tk].astype(BF16)

    lf = -_softplus(-(misc_ref[0] + bf_ref[...]))
    ti = lax.broadcasted_iota(jnp.int32, (t, t), 0)
    tj = lax.broadcasted_iota(jnp.int32, (t, t), 1)
    tri = jnp.where(tj <= ti, 1.0, 0.0).astype(BF16)
    cum = _dot_exact_lhs(tri, lf, 3) + carry_scr[...]
    carry_scr[...] = cum[t - 1:t]
    out = _dot(k_ref[0], spread_ref[...])
    for j, part in enumerate(_split_bf16(-cum, 3)):
        out = out + _dot(part, place_ref[j])
    ka_ref[0] = out.astype(BF16)
    qa_ref[0] = (_dot(q_ref[0], spread_ref[...]) + ones_ref[...]).astype(BF16)


FOX_TK = 512


def _fox_prep_call(pf3, pb3, b_f_row, place, spread, ones_row, t=512):
    B, S, _ = pf3.shape
    W = H_FOX * LANE
    tk = FOX_TK
    col = lambda c: pl.BlockSpec((1, t, C_FOX), lambda b, i: (b, i, c // C_FOX))
    const = lambda a: pl.BlockSpec(a.shape, lambda b, i: (0,) * a.ndim)
    return pl.pallas_call(
        functools.partial(_fox_prep_kernel, t=t, tk=tk),
        out_shape=(jax.ShapeDtypeStruct((B, S, W), BF16),
                   jax.ShapeDtypeStruct((B, S, W), BF16),
                   jax.ShapeDtypeStruct((B, H_FOX, S // tk, HEAD_DIM, tk), BF16)),
        grid=(B, S // t),
        in_specs=[pl.BlockSpec((1, t, LANE), lambda b, i: (b, i, PF_MISC // LANE)),
                  col(PB_FQ), col(PB_FK), col(PB_FV),
                  const(b_f_row), const(place), const(spread), const(ones_row)],
        out_specs=(pl.BlockSpec((1, t, W), lambda b, i: (b, i, 0)),
                   pl.BlockSpec((1, t, W), lambda b, i: (b, i, 0)),
                   pl.BlockSpec((1, H_FOX, t // tk, HEAD_DIM, tk), lambda b, i: (b, 0, i, 0, 0))),
        scratch_shapes=[pltpu.VMEM((1, LANE), F32)],
        compiler_params=_cparams(("arbitrary", "arbitrary")),
        name="fox_prep",
    )(pf3, pb3, pb3, pb3, b_f_row, place, spread, ones_row)


def _fox_attn_kernel(q_ref, k_ref, vt_ref, o_ref, *, tq, tk):
    qi = pl.program_id(1)
    krow = lax.broadcasted_iota(jnp.int32, (tk, tq), 0)
    qcol = lax.broadcasted_iota(jnp.int32, (tk, tq), 1)
    per = tq // tk
    heads = range(H_FOX)
    qs = [q_ref[0, :, h * LANE:(h + 1) * LANE] for h in heads]

    def logits(j, h):
        start = pl.multiple_of(j * tk, tk)
        return _dot_nt(k_ref[0, pl.ds(start, tk), h * LANE:(h + 1) * LANE], qs[h])

    carries = tuple(_attn_init_t(HEAD_DIM, tq) for _ in heads)
    for d in range(per):
        j = qi * per + d
        ss = [jnp.where(krow + d * tk <= qcol, logits(j, h), NEG_INF) for h in heads]
        carries = _online_update_multi(carries, ss, [vt_ref[0, h, j] for h in heads])

    def body(j, carries):
        return _online_update_multi(carries, [logits(j, h) for h in heads], [vt_ref[0, h, j] for h in heads])

    carries = lax.fori_loop(0, qi * per, body, carries)
    outs = [acc / l for (_, l, acc) in carries]
    for hp in range(H_FOX // 2):
        pair = jnp.concatenate([outs[2 * hp], outs[2 * hp + 1]], axis=0)
        o_ref[0, :, hp * LANE:(hp + 1) * LANE] = pair.T.astype(o_ref.dtype)


def _fox_attn_call(qaug, kaug, v_t, tq=512):
    B, S, W = qaug.shape
    tk = FOX_TK
    return pl.pallas_call(
        functools.partial(_fox_attn_kernel, tq=tq, tk=tk),
        out_shape=jax.ShapeDtypeStruct((B, S, C_FOX), BF16),
        grid=(B, S // tq),
        in_specs=[pl.BlockSpec((1, tq, W), lambda b, i: (b, i, 0)),
                  pl.BlockSpec((1, S, W), lambda b, i: (b, 0, 0)),
                  pl.BlockSpec((1, H_FOX, S // tk, HEAD_DIM, tk), lambda b, i: (b, 0, 0, 0, 0))],
        out_specs=pl.BlockSpec((1, tq, C_FOX), lambda b, i: (b, i, 0)),
        compiler_params=_cparams(("arbitrary", "arbitrary")),
        name="fox_attn",
    )(qaug, kaug, v_t)


def _bias_of_dist(n, tab_ref, h):
    val = jnp.zeros(n.shape, F32) + tab_ref[0, h]
    for b in range(1, NUM_BUCKETS):
        val = jnp.where(n >= BUCKET_LB[b], tab_ref[b, h], val)
    return val - tab_ref[NUM_BUCKETS - 1, h]


def _bias_cmp_kernel(tab_ref, o_ref, *, tt):
    t0 = pl.program_id(0) * tt
    nc = o_ref.shape[1]
    c = lax.broadcasted_iota(jnp.int32, (nc, tt), 0)
    t = lax.broadcasted_iota(jnp.int32, (nc, tt), 1) + t0
    n = jnp.maximum(t - (c * D_CMP + L_CMP - 1), 0)
    for h in range(H_NSA):
        o_ref[h] = _bias_of_dist(n, tab_ref, h)


def _bias_near_kernel(tab_ref, o_ref, *, tq):
    j = lax.broadcasted_iota(jnp.int32, (tq, tq), 0)
    i = lax.broadcasted_iota(jnp.int32, (tq, tq), 1)
    for g in range(G_NSA):
        for near in range(2):
            n = jnp.maximum(i - j + near * tq, 0)
            for h in range(HPG):
                o_ref[g, near, :, h * tq:(h + 1) * tq] = _bias_of_dist(n, tab_ref, g * HPG + h)


def _bias_tables(rel_bias, S, tq):
    nc = S // D_CMP
    tt = 512
    smem = pl.BlockSpec(memory_space=pltpu.SMEM)
    bias_c = pl.pallas_call(
        functools.partial(_bias_cmp_kernel, tt=tt),
        out_shape=jax.ShapeDtypeStruct((H_NSA, nc, S), F32),
        grid=(S // tt,),
        in_specs=[smem],
        out_specs=pl.BlockSpec((H_NSA, nc, tt), lambda i: (0, 0, i)),
        compiler_params=_cparams(("arbitrary",)),
        name="nsa_bias_cmp",
    )(rel_bias)
    bias_n = pl.pallas_call(
        functools.partial(_bias_near_kernel, tq=tq),
        out_shape=jax.ShapeDtypeStruct((G_NSA, 2, tq, HPG * tq), F32),
        in_specs=[smem],
        name="nsa_bias_near",
    )(rel_bias)
    return bias_c, bias_n


def _gelu_tanh(x):
    return 0.5 * x * (1.0 + jnp.tanh(math.sqrt(2.0 / math.pi) * (x + 0.044715 * (x * x * x))))


def _nsa_compress_kernel(kc_ref, vc_ref, wk1_ref, wv1_ref, wk2_ref, wv2_ref, pek_ref, pev_ref,
                         kcmp_ref, vcmpT_ref):
    nc = kc_ref.shape[1]
    for g in range(G_NSA):
        for src_ref, w1_ref, w2_ref, pe_ref, is_k in ((kc_ref, wk1_ref, wk2_ref, pek_ref, True),
                                                      (vc_ref, wv1_ref, wv2_ref, pev_ref, False)):
            ch = src_ref[0]
            p1 = _dot(ch, w1_ref[g, 0])
            p2 = _dot(ch, w1_ref[g, 1])
            pec = (_dot(pe_ref[0], w1_ref[g, 0].astype(F32)) + _dot(pe_ref[1], w1_ref[g, 1].astype(F32)))[0:1]
            hid = p1 + pltpu.roll(p2, nc - 1, 0) + pec
            act = _gelu_tanh(hid).astype(BF16)
            if is_k:
                kcmp_ref[0, g] = _dot(act, w2_ref[...]).astype(BF16)
            else:
                vcmpT_ref[0, g] = _dot_nt(w2_ref[...], act).astype(BF16)


def _nsa_compress_call(kc_flat, vc_flat, wk1, wv1, wk2, wv2, pek, pev):
    B, nc, W = kc_flat.shape
    full = lambda a: pl.BlockSpec(a.shape, lambda b: (0,) * a.ndim)
    return pl.pallas_call(
        _nsa_compress_kernel,
        out_shape=(jax.ShapeDtypeStruct((B, G_NSA, nc, HEAD_DIM), BF16),
                   jax.ShapeDtypeStruct((B, G_NSA, HEAD_DIM, nc), BF16)),
        grid=(B,),
        in_specs=[pl.BlockSpec((1, nc, W), lambda b: (b, 0, 0)),
                  pl.BlockSpec((1, nc, W), lambda b: (b, 0, 0)),
                  full(wk1), full(wv1), full(wk2), full(wv2), full(pek), full(pev)],
        out_specs=(pl.BlockSpec((1, G_NSA, nc, HEAD_DIM), lambda b: (b, 0, 0, 0)),
                   pl.BlockSpec((1, G_NSA, HEAD_DIM, nc), lambda b: (b, 0, 0, 0))),
        compiler_params=_cparams(("arbitrary",)),
        name="nsa_compress",
    )(kc_flat, vc_flat, wk1, wv1, wk2, wv2, pek, pev)


def _nsa_select_kernel(q_ref, kcmp_ref, vcmpT_ref, bias_ref, ov_ref, ocmp_ref, neg_ref, *, tq):
    qi = pl.program_id(0)
    nc = kcmp_ref.shape[2]
    nsb = ov_ref.shape[0]
    t = lax.broadcasted_iota(jnp.int32, (1, tq), 1) + qi * tq
    cidx = lax.broadcasted_iota(jnp.int32, (nc, 1), 0)
    valid_c = (cidx * D_CMP + L_CMP - 1) <= t
    jf = lax.broadcasted_iota(jnp.int32, (nsb, tq), 0).astype(F32)
    jb = lax.broadcasted_iota(jnp.int32, (nsb, 1), 0)
    back = t // L_SLC - jb
    valid_b = back >= 0
    forced = (jb == 0) | (valid_b & (back < N_LOCAL))
    heads = range(H_NSA)
    groups = range(G_NSA)
    ss = [jnp.where(valid_c, _dot_nt(kcmp_ref[0, hh // HPG], q_ref[0, :, hh * HEAD_DIM:(hh + 1) * HEAD_DIM])
                    + bias_ref[hh], NEG_INF) for hh in heads]
    ms = [jnp.max(s, axis=0, keepdims=True) for s in ss]
    ps = [jnp.where(valid_c, jnp.exp(ss[hh] - ms[hh]), 0.0) for hh in heads]
    ls = [jnp.sum(p, axis=0, keepdims=True) for p in ps]
    ps = [(ps[hh] / jnp.maximum(ls[hh], 1e-30)).astype(BF16) for hh in heads]
    for hh in heads:
        ocmp_ref[0, hh] = _dot(vcmpT_ref[0, hh // HPG], ps[hh])
    imps = [_dot(ov_ref[...], ps[hh]) for hh in heads]
    group_imp = lambda g: functools.reduce(lambda a, b: a + b, imps[g * HPG:(g + 1) * HPG])
    scores = [jnp.where(valid_b, jnp.where(forced, FORCE_SCORE, group_imp(g)), -1.0) for g in groups]
    sels = [jnp.zeros((nsb, tq), F32) for _ in groups]
    for _ in range(N_SLC):
        mxs = [jnp.max(sc, axis=0, keepdims=True) for sc in scores]
        firsts = [jnp.min(jnp.where(scores[g] == mxs[g], jf, float(nsb)), axis=0, keepdims=True) for g in groups]
        picks = [jf == firsts[g] for g in groups]
        sels = [jnp.where(picks[g] & (mxs[g] >= 0.0), 1.0, sels[g]) for g in groups]
        scores = [jnp.where(picks[g], -2.0, scores[g]) for g in groups]
    for g in groups:
        neg = jnp.where(sels[g] > 0.0, 0.0, NEG_INF)
        pieces = [jnp.zeros((HEAD_DIM, tq), F32), neg]
        if nsb < HEAD_DIM:
            pieces.append(jnp.zeros((HEAD_DIM - nsb, tq), F32))
        neg_ref[0, g] = jnp.concatenate(pieces, axis=0).T.astype(BF16)


def _nsa_select_call(pb3, kcmp, vcmpT, bias_c, overlap_t, tq):
    B, S, _ = pb3.shape
    nc = kcmp.shape[2]
    return pl.pallas_call(
        functools.partial(_nsa_select_kernel, tq=tq),
        out_shape=(jax.ShapeDtypeStruct((B, H_NSA, HEAD_DIM, S), F32),
                   jax.ShapeDtypeStruct((B, G_NSA, S, LANE), BF16)),
        grid=(S // tq, B),
        in_specs=[pl.BlockSpec((1, tq, C_NSA), lambda i, b: (b, i, PB_NQ // C_NSA)),
                  pl.BlockSpec((1, G_NSA, nc, HEAD_DIM), lambda i, b: (b, 0, 0, 0)),
                  pl.BlockSpec((1, G_NSA, HEAD_DIM, nc), lambda i, b: (b, 0, 0, 0)),
                  pl.BlockSpec((H_NSA, nc, tq), lambda i, b: (0, 0, i)),
                  pl.BlockSpec(overlap_t.shape, lambda i, b: (0, 0))],
        out_specs=(pl.BlockSpec((1, H_NSA, HEAD_DIM, tq), lambda i, b: (b, 0, 0, i)),
                   pl.BlockSpec((1, G_NSA, tq, LANE), lambda i, b: (b, 0, i, 0))),
        compiler_params=_cparams(("arbitrary", "arbitrary")),
        name="nsa_select",
    )(pb3, kcmp, vcmpT, bias_c, overlap_t)


def _nsa_attn_kernel(q_ref, neg_ref, ks_ref, vs_ref, kw_ref, vw_ref, ocmp_ref, misc_ref, bias_ref, spread_ref,
                     o_ref, kaug_scr, kwin_scr, vst_scr, vwt_scr, *, tq):
    qi = pl.program_id(1)
    S = ks_ref.shape[1]
    cols = HPG * tq
    n_tiles = S // tq
    assert WINDOW == 2 * tq

    @pl.when(qi == 0)
    def _():
        srow = lax.broadcasted_iota(jnp.int32, (tq, LANE), 0)
        slane = lax.broadcasted_iota(jnp.int32, (tq, LANE), 1)

        def fill(t, carry):
            start = pl.multiple_of(t * tq, tq)
            onehot = jnp.where(slane == HEAD_DIM + (srow + t * tq) // L_SLC, 1.0, 0.0)
            ks = ks_ref[0, pl.ds(start, tq), :].astype(F32)
            kw = kw_ref[0, pl.ds(start, tq), :].astype(F32)
            vs_t = vs_ref[0, pl.ds(start, tq), :].astype(F32).T
            vw_t = vw_ref[0, pl.ds(start, tq), :].astype(F32).T
            for g in range(G_NSA):
                ks_g = ks if g == 0 else pltpu.roll(ks, LANE - g * HEAD_DIM, 1)
                kw_g = kw if g == 0 else pltpu.roll(kw, LANE - g * HEAD_DIM, 1)
                kaug_scr[g, pl.ds(start, tq), :] = (jnp.where(slane < HEAD_DIM, ks_g, 0.0) + onehot).astype(BF16)
                kwin_scr[g, pl.ds(start, tq), :] = jnp.where(slane < HEAD_DIM, kw_g, 0.0).astype(BF16)
                vst_scr[g, t] = vs_t[g * HEAD_DIM:(g + 1) * HEAD_DIM].astype(BF16)
                vwt_scr[g, t] = vw_t[g * HEAD_DIM:(g + 1) * HEAD_DIM].astype(BF16)
            return carry

        lax.fori_loop(0, n_tiles, fill, 0)

    jk = lax.broadcasted_iota(jnp.int32, (tq, cols), 0)
    ic = lax.broadcasted_iota(jnp.int32, (tq, cols), 1) % tq
    causal = jk <= ic
    in_window = jk > ic
    prev = jnp.maximum(qi - 1, 0)
    prev2 = jnp.maximum(qi - 2, 0)
    has_prev = qi >= 1
    has_prev2 = qi >= 2
    groups = range(G_NSA)
    q_wide = _dot(q_ref[0], spread_ref[...])
    qa = [jnp.concatenate([(q_wide[:, (g * HPG + h) * LANE:(g * HPG + h + 1) * LANE]
                            + neg_ref[0, g].astype(F32)).astype(BF16)
                           for h in range(HPG)], axis=0) for g in groups]

    def sel_logits(j, g):
        start = pl.multiple_of(j * tq, tq)
        return _dot_nt(kaug_scr[g, pl.ds(start, tq), :], qa[g])

    def win_logits(j, g):
        start = pl.multiple_of(j * tq, tq)
        return _dot_nt(kwin_scr[g, pl.ds(start, tq), :], qa[g])

    carries = tuple(_attn_init_t(HEAD_DIM, cols) for _ in range(2 * G_NSA))
    ss = ([jnp.where(causal, sel_logits(qi, g) + bias_ref[g, 0], NEG_INF) for g in groups]
          + [jnp.where(causal, win_logits(qi, g) + bias_ref[g, 0], NEG_INF) for g in groups])
    carries = _online_update_multi(carries, ss, [vst_scr[g, qi] for g in groups] + [vwt_scr[g, qi] for g in groups])
    ss = ([jnp.where(has_prev, sel_logits(prev, g) + bias_ref[g, 1], NEG_INF) for g in groups]
          + [jnp.where(has_prev, win_logits(prev, g) + bias_ref[g, 1], NEG_INF) for g in groups])
    carries = _online_update_multi(carries, ss,
                                   [vst_scr[g, prev] for g in groups] + [vwt_scr[g, prev] for g in groups])
    ss = [jnp.where(has_prev2 & in_window, win_logits(prev2, g), NEG_INF) for g in groups]
    win = _online_update_multi(carries[G_NSA:], ss, [vwt_scr[g, prev2] for g in groups])

    def body(j, c):
        return _online_update_multi(c, [sel_logits(j, g) for g in groups], [vst_scr[g, j] for g in groups])

    sel = lax.fori_loop(0, prev, body, carries[:G_NSA])

    gates = _sigmoid(misc_ref[0].T)
    outs = []
    for g in groups:
        o_slc = sel[g][2] / sel[g][1]
        o_win = win[g][2] / win[g][1]
        for h in range(HPG):
            hh = g * HPG + h
            base = MISC_GATE + hh * 3
            cs = slice(h * tq, (h + 1) * tq)
            outs.append(gates[base:base + 1] * ocmp_ref[0, hh] + gates[base + 1:base + 2] * o_slc[:, cs]
                        + gates[base + 2:base + 3] * o_win[:, cs])
    for pair in range(H_NSA // 2):
        both = jnp.concatenate([outs[2 * pair], outs[2 * pair + 1]], axis=0)
        o_ref[0, :, pair * LANE:(pair + 1) * LANE] = both.T.astype(o_ref.dtype)


def _nsa_attn_call(pb3, neg, ocmp, pf3, bias_n, spread, tq):
    B, S, _ = pb3.shape
    kv = lambda col: pl.BlockSpec((1, S, LANE), lambda b, i: (b, 0, col // LANE))
    return pl.pallas_call(
        functools.partial(_nsa_attn_kernel, tq=tq),
        out_shape=jax.ShapeDtypeStruct((B, S, C_NSA), BF16),
        grid=(B, S // tq),
        in_specs=[pl.BlockSpec((1, tq, C_NSA), lambda b, i: (b, i, PB_NQ // C_NSA)),
                  pl.BlockSpec((1, G_NSA, tq, LANE), lambda b, i: (b, 0, i, 0)),
                  kv(PB_KS), kv(PB_VS), kv(PB_KW), kv(PB_VW),
                  pl.BlockSpec((1, H_NSA, HEAD_DIM, tq), lambda b, i: (b, 0, 0, i)),
                  pl.BlockSpec((1, tq, LANE), lambda b, i: (b, i, PF_MISC // LANE)),
                  pl.BlockSpec(bias_n.shape, lambda b, i: (0, 0, 0, 0)),
                  pl.BlockSpec(spread.shape, lambda b, i: (0, 0))],
        out_specs=pl.BlockSpec((1, tq, C_NSA), lambda b, i: (b, i, 0)),
        scratch_shapes=[pltpu.VMEM((G_NSA, S, LANE), BF16),
                        pltpu.VMEM((G_NSA, S, LANE), BF16),
                        pltpu.VMEM((G_NSA, S // tq, HEAD_DIM, tq), BF16),
                        pltpu.VMEM((G_NSA, S // tq, HEAD_DIM, tq), BF16)],
        compiler_params=_cparams(("arbitrary", "arbitrary")),
        name="nsa_attn",
    )(pb3, neg, pb3, pb3, pb3, pb3, ocmp, pf3, bias_n, spread)


def _outproj_kernel(x_ref, ya_ref, yb_ref, yc_ref, w_ref, gm_ref, g_ref, o_ref):
    y_cat = jnp.concatenate([ya_ref[...], yb_ref[...], yc_ref[...]], axis=1)
    o_ref[...] = x_ref[...] + gm_ref[0] * _rmsnorm(_dot(y_cat, w_ref[...]), g_ref[...])


def _outproj_call(x2d, ya, yb, yc, w, gm, g, S, tm=512):
    N, D = x2d.shape
    per = S // tm
    rows = lambda a: pl.BlockSpec((tm, a.shape[1]), lambda i: (i, 0))
    return pl.pallas_call(
        _outproj_kernel,
        out_shape=jax.ShapeDtypeStruct((N, D), F32),
        grid=(N // tm,),
        in_specs=[rows(x2d), rows(ya), rows(yb), rows(yc),
                  pl.BlockSpec(w.shape, lambda i: (0, 0)),
                  pl.BlockSpec((1, 1, D), lambda i: (i // per, 0, 0)),
                  pl.BlockSpec((1, D), lambda i: (0, 0))],
        out_specs=pl.BlockSpec((tm, D), lambda i: (i, 0)),
        compiler_params=_cparams(("arbitrary",)),
        name="out_proj",
    )(x2d, ya, yb, yc, w, gm, g.reshape(1, D))


def _ffn_kernel(x_ref, halo_ref, sc_ref, sh_ref, gf_ref, g2_ref, g3_ref, wg_ref, wv_ref,
                cwg_ref, cwv_ref, cbg_ref, cbv_ref, wd_ref, o_ref, h_scr, acc_scr, *, tm, rows_per_seq):
    i = pl.program_id(0)
    f = pl.program_id(1)

    @pl.when(f == 0)
    def _():
        xe = jnp.concatenate([halo_ref[...], x_ref[...]], axis=0)
        h = _rmsnorm(xe, g2_ref[...]) * (1.0 + sc_ref[0]) + sh_ref[0]
        row = lax.broadcasted_iota(jnp.int32, (tm + 8, 1), 0)
        first = (i * tm) % rows_per_seq == 0
        h_scr[...] = jnp.where((row < 8) & first, 0.0, h).astype(BF16)
        acc_scr[...] = jnp.zeros_like(acc_scr)

    h = h_scr[...]

    def conv(w_ref, cw_ref, cb_ref):
        u = _dot(h, w_ref[...])
        y = (cw_ref[2:3] * u + cw_ref[1:2] * pltpu.roll(u, 1, 0) + cw_ref[0:1] * pltpu.roll(u, 2, 0)
             + cb_ref[...])
        return y[8:]

    gate = conv(wg_ref, cwg_ref, cbg_ref)
    val = conv(wv_ref, cwv_ref, cbv_ref)
    act = (gate * _sigmoid(gate) * val).astype(BF16)
    acc_scr[...] += _dot(act, wd_ref[...])

    @pl.when(f == pl.num_programs(1) - 1)
    def _():
        o_ref[...] = x_ref[...] + gf_ref[0] * _rmsnorm(acc_scr[...], g3_ref[...])


def _ffn_call(x2d, sc, sh, gf, g2, g3, w_up, conv_w, conv_b, w_down, S, tm=512, tf=1408):
    N, D = x2d.shape
    F = w_down.shape[0]
    nf = F // tf
    per = S // tm
    mod = pl.BlockSpec((1, 1, D), lambda i, f: (i // per, 0, 0))
    vec = pl.BlockSpec((1, D), lambda i, f: (0, 0))
    cb = conv_b.reshape(1, 2 * F)
    return pl.pallas_call(
        functools.partial(_ffn_kernel, tm=tm, rows_per_seq=S),
        out_shape=jax.ShapeDtypeStruct((N, D), F32),
        grid=(N // tm, nf),
        in_specs=[pl.BlockSpec((tm, D), lambda i, f: (i, 0)),
                  pl.BlockSpec((8, D), lambda i, f: (jnp.maximum(i * (tm // 8) - 1, 0), 0)),
                  mod, mod, mod, vec, vec,
                  pl.BlockSpec((D, tf), lambda i, f: (0, f)),
                  pl.BlockSpec((D, tf), lambda i, f: (0, nf + f)),
                  pl.BlockSpec((CONV_W, tf), lambda i, f: (0, f)),
                  pl.BlockSpec((CONV_W, tf), lambda i, f: (0, nf + f)),
                  pl.BlockSpec((1, tf), lambda i, f: (0, f)),
                  pl.BlockSpec((1, tf), lambda i, f: (0, nf + f)),
                  pl.BlockSpec((tf, D), lambda i, f: (f, 0))],
        out_specs=pl.BlockSpec((tm, D), lambda i, f: (i, 0)),
        scratch_shapes=[pltpu.VMEM((tm + 8, D), BF16), pltpu.VMEM((tm, D), F32)],
        compiler_params=_cparams(("arbitrary", "arbitrary")),
        name="conv_ffn",
    )(x2d, x2d, sc, sh, gf, g2.reshape(1, D), g3.reshape(1, D), w_up, w_up, conv_w, conv_w, cb, cb, w_down)


def _column_maps():
    n_rwkv = 3 * C_RWKV + R_DECAY + R_AAA + R_GATE
    n_fox = 3 * C_FOX + H_FOX
    fox0 = n_rwkv
    nsa0 = n_rwkv + n_fox
    pf = np.full(PF_COLS, -1, np.int64)
    pf[PF_R:PF_R + 3 * C_RWKV] = np.arange(3 * C_RWKV)
    assert (LORA_A, LORA_G, LANE) == (R_DECAY, R_DECAY + R_AAA, R_DECAY + R_AAA + R_GATE)
    pf[PF_LORA:PF_LORA + LANE] = 3 * C_RWKV + np.arange(LANE)
    pf[PF_MISC + MISC_F:PF_MISC + MISC_F + H_FOX] = fox0 + 3 * C_FOX + np.arange(H_FOX)
    nsa_gate0 = nsa0 + C_NSA + 6 * G_NSA * HEAD_DIM
    pf[PF_MISC + MISC_GATE:PF_MISC + MISC_GATE + 3 * H_NSA] = nsa_gate0 + np.arange(3 * H_NSA)

    pb = np.full(PB_COLS, -1, np.int64)
    scale = np.ones(PB_COLS, np.float32)
    d = np.arange(HEAD_DIM)
    del d
    pb[PB_FQ:PB_FQ + C_FOX] = fox0 + np.arange(C_FOX)
    scale[PB_FQ:PB_FQ + C_FOX] = HEAD_DIM ** -0.5
    pb[PB_FK:PB_FK + C_FOX] = fox0 + C_FOX + np.arange(C_FOX)
    pb[PB_FV:PB_FV + C_FOX] = fox0 + 2 * C_FOX + np.arange(C_FOX)
    pb[PB_NQ:PB_NQ + C_NSA] = nsa0 + np.arange(C_NSA)
    scale[PB_NQ:PB_NQ + C_NSA] = HEAD_DIM ** -0.5
    ckv = G_NSA * HEAD_DIM
    kc0 = nsa0 + C_NSA
    for n, base in enumerate((PB_KC, PB_VC, PB_KS, PB_VS, PB_KW, PB_VW)):
        pb[base:base + ckv] = kc0 + n * ckv + np.arange(ckv)
    return pf, pb, scale


def _lora_rows(w, lane0):
    out = jnp.zeros((LANE, w.shape[1]), w.dtype)
    return out.at[lane0:lane0 + w.shape[0]].set(w).astype(BF16)


def _pad_rows(w, rows):
    return jnp.concatenate([w, jnp.zeros((rows - w.shape[0],) + w.shape[1:], w.dtype)], axis=0)


def _compress_w1(w1):
    hid = w1.shape[1]
    w = w1.reshape(2, D_CMP, HEAD_DIM, hid)
    out = jnp.zeros((G_NSA, 2, D_CMP, G_NSA, HEAD_DIM, hid), w1.dtype)
    for g in range(G_NSA):
        out = out.at[g, :, :, g].set(w)
    return out.reshape(G_NSA, 2, D_CMP * G_NSA * HEAD_DIM, hid).astype(BF16)


def _compress_pe(pe):
    half = pe.reshape(2, 1, D_CMP, 1, HEAD_DIM)
    return jnp.broadcast_to(half, (2, 8, D_CMP, G_NSA, HEAD_DIM)).reshape(2, 8, D_CMP * G_NSA * HEAD_DIM)


def _fox_place():
    place = np.zeros((3, LANE, H_FOX * LANE), np.float32)
    for j in range(3):
        for h in range(H_FOX):
            place[j, MISC_F + h, h * LANE + HEAD_DIM + j] = 1.0
    return jnp.asarray(place, BF16)


def _fox_spread():
    spread = np.zeros((C_FOX, H_FOX * LANE), np.float32)
    ones_row = np.zeros((1, H_FOX * LANE), np.float32)
    d = np.arange(HEAD_DIM)
    for h in range(H_FOX):
        spread[h * HEAD_DIM + d, h * LANE + d] = 1.0
        ones_row[0, h * LANE + HEAD_DIM:h * LANE + HEAD_DIM + 3] = 1.0
    return jnp.asarray(spread, BF16), jnp.asarray(ones_row, F32)


def _overlap_t(S):
    nc = S // D_CMP
    nsb = S // L_SLC
    c0 = np.arange(nc) * D_CMP
    c1 = c0 + L_CMP - 1
    s0 = np.arange(nsb) * L_SLC
    ov = (c0[None, :] <= s0[:, None] + L_SLC - 1) & (c1[None, :] >= s0[:, None])
    ov[:, nc - 1] = False
    return jnp.asarray(ov.astype(np.float32), BF16)


def kernel(x, c, ada_w, ada_b, norm_g, w_in, rwkv_mu, rwkv_w0, rwkv_w_up, rwkv_a0, rwkv_a_up, rwkv_g_up, rwkv_k_k, rwkv_k_a, rwkv_r_k, rwkv_ln_w, rwkv_ln_b, fox_b_f, nsa_pe_k, nsa_pe_v, nsa_ck_w1, nsa_ck_w2, nsa_cv_w1, nsa_cv_w2, rel_bias, w_out, ffn_up, ffn_conv_w, ffn_conv_b, ffn_down):
    B, S, D = x.shape
    L = w_in.shape[0]
    tq_sel = 256
    tq_nsa = WINDOW // 2
    assert S % 512 == 0 and S // L_SLC <= HEAD_DIM and D == 1024

    pf_idx, pb_idx, pb_scale = _column_maps()
    w_ext = jnp.concatenate([w_in, jnp.zeros((L, D, 1), w_in.dtype)], axis=2)
    w_pf = jnp.take(w_ext, jnp.asarray(pf_idx), axis=2).astype(BF16)
    w_pb = (jnp.take(w_ext, jnp.asarray(pb_idx), axis=2) * pb_scale).astype(BF16)
    mu_ext = jnp.concatenate([rwkv_mu, jnp.zeros((L, 1), F32)], axis=1)
    mu_pf = jnp.take(mu_ext, jnp.asarray(pf_idx[:PF_RWKV]), axis=1)

    mod_all = _mod_call(c, ada_w, ada_b).reshape(L, B, 6, 1, D)
    bias_c, bias_n = _bias_tables(rel_bias, S, tq_nsa)
    place = _fox_place()
    spread, ones_row = _fox_spread()
    overlap_t = _overlap_t(S)

    x2d = x.reshape(B * S, D)
    for l in range(L):
        sh_m, sc_m, g_m, sh_f, sc_f, g_f = (mod_all[l, :, j] for j in range(6))
        pf = _inproj_call(x2d, sc_m, sh_m, norm_g[l, 0], w_pf[l], F32, S)
        pb = _inproj_call(x2d, sc_m, sh_m, norm_g[l, 0], w_pb[l], BF16, S)
        pf3 = pf.reshape(B, S, PF_COLS)
        pb3 = pb.reshape(B, S, PB_COLS)

        ry, mg, bonus, gate = _rwkv_local_call(
            pf, mu_pf[l], rwkv_w0[l], _lora_rows(rwkv_w_up[l], LORA_W), rwkv_a0[l],
            _lora_rows(rwkv_a_up[l], LORA_A), _lora_rows(rwkv_g_up[l], LORA_G),
            rwkv_k_k[l], rwkv_k_a[l], rwkv_r_k[l], S)
        ya = _rwkv_scan_call(ry, mg, bonus, gate, rwkv_ln_w[l], rwkv_ln_b[l], B, S)

        b_f_row = jnp.zeros((1, LANE), F32).at[0, MISC_F:MISC_F + H_FOX].set(fox_b_f[l])
        qaug, kaug, fox_vt = _fox_prep_call(pf3, pb3, b_f_row, place, spread, ones_row)
        yb = _fox_attn_call(qaug, kaug, fox_vt)

        kc_flat = pb3[:, :, PB_KC:PB_KC + LANE].reshape(B, S // D_CMP, D_CMP * LANE)
        vc_flat = pb3[:, :, PB_VC:PB_VC + LANE].reshape(B, S // D_CMP, D_CMP * LANE)
        kcmp, vcmpT = _nsa_compress_call(
            kc_flat, vc_flat, _compress_w1(nsa_ck_w1[l]), _compress_w1(nsa_cv_w1[l]),
            nsa_ck_w2[l].astype(BF16), nsa_cv_w2[l].T.astype(BF16),
            _compress_pe(nsa_pe_k[l]), _compress_pe(nsa_pe_v[l]))
        ocmp, neg = _nsa_select_call(pb3, kcmp, vcmpT, bias_c, overlap_t, tq_sel)
        yc = _nsa_attn_call(pb3, neg, ocmp, pf3, bias_n, spread, tq_nsa)

        x2d = _outproj_call(x2d, ya.reshape(B * S, C_RWKV), yb.reshape(B * S, C_FOX),
                            yc.reshape(B * S, C_NSA), w_out[l].astype(BF16), g_m, norm_g[l, 1], S)
        x2d = _ffn_call(x2d, sc_f, sh_f, g_f, norm_g[l, 2], norm_g[l, 3], ffn_up[l].astype(BF16),
                        ffn_conv_w[l], ffn_conv_b[l], ffn_down[l].astype(BF16), S)
    return x2d.reshape(B, S, D)
```

```python
import functools
import math

import numpy as np
import jax
import jax.numpy as jnp
from jax import lax
from jax.experimental import pallas as pl
from jax.experimental.pallas import tpu as pltpu

F32 = jnp.float32
BF16 = jnp.bfloat16

HEAD_DIM = 64
H_RWKV = 4
C_RWKV = H_RWKV * HEAD_DIM
H_FOX = 6
C_FOX = H_FOX * HEAD_DIM
H_NSA = 6
C_NSA = H_NSA * HEAD_DIM
G_NSA = 2
HPG = H_NSA // G_NSA
R_DECAY = 32
R_AAA = 32
R_GATE = 64
L_CMP = 32
D_CMP = 16
CMP_HID = 128
L_SLC = 64
N_SLC = 16
N_LOCAL = 2
WINDOW = 512
NUM_BUCKETS = 32
MAX_DISTANCE = 128
CONV_W = 3
RMS_EPS = 1e-6
GN_EPS = 64e-5
NEG_INF = -1e30
FORCE_SCORE = 1e4

LANE = 128
CHUNK = 64
VMEM_LIMIT = 56 * 1024 * 1024

PF_R, PF_K, PF_V, PF_LORA, PF_MISC = 0, 256, 512, 768, 896
LORA_W, LORA_A, LORA_G = 0, 32, 64
PF_RWKV = 896
PF_COLS = 1024
MISC_F = 0
MISC_GATE = 8
PB_FQ, PB_FK, PB_FV, PB_NQ = 0, 384, 768, 1152
PB_KC, PB_VC, PB_KS, PB_VS, PB_KW, PB_VW = 1536, 1664, 1792, 1920, 2048, 2176
PB_COLS = 2304


def _bucket_lower_bounds():
    n = np.arange(0, 4 * MAX_DISTANCE, dtype=np.int64)
    max_exact = NUM_BUCKETS // 2
    nf = np.maximum(n, 1).astype(np.float32)
    large = max_exact + (np.log(nf / np.float32(max_exact)) / np.float32(math.log(MAX_DISTANCE / max_exact))
                         * np.float32(NUM_BUCKETS - max_exact)).astype(np.int32)
    large = np.minimum(large, NUM_BUCKETS - 1)
    bucket = np.where(n < max_exact, n, large)
    return [int(np.argmax(bucket >= b)) for b in range(NUM_BUCKETS)]


BUCKET_LB = _bucket_lower_bounds()
BIAS_CONST_DIST = BUCKET_LB[NUM_BUCKETS - 1]


def _cparams(sem, vmem=None):
    return pltpu.CompilerParams(dimension_semantics=sem, vmem_limit_bytes=vmem or VMEM_LIMIT)


def _dot(a, b):
    return jnp.dot(a, b, preferred_element_type=F32)


def _dot_nt(a, b):
    return lax.dot_general(a, b, (((1,), (1,)), ((), ())), preferred_element_type=F32)


def _split_bf16(x, n):
    parts, r = [], x
    for i in range(n):
        p = r.astype(BF16)
        parts.append(p)
        if i + 1 < n:
            r = r - p.astype(F32)
    return parts


def _dot_hp(a, b, nt=False):
    f = _dot_nt if nt else _dot
    ah, al = _split_bf16(a, 2)
    bh, bl = _split_bf16(b, 2)
    return f(ah, bh) + (f(ah, bl) + f(al, bh))


def _dot_exact_lhs(a_bf16, b, n):
    out = None
    for p in _split_bf16(b, n):
        t = _dot(a_bf16, p)
        out = t if out is None else out + t
    return out


def _softplus(x):
    return jnp.maximum(x, 0.0) + jnp.log(1.0 + jnp.exp(-jnp.abs(x)))


def _sigmoid(x):
    return 1.0 / (1.0 + jnp.exp(-x))


def _rmsnorm(x, g):
    return x * lax.rsqrt(jnp.mean(x * x, axis=-1, keepdims=True) + RMS_EPS) * g


def _mod_kernel(c_ref, w_ref, b_ref, o_ref):
    c = c_ref[...]
    s = (c * _sigmoid(c)).astype(BF16)
    o_ref[0] = _dot(s, w_ref[0].astype(BF16)) + b_ref[0]


def _mod_call(c, ada_w, ada_b):
    L, D, N = ada_w.shape
    B = c.shape[0]
    tn = 1536
    return pl.pallas_call(
        _mod_kernel,
        out_shape=jax.ShapeDtypeStruct((L, B, N), F32),
        grid=(L, N // tn),
        in_specs=[pl.BlockSpec((B, D), lambda l, j: (0, 0)),
                  pl.BlockSpec((1, D, tn), lambda l, j: (l, 0, j)),
                  pl.BlockSpec((1, 1, tn), lambda l, j: (l, 0, j))],
        out_specs=pl.BlockSpec((1, B, tn), lambda l, j: (l, 0, j)),
        compiler_params=_cparams(("arbitrary", "arbitrary")),
        name="adaln_mod",
    )(c, ada_w, ada_b.reshape(L, 1, N))


def _inproj_kernel(x_ref, sc_ref, sh_ref, g_ref, wf_ref, wb_ref, of_ref, ob_ref, *, n_chunk):
    h = _rmsnorm(x_ref[...], g_ref[...]) * (1.0 + sc_ref[0]) + sh_ref[0]
    h = h.astype(BF16)
    for w_ref, o_ref in ((wf_ref, of_ref), (wb_ref, ob_ref)):
        n = o_ref.shape[1]
        for n0 in range(0, n, n_chunk):
            n1 = min(n, n0 + n_chunk)
            o_ref[:, n0:n1] = _dot(h, w_ref[:, n0:n1]).astype(o_ref.dtype)


def _inproj_call(x2d, sc, sh, g, w_f32cols, w_bf16cols, S, tm=512):
    N, D = x2d.shape
    cf, cb = w_f32cols.shape[1], w_bf16cols.shape[1]
    per = S // tm
    return pl.pallas_call(
        functools.partial(_inproj_kernel, n_chunk=512),
        out_shape=(jax.ShapeDtypeStruct((N, cf), F32), jax.ShapeDtypeStruct((N, cb), BF16)),
        grid=(N // tm,),
        in_specs=[pl.BlockSpec((tm, D), lambda i: (i, 0)),
                  pl.BlockSpec((1, 1, D), lambda i: (i // per, 0, 0)),
                  pl.BlockSpec((1, 1, D), lambda i: (i // per, 0, 0)),
                  pl.BlockSpec((1, D), lambda i: (0, 0)),
                  pl.BlockSpec((D, cf), lambda i: (0, 0)),
                  pl.BlockSpec((D, cb), lambda i: (0, 0))],
        out_specs=(pl.BlockSpec((tm, cf), lambda i: (i, 0)), pl.BlockSpec((tm, cb), lambda i: (i, 0))),
        compiler_params=_cparams(("arbitrary",)),
        name="in_proj",
    )(x2d, sc, sh, g.reshape(1, D), w_f32cols, w_bf16cols)


def _rwkv_local_kernel(p_ref, halo_ref, mu_ref, w0_ref, wup_ref, a0_ref, aup_ref, gup_ref,
                       kk_ref, ka_ref, rk_ref,
                       ry_ref, mg_ref, bonus_ref, gate_ref, *, tr, rows_per_seq):
    C = C_RWKV
    i = pl.program_id(0)
    first = (i * tr) % rows_per_seq == 0
    p = p_ref[:, :PF_RWKV]
    row = lax.broadcasted_iota(jnp.int32, (tr, 1), 0)
    prev_last = jnp.where(first, 0.0, halo_ref[7:8, :PF_RWKV])
    prev = jnp.where(row == 0, prev_last, pltpu.roll(p, 1, 0))
    ps = p + (prev - p) * mu_ref[...]
    r = ps[:, PF_R:PF_R + C]
    k = ps[:, PF_K:PF_K + C]
    v = ps[:, PF_V:PF_V + C]
    lora = ps[:, PF_LORA:PF_LORA + LANE]
    wl = w0_ref[...] + _dot(jnp.tanh(lora).astype(BF16), wup_ref[...])
    lw = -jnp.exp(-_softplus(-wl) - 0.5)
    a = _sigmoid(a0_ref[...] + _dot(lora.astype(BF16), aup_ref[...]))
    gate_ref[...] = _dot(_sigmoid(lora).astype(BF16), gup_ref[...])

    li = lax.broadcasted_iota(jnp.int32, (C, C), 0) // HEAD_DIM
    lj = lax.broadcasted_iota(jnp.int32, (C, C), 1) // HEAD_DIM
    same_head = li == lj
    head_ones = jnp.where(same_head, 1.0, 0.0).astype(BF16)

    kk = k * kk_ref[...]
    nrm = jnp.sqrt(_dot_exact_lhs_t(kk * kk, head_ones))
    kk = kk / jnp.maximum(nrm, 1e-12)
    k2 = k * (1.0 + (a - 1.0) * ka_ref[...])
    bonus_ref[...] = _dot_exact_lhs_t(r * k2 * rk_ref[...], head_ones) * v
    avec = -kk
    bvec = kk * a

    nch = tr // CHUNK
    ti = lax.broadcasted_iota(jnp.int32, (tr, tr), 0)
    tj = lax.broadcasted_iota(jnp.int32, (tr, tr), 1)
    same_chunk = ti // CHUNK == tj // CHUNK
    tri_incl = jnp.where(same_chunk & (tj <= ti), 1.0, 0.0).astype(BF16)
    chunk_ones = jnp.where(same_chunk, 1.0, 0.0).astype(BF16)
    Lc = _dot_exact_lhs(tri_incl, lw, 3)
    Lend = _dot_exact_lhs(chunk_ones, lw, 3)
    e_cur = jnp.exp(Lc)
    e_inv = jnp.exp(-Lc)
    e_end = jnp.exp(Lend - Lc)
    At = avec * jnp.exp(Lc - lw)
    Rt = r * e_cur
    Bt = bvec * e_inv
    Kt = k2 * e_inv
    bh_t = (bvec * e_end).T
    kh_t = (k2 * e_end).T
    pc = jnp.exp(Lend)

    pairs = [(c, h) for c in range(nch) for h in range(H_RWKV)]
    blk = lambda x: jnp.stack([x[c * CHUNK:(c + 1) * CHUNK, h * HEAD_DIM:(h + 1) * HEAD_DIM]
                               for c, h in pairs]).astype(BF16)
    blk_t = lambda x: jnp.stack([x[h * HEAD_DIM:(h + 1) * HEAD_DIM, c * CHUNK:(c + 1) * CHUNK]
                                 for c, h in pairs]).astype(BF16)
    bmm = lambda a, b: jnp.einsum('nij,njk->nik', a.astype(BF16), b.astype(BF16), preferred_element_type=F32)
    bmm_nt = lambda a, b: jnp.einsum('nid,nkd->nik', a, b, preferred_element_type=F32)

    ci = lax.broadcasted_iota(jnp.int32, (CHUNK, CHUNK), 0)
    cj = lax.broadcasted_iota(jnp.int32, (CHUNK, CHUNK), 1)
    lower_strict = cj < ci
    lower_incl = cj <= ci
    a_b, r_b, b_b, k_b, v_b = blk(At), blk(Rt), blk(Bt), blk(Kt), blk(v)
    ar = jnp.concatenate([a_b, r_b], axis=1)
    ab = bmm_nt(ar, b_b)
    ak = bmm_nt(ar, k_b)
    n_mat = jnp.where(lower_strict, ab[:, :CHUNK], 0.0)
    a_ak = jnp.where(lower_strict, ak[:, :CHUNK], 0.0)
    a_rb = jnp.where(lower_incl, ab[:, CHUNK:], 0.0)
    a_rk = jnp.where(lower_incl, ak[:, CHUNK:], 0.0)
    av = bmm(jnp.concatenate([a_ak, a_rk], axis=1), v_b)
    x = jnp.concatenate([a_b.astype(F32), av[:, :CHUNK]], axis=2)
    x = x + bmm(n_mat, x)
    for _ in range(5):
        n_mat = bmm(n_mat, n_mat)
        x = x + bmm(n_mat, x)
    corr = bmm(a_rb, x)
    rbar = r_b.astype(F32) + corr[:, :, :HEAD_DIM]
    y0 = av[:, CHUNK:] + corr[:, :, HEAD_DIM:]
    m_mat = bmm(blk_t(bh_t), x)
    g_add = bmm(blk_t(kh_t), v_b)
    eye = ci == cj
    for c in range(nch):
        sl = slice(c * CHUNK, (c + 1) * CHUNK)
        ns = [c * H_RWKV + h for h in range(H_RWKV)]
        ry_ref[sl, :C] = jnp.concatenate([rbar[n] for n in ns], axis=1)
        ry_ref[sl, C:] = jnp.concatenate([y0[n] for n in ns], axis=1)
        for h, n in enumerate(ns):
            hs = slice(h * HEAD_DIM, (h + 1) * HEAD_DIM)
            pc_h = pc[c * CHUNK:c * CHUNK + 1, hs]
            m_h = jnp.where(eye, pc_h, 0.0) + m_mat[n, :, :HEAD_DIM]
            g_h = m_mat[n, :, HEAD_DIM:] + g_add[n]
            mg_ref[c, h] = jnp.concatenate([m_h, g_h], axis=1)


def _dot_exact_lhs_t(x, ones_bf16):
    xh, xl = _split_bf16(x, 2)
    return _dot(xh, ones_bf16) + _dot(xl, ones_bf16)


def _rwkv_local_call(pf, mu, w0, wup, a0, aup, gup, k_k, k_a, r_k, S, tr=512):
    N = pf.shape[0]
    C = C_RWKV
    row = lambda a: a.reshape(1, -1)
    full = lambda a: pl.BlockSpec(a.shape, lambda i: (0,) * a.ndim)
    args = [row(mu), row(w0), wup, row(a0), aup, gup, row(k_k), row(k_a), row(r_k)]
    return pl.pallas_call(
        functools.partial(_rwkv_local_kernel, tr=tr, rows_per_seq=S),
        out_shape=(jax.ShapeDtypeStruct((N, 2 * C), F32),
                   jax.ShapeDtypeStruct((N // CHUNK, H_RWKV, HEAD_DIM, 2 * HEAD_DIM), F32),
                   jax.ShapeDtypeStruct((N, C), F32),
                   jax.ShapeDtypeStruct((N, C), F32)),
        grid=(N // tr,),
        in_specs=[pl.BlockSpec((tr, PF_COLS), lambda i: (i, 0)),
                  pl.BlockSpec((8, PF_COLS), lambda i: (jnp.maximum(i * (tr // 8) - 1, 0), 0))]
                 + [full(a) for a in args],
        out_specs=(pl.BlockSpec((tr, 2 * C), lambda i: (i, 0)),
                   pl.BlockSpec((tr // CHUNK, H_RWKV, HEAD_DIM, 2 * HEAD_DIM), lambda i: (i, 0, 0, 0)),
                   pl.BlockSpec((tr, C), lambda i: (i, 0)),
                   pl.BlockSpec((tr, C), lambda i: (i, 0))),
        compiler_params=_cparams(("arbitrary",)),
        name="rwkv_local",
    )(pf, pf, *args)


def _rwkv_scan_kernel(ry_ref, mg_ref, bonus_ref, gate_ref, lnw_ref, lnb_ref, o_ref, s_scr):
    C = C_RWKV
    B = ry_ref.shape[0]

    @pl.when(pl.program_id(0) == 0)
    def _():
        s_scr[...] = jnp.zeros_like(s_scr)

    li = lax.broadcasted_iota(jnp.int32, (C, C), 0) // HEAD_DIM
    lj = lax.broadcasted_iota(jnp.int32, (C, C), 1) // HEAD_DIM
    head_avg = jnp.where(li == lj, 1.0 / HEAD_DIM, 0.0).astype(BF16)
    bs = range(B)
    zero = jnp.zeros((HEAD_DIM, HEAD_DIM), F32)

    def block_diag(blocks):
        return jnp.concatenate([jnp.concatenate([blk if j == h else zero for j in range(H_RWKV)], axis=1)
                                for h, blk in enumerate(blocks)], axis=0)

    s0 = [s_scr[b].astype(BF16) for b in bs]
    ys = [_dot(ry_ref[b, :, :C].astype(BF16), s0[b]) + ry_ref[b, :, C:] for b in bs]
    for b in bs:
        m_bd = block_diag([mg_ref[b, 0, h, :, :HEAD_DIM] for h in range(H_RWKV)])
        g_bd = block_diag([mg_ref[b, 0, h, :, HEAD_DIM:] for h in range(H_RWKV)])
        s_scr[b] = _dot(m_bd.astype(BF16), s0[b]) + g_bd
    means = [_dot_exact_lhs_t(ys[b], head_avg) for b in bs]
    ds = [ys[b] - means[b] for b in bs]
    vs = [_dot_exact_lhs_t(ds[b] * ds[b], head_avg) for b in bs]
    for b in bs:
        yn = ds[b] * lax.rsqrt(vs[b] + GN_EPS) * lnw_ref[...] + lnb_ref[...]
        o_ref[b] = ((yn + bonus_ref[b]) * gate_ref[b]).astype(o_ref.dtype)


def _rwkv_scan_call(ry, mg, bonus, gate, ln_w, ln_b, B, S):
    C = C_RWKV
    nc = S // CHUNK
    return pl.pallas_call(
        _rwkv_scan_kernel,
        out_shape=jax.ShapeDtypeStruct((B, S, C), BF16),
        grid=(nc,),
        in_specs=[pl.BlockSpec((B, CHUNK, 2 * C), lambda c: (0, c, 0)),
                  pl.BlockSpec((B, 1, H_RWKV, HEAD_DIM, 2 * HEAD_DIM), lambda c: (0, c, 0, 0, 0)),
                  pl.BlockSpec((B, CHUNK, C), lambda c: (0, c, 0)),
                  pl.BlockSpec((B, CHUNK, C), lambda c: (0, c, 0)),
                  pl.BlockSpec((1, C), lambda c: (0, 0)),
                  pl.BlockSpec((1, C), lambda c: (0, 0))],
        out_specs=pl.BlockSpec((B, CHUNK, C), lambda c: (0, c, 0)),
        scratch_shapes=[pltpu.VMEM((B, C, C), F32)],
        compiler_params=_cparams(("arbitrary",)),
        name="rwkv_scan",
    )(ry.reshape(B, S, 2 * C), mg.reshape(B, nc, H_RWKV, HEAD_DIM, 2 * HEAD_DIM), bonus.reshape(B, S, C),
      gate.reshape(B, S, C), ln_w.reshape(1, C), ln_b.reshape(1, C))


def _online_update(carry, s, v):
    m, l, acc = carry
    m_new = jnp.maximum(m, jnp.max(s, axis=-1, keepdims=True))
    alpha = jnp.exp(m - m_new)
    p = jnp.exp(s - m_new)
    l = alpha * l + jnp.sum(p, axis=-1, keepdims=True)
    acc = alpha * acc + _dot(p.astype(BF16), v)
    return m_new, l, acc


def _attn_init(rows):
    return (jnp.full((rows, 1), NEG_INF, F32), jnp.zeros((rows, 1), F32), jnp.zeros((rows, LANE), F32))


def _online_update_t(carry, s_t, v_t):
    m, l, acc = carry
    m_new = jnp.maximum(m, jnp.max(s_t, axis=0, keepdims=True))
    alpha = jnp.exp(m - m_new)
    p = jnp.exp(s_t - m_new)
    l = alpha * l + jnp.sum(p, axis=0, keepdims=True)
    acc = alpha * acc + _dot(v_t, p.astype(BF16))
    return m_new, l, acc


def _attn_init_t(d, cols):
    return (jnp.full((1, cols), NEG_INF, F32), jnp.zeros((1, cols), F32), jnp.zeros((d, cols), F32))


def _online_update_multi(carries, ss, vts):
    n = range(len(ss))
    ms = [jnp.maximum(carries[i][0], jnp.max(ss[i], axis=0, keepdims=True)) for i in n]
    ps = [jnp.exp(ss[i] - ms[i]) for i in n]
    pvs = [_dot(vts[i], ps[i].astype(BF16)) for i in n]
    out = []
    for i in n:
        m, l, acc = carries[i]
        alpha = jnp.exp(m - ms[i])
        out.append((ms[i], alpha * l + jnp.sum(ps[i], axis=0, keepdims=True), alpha * acc + pvs[i]))
    return tuple(out)


def _fox_prep_kernel(misc_ref, q_ref, k_ref, v_ref, bf_ref, place_ref, spread_ref, ones_ref,
                     qa_ref, ka_ref, vt_ref, carry_scr, *, t, tk):
    @pl.when(pl.program_id(1) == 0)
    def _():
        carry_scr[...] = jnp.zeros_like(carry_scr)

    v_t = v_ref[0].astype(F32).T
    for h in range(H_FOX):
        for j in range(t // tk):
            vt_ref[0, h, j] = v_t[h * HEAD_DIM:(h + 1) * HEAD_DIM, j * tk:(j + 1) * tk].astype(BF16)

    lf = -_softplus(-(misc_ref[0] + bf_ref[...]))
    ti = lax.broadcasted_iota(jnp.int32, (t, t), 0)
    tj = lax.broadcasted_iota(jnp.int32, (t, t), 1)
    tri = jnp.where(tj <= ti, 1.0, 0.0).astype(BF16)
    cum = _dot_exact_lhs(tri, lf, 3) + carry_scr[...]
    carry_scr[...] = cum[t - 1:t]
    out = _dot(k_ref[0], spread_ref[...])
    for j, part in enumerate(_split_bf16(-cum, 3)):
        out = out + _dot(part, place_ref[j])
    ka_ref[0] = out.astype(BF16)
    qa_ref[0] = (_dot(q_ref[0], spread_ref[...]) + ones_ref[...]).astype(BF16)


FOX_TK = 512


def _fox_prep_call(pf3, pb3, b_f_row, place, spread, ones_row, t=512):
    B, S, _ = pf3.shape
    W = H_FOX * LANE
    tk = FOX_TK
    col = lambda c: pl.BlockSpec((1, t, C_FOX), lambda b, i: (b, i, c // C_FOX))
    const = lambda a: pl.BlockSpec(a.shape, lambda b, i: (0,) * a.ndim)
    return pl.pallas_call(
        functools.partial(_fox_prep_kernel, t=t, tk=tk),
        out_shape=(jax.ShapeDtypeStruct((B, S, W), BF16),
                   jax.ShapeDtypeStruct((B, S, W), BF16),
                   jax.ShapeDtypeStruct((B, H_FOX, S // tk, HEAD_DIM, tk), BF16)),
        grid=(B, S // t),
        in_specs=[pl.BlockSpec((1, t, LANE), lambda b, i: (b, i, PF_MISC // LANE)),
                  col(PB_FQ), col(PB_FK), col(PB_FV),
                  const(b_f_row), const(place), const(spread), const(ones_row)],
        out_specs=(pl.BlockSpec((1, t, W), lambda b, i: (b, i, 0)),
                   pl.BlockSpec((1, t, W), lambda b, i: (b, i, 0)),
                   pl.BlockSpec((1, H_FOX, t // tk, HEAD_DIM, tk), lambda b, i: (b, 0, i, 0, 0))),
        scratch_shapes=[pltpu.VMEM((1, LANE), F32)],
        compiler_params=_cparams(("arbitrary", "arbitrary")),
        name="fox_prep",
    )(pf3, pb3, pb3, pb3, b_f_row, place, spread, ones_row)


def _fox_attn_kernel(q_ref, k_ref, vt_ref, o_ref, *, tq, tk):
    qi = pl.program_id(1)
    krow = lax.broadcasted_iota(jnp.int32, (tk, tq), 0)
    qcol = lax.broadcasted_iota(jnp.int32, (tk, tq), 1)
    per = tq // tk
    heads = range(H_FOX)
    qs = [q_ref[0, :, h * LANE:(h + 1) * LANE] for h in heads]

    def logits(j, h):
        start = pl.multiple_of(j * tk, tk)
        return _dot_nt(k_ref[0, pl.ds(start, tk), h * LANE:(h + 1) * LANE], qs[h])

    carries = tuple(_attn_init_t(HEAD_DIM, tq) for _ in heads)
    for d in range(per):
        j = qi * per + d
        ss = [jnp.where(krow + d * tk <= qcol, logits(j, h), NEG_INF) for h in heads]
        carries = _online_update_multi(carries, ss, [vt_ref[0, h, j] for h in heads])

    def body(j, carries):
        return _online_update_multi(carries, [logits(j, h) for h in heads], [vt_ref[0, h, j] for h in heads])

    carries = lax.fori_loop(0, qi * per, body, carries)
    outs = [acc / l for (_, l, acc) in carries]
    for hp in range(H_FOX // 2):
        pair = jnp.concatenate([outs[2 * hp], outs[2 * hp + 1]], axis=0)
        o_ref[0, :, hp * LANE:(hp + 1) * LANE] = pair.T.astype(o_ref.dtype)


def _fox_attn_call(qaug, kaug, v_t, tq=512):
    B, S, W = qaug.shape
    tk = FOX_TK
    return pl.pallas_call(
        functools.partial(_fox_attn_kernel, tq=tq, tk=tk),
        out_shape=jax.ShapeDtypeStruct((B, S, C_FOX), BF16),
        grid=(B, S // tq),
        in_specs=[pl.BlockSpec((1, tq, W), lambda b, i: (b, i, 0)),
                  pl.BlockSpec((1, S, W), lambda b, i: (b, 0, 0)),
                  pl.BlockSpec((1, H_FOX, S // tk, HEAD_DIM, tk), lambda b, i: (b, 0, 0, 0, 0))],
        out_specs=pl.BlockSpec((1, tq, C_FOX), lambda b, i: (b, i, 0)),
        compiler_params=_cparams(("arbitrary", "arbitrary")),
        name="fox_attn",
    )(qaug, kaug, v_t)


def _bias_of_dist(n, tab_ref, h):
    val = jnp.zeros(n.shape, F32) + tab_ref[0, h]
    for b in range(1, NUM_BUCKETS):
        val = jnp.where(n >= BUCKET_LB[b], tab_ref[b, h], val)
    return val - tab_ref[NUM_BUCKETS - 1, h]


def _bias_cmp_kernel(tab_ref, o_ref, *, tt):
    t0 = pl.program_id(0) * tt
    nc = o_ref.shape[1]
    c = lax.broadcasted_iota(jnp.int32, (nc, tt), 0)
    t = lax.broadcasted_iota(jnp.int32, (nc, tt), 1) + t0
    n = jnp.maximum(t - (c * D_CMP + L_CMP - 1), 0)
    for h in range(H_NSA):
        o_ref[h] = _bias_of_dist(n, tab_ref, h)


def _bias_near_kernel(tab_ref, o_ref, *, tq):
    j = lax.broadcasted_iota(jnp.int32, (tq, tq), 0)
    i = lax.broadcasted_iota(jnp.int32, (tq, tq), 1)
    for g in range(G_NSA):
        for near in range(2):
            n = jnp.maximum(i - j + near * tq, 0)
            for h in range(HPG):
                o_ref[g, near, :, h * tq:(h + 1) * tq] = _bias_of_dist(n, tab_ref, g * HPG + h)


def _bias_tables(rel_bias, S, tq):
    nc = S // D_CMP
    tt = 512
    smem = pl.BlockSpec(memory_space=pltpu.SMEM)
    bias_c = pl.pallas_call(
        functools.partial(_bias_cmp_kernel, tt=tt),
        out_shape=jax.ShapeDtypeStruct((H_NSA, nc, S), F32),
        grid=(S // tt,),
        in_specs=[smem],
        out_specs=pl.BlockSpec((H_NSA, nc, tt), lambda i: (0, 0, i)),
        compiler_params=_cparams(("arbitrary",)),
        name="nsa_bias_cmp",
    )(rel_bias)
    bias_n = pl.pallas_call(
        functools.partial(_bias_near_kernel, tq=tq),
        out_shape=jax.ShapeDtypeStruct((G_NSA, 2, tq, HPG * tq), F32),
        in_specs=[smem],
        name="nsa_bias_near",
    )(rel_bias)
    return bias_c, bias_n


def _gelu_tanh(x):
    return 0.5 * x * (1.0 + jnp.tanh(math.sqrt(2.0 / math.pi) * (x + 0.044715 * (x * x * x))))


def _nsa_compress_kernel(kc_ref, vc_ref, wk1_ref, wv1_ref, wk2_ref, wv2_ref, pek_ref, pev_ref,
                         kcmp_ref, vcmpT_ref):
    nc = kc_ref.shape[1]
    for g in range(G_NSA):
        for src_ref, w1_ref, w2_ref, pe_ref, is_k in ((kc_ref, wk1_ref, wk2_ref, pek_ref, True),
                                                      (vc_ref, wv1_ref, wv2_ref, pev_ref, False)):
            ch = src_ref[0]
            p1 = _dot(ch, w1_ref[g, 0])
            p2 = _dot(ch, w1_ref[g, 1])
            pec = (_dot(pe_ref[0], w1_ref[g, 0].astype(F32)) + _dot(pe_ref[1], w1_ref[g, 1].astype(F32)))[0:1]
            hid = p1 + pltpu.roll(p2, nc - 1, 0) + pec
            act = _gelu_tanh(hid).astype(BF16)
            if is_k:
                kcmp_ref[0, g] = _dot(act, w2_ref[...]).astype(BF16)
            else:
                vcmpT_ref[0, g] = _dot_nt(w2_ref[...], act).astype(BF16)


def _nsa_compress_call(kc_flat, vc_flat, wk1, wv1, wk2, wv2, pek, pev):
    B, nc, W = kc_flat.shape
    full = lambda a: pl.BlockSpec(a.shape, lambda b: (0,) * a.ndim)
    return pl.pallas_call(
        _nsa_compress_kernel,
        out_shape=(jax.ShapeDtypeStruct((B, G_NSA, nc, HEAD_DIM), BF16),
                   jax.ShapeDtypeStruct((B, G_NSA, HEAD_DIM, nc), BF16)),
        grid=(B,),
        in_specs=[pl.BlockSpec((1, nc, W), lambda b: (b, 0, 0)),
                  pl.BlockSpec((1, nc, W), lambda b: (b, 0, 0)),
                  full(wk1), full(wv1), full(wk2), full(wv2), full(pek), full(pev)],
        out_specs=(pl.BlockSpec((1, G_NSA, nc, HEAD_DIM), lambda b: (b, 0, 0, 0)),
                   pl.BlockSpec((1, G_NSA, HEAD_DIM, nc), lambda b: (b, 0, 0, 0))),
        compiler_params=_cparams(("arbitrary",)),
        name="nsa_compress",
    )(kc_flat, vc_flat, wk1, wv1, wk2, wv2, pek, pev)


def _nsa_select_kernel(q_ref, kcmp_ref, vcmpT_ref, bias_ref, ov_ref, ocmp_ref, neg_ref, *, tq):
    qi = pl.program_id(0)
    nc = kcmp_ref.shape[2]
    nsb = ov_ref.shape[0]
    t = lax.broadcasted_iota(jnp.int32, (1, tq), 1) + qi * tq
    cidx = lax.broadcasted_iota(jnp.int32, (nc, 1), 0)
    valid_c = (cidx * D_CMP + L_CMP - 1) <= t
    jf = lax.broadcasted_iota(jnp.int32, (nsb, tq), 0).astype(F32)
    jb = lax.broadcasted_iota(jnp.int32, (nsb, 1), 0)
    back = t // L_SLC - jb
    valid_b = back >= 0
    forced = (jb == 0) | (valid_b & (back < N_LOCAL))
    heads = range(H_NSA)
    groups = range(G_NSA)
    ss = [jnp.where(valid_c, _dot_nt(kcmp_ref[0, hh // HPG], q_ref[0, :, hh * HEAD_DIM:(hh + 1) * HEAD_DIM])
                    + bias_ref[hh], NEG_INF) for hh in heads]
    ms = [jnp.max(s, axis=0, keepdims=True) for s in ss]
    ps = [jnp.where(valid_c, jnp.exp(ss[hh] - ms[hh]), 0.0) for hh in heads]
    ls = [jnp.sum(p, axis=0, keepdims=True) for p in ps]
    ps = [(ps[hh] / jnp.maximum(ls[hh], 1e-30)).astype(BF16) for hh in heads]
    for hh in heads:
        ocmp_ref[0, hh] = _dot(vcmpT_ref[0, hh // HPG], ps[hh])
    imps = [_dot(ov_ref[...], ps[hh]) for hh in heads]
    group_imp = lambda g: functools.reduce(lambda a, b: a + b, imps[g * HPG:(g + 1) * HPG])
    scores = [jnp.where(valid_b, jnp.where(forced, FORCE_SCORE, group_imp(g)), -1.0) for g in groups]
    sels = [jnp.zeros((nsb, tq), F32) for _ in groups]
    for _ in range(N_SLC):
        mxs = [jnp.max(sc, axis=0, keepdims=True) for sc in scores]
        firsts = [jnp.min(jnp.where(scores[g] == mxs[g], jf, float(nsb)), axis=0, keepdims=True) for g in groups]
        picks = [jf == firsts[g] for g in groups]
        sels = [jnp.where(picks[g] & (mxs[g] >= 0.0), 1.0, sels[g]) for g in groups]
        scores = [jnp.where(picks[g], -2.0, scores[g]) for g in groups]
    for g in groups:
        neg = jnp.where(sels[g] > 0.0, 0.0, NEG_INF)
        pieces = [jnp.zeros((HEAD_DIM, tq), F32), neg]
        if nsb < HEAD_DIM:
            pieces.append(jnp.zeros((HEAD_DIM - nsb, tq), F32))
        neg_ref[0, g] = jnp.concatenate(pieces, axis=0).T.astype(BF16)


def _nsa_select_call(pb3, kcmp, vcmpT, bias_c, overlap_t, tq):
    B, S, _ = pb3.shape
    nc = kcmp.shape[2]
    return pl.pallas_call(
        functools.partial(_nsa_select_kernel, tq=tq),
        out_shape=(jax.ShapeDtypeStruct((B, H_NSA, HEAD_DIM, S), F32),
                   jax.ShapeDtypeStruct((B, G_NSA, S, LANE), BF16)),
        grid=(S // tq, B),
        in_specs=[pl.BlockSpec((1, tq, C_NSA), lambda i, b: (b, i, PB_NQ // C_NSA)),
                  pl.BlockSpec((1, G_NSA, nc, HEAD_DIM), lambda i, b: (b, 0, 0, 0)),
                  pl.BlockSpec((1, G_NSA, HEAD_DIM, nc), lambda i, b: (b, 0, 0, 0)),
                  pl.BlockSpec((H_NSA, nc, tq), lambda i, b: (0, 0, i)),
                  pl.BlockSpec(overlap_t.shape, lambda i, b: (0, 0))],
        out_specs=(pl.BlockSpec((1, H_NSA, HEAD_DIM, tq), lambda i, b: (b, 0, 0, i)),
                   pl.BlockSpec((1, G_NSA, tq, LANE), lambda i, b: (b, 0, i, 0))),
        compiler_params=_cparams(("arbitrary", "arbitrary")),
        name="nsa_select",
    )(pb3, kcmp, vcmpT, bias_c, overlap_t)


def _nsa_attn_kernel(q_ref, neg_ref, ks_ref, vs_ref, kw_ref, vw_ref, ocmp_ref, misc_ref, bias_ref, spread_ref,
                     o_ref, kaug_scr, kwin_scr, vst_scr, vwt_scr, *, tq):
    qi = pl.program_id(1)
    S = ks_ref.shape[1]
    cols = HPG * tq
    n_tiles = S // tq
    assert WINDOW == 2 * tq

    @pl.when(qi == 0)
    def _():
        srow = lax.broadcasted_iota(jnp.int32, (tq, LANE), 0)
        slane = lax.broadcasted_iota(jnp.int32, (tq, LANE), 1)

        def fill(t, carry):
            start = pl.multiple_of(t * tq, tq)
            onehot = jnp.where(slane == HEAD_DIM + (srow + t * tq) // L_SLC, 1.0, 0.0)
            ks = ks_ref[0, pl.ds(start, tq), :].astype(F32)
            kw = kw_ref[0, pl.ds(start, tq), :].astype(F32)
            vs_t = vs_ref[0, pl.ds(start, tq), :].astype(F32).T
            vw_t = vw_ref[0, pl.ds(start, tq), :].astype(F32).T
            for g in range(G_NSA):
                ks_g = ks if g == 0 else pltpu.roll(ks, LANE - g * HEAD_DIM, 1)
                kw_g = kw if g == 0 else pltpu.roll(kw, LANE - g * HEAD_DIM, 1)
                kaug_scr[g, pl.ds(start, tq), :] = (jnp.where(slane < HEAD_DIM, ks_g, 0.0) + onehot).astype(BF16)
                kwin_scr[g, pl.ds(start, tq), :] = jnp.where(slane < HEAD_DIM, kw_g, 0.0).astype(BF16)
                vst_scr[g, t] = vs_t[g * HEAD_DIM:(g + 1) * HEAD_DIM].astype(BF16)
                vwt_scr[g, t] = vw_t[g * HEAD_DIM:(g + 1) * HEAD_DIM].astype(BF16)
            return carry

        lax.fori_loop(0, n_tiles, fill, 0)

    jk = lax.broadcasted_iota(jnp.int32, (tq, cols), 0)
    ic = lax.broadcasted_iota(jnp.int32, (tq, cols), 1) % tq
    causal = jk <= ic
    in_window = jk > ic
    prev = jnp.maximum(qi - 1, 0)
    prev2 = jnp.maximum(qi - 2, 0)
    has_prev = qi >= 1
    has_prev2 = qi >= 2
    groups = range(G_NSA)
    q_wide = _dot(q_ref[0], spread_ref[...])
    qa = [jnp.concatenate([(q_wide[:, (g * HPG + h) * LANE:(g * HPG + h + 1) * LANE]
                            + neg_ref[0, g].astype(F32)).astype(BF16)
                           for h in range(HPG)], axis=0) for g in groups]

    def sel_logits(j, g):
        start = pl.multiple_of(j * tq, tq)
        return _dot_nt(kaug_scr[g, pl.ds(start, tq), :], qa[g])

    def win_logits(j, g):
        start = pl.multiple_of(j * tq, tq)
        return _dot_nt(kwin_scr[g, pl.ds(start, tq), :], qa[g])

    carries = tuple(_attn_init_t(HEAD_DIM, cols) for _ in range(2 * G_NSA))
    ss = ([jnp.where(causal, sel_logits(qi, g) + bias_ref[g, 0], NEG_INF) for g in groups]
          + [jnp.where(causal, win_logits(qi, g) + bias_ref[g, 0], NEG_INF) for g in groups])
    carries = _online_update_multi(carries, ss, [vst_scr[g, qi] for g in groups] + [vwt_scr[g, qi] for g in groups])
    ss = ([jnp.where(has_prev, sel_logits(prev, g) + bias_ref[g, 1], NEG_INF) for g in groups]
          + [jnp.where(has_prev, win_logits(prev, g) + bias_ref[g, 1], NEG_INF) for g in groups])
    carries = _online_update_multi(carries, ss,
                                   [vst_scr[g, prev] for g in groups] + [vwt_scr[g, prev] for g in groups])
    ss = [jnp.where(has_prev2 & in_window, win_logits(prev2, g), NEG_INF) for g in groups]
    win = _online_update_multi(carries[G_NSA:], ss, [vwt_scr[g, prev2] for g in groups])

    def body(j, c):
        return _online_update_multi(c, [sel_logits(j, g) for g in groups], [vst_scr[g, j] for g in groups])

    sel = lax.fori_loop(0, prev, body, carries[:G_NSA])

    gates = _sigmoid(misc_ref[0].T)
    outs = []
    for g in groups:
        o_slc = sel[g][2] / sel[g][1]
        o_win = win[g][2] / win[g][1]
        for h in range(HPG):
            hh = g * HPG + h
            base = MISC_GATE + hh * 3
            cs = slice(h * tq, (h + 1) * tq)
            outs.append(gates[base:base + 1] * ocmp_ref[0, hh] + gates[base + 1:base + 2] * o_slc[:, cs]
                        + gates[base + 2:base + 3] * o_win[:, cs])
    for pair in range(H_NSA // 2):
        both = jnp.concatenate([outs[2 * pair], outs[2 * pair + 1]], axis=0)
        o_ref[0, :, pair * LANE:(pair + 1) * LANE] = both.T.astype(o_ref.dtype)


def _nsa_attn_call(pb3, neg, ocmp, pf3, bias_n, spread, tq):
    B, S, _ = pb3.shape
    kv = lambda col: pl.BlockSpec((1, S, LANE), lambda b, i: (b, 0, col // LANE))
    return pl.pallas_call(
        functools.partial(_nsa_attn_kernel, tq=tq),
        out_shape=jax.ShapeDtypeStruct((B, S, C_NSA), BF16),
        grid=(B, S // tq),
        in_specs=[pl.BlockSpec((1, tq, C_NSA), lambda b, i: (b, i, PB_NQ // C_NSA)),
                  pl.BlockSpec((1, G_NSA, tq, LANE), lambda b, i: (b, 0, i, 0)),
                  kv(PB_KS), kv(PB_VS), kv(PB_KW), kv(PB_VW),
                  pl.BlockSpec((1, H_NSA, HEAD_DIM, tq), lambda b, i: (b, 0, 0, i)),
                  pl.BlockSpec((1, tq, LANE), lambda b, i: (b, i, PF_MISC // LANE)),
                  pl.BlockSpec(bias_n.shape, lambda b, i: (0, 0, 0, 0)),
                  pl.BlockSpec(spread.shape, lambda b, i: (0, 0))],
        out_specs=pl.BlockSpec((1, tq, C_NSA), lambda b, i: (b, i, 0)),
        scratch_shapes=[pltpu.VMEM((G_NSA, S, LANE), BF16),
                        pltpu.VMEM((G_NSA, S, LANE), BF16),
                        pltpu.VMEM((G_NSA, S // tq, HEAD_DIM, tq), BF16),
                        pltpu.VMEM((G_NSA, S // tq, HEAD_DIM, tq), BF16)],
        compiler_params=_cparams(("arbitrary", "arbitrary")),
        name="nsa_attn",
    )(pb3, neg, pb3, pb3, pb3, pb3, ocmp, pf3, bias_n, spread)


def _outproj_kernel(x_ref, ya_ref, yb_ref, yc_ref, w_ref, gm_ref, g_ref, o_ref):
    y_cat = jnp.concatenate([ya_ref[...], yb_ref[...], yc_ref[...]], axis=1)
    o_ref[...] = x_ref[...] + gm_ref[0] * _rmsnorm(_dot(y_cat, w_ref[...]), g_ref[...])


def _outproj_call(x2d, ya, yb, yc, w, gm, g, S, tm=512):
    N, D = x2d.shape
    per = S // tm
    rows = lambda a: pl.BlockSpec((tm, a.shape[1]), lambda i: (i, 0))
    return pl.pallas_call(
        _outproj_kernel,
        out_shape=jax.ShapeDtypeStruct((N, D), F32),
        grid=(N // tm,),
        in_specs=[rows(x2d), rows(ya), rows(yb), rows(yc),
                  pl.BlockSpec(w.shape, lambda i: (0, 0)),
                  pl.BlockSpec((1, 1, D), lambda i: (i // per, 0, 0)),
                  pl.BlockSpec((1, D), lambda i: (0, 0))],
        out_specs=pl.BlockSpec((tm, D), lambda i: (i, 0)),
        compiler_params=_cparams(("arbitrary",)),
        name="out_proj",
    )(x2d, ya, yb, yc, w, gm, g.reshape(1, D))


def _ffn_kernel(x_ref, halo_ref, sc_ref, sh_ref, gf_ref, g2_ref, g3_ref, wg_ref, wv_ref,
                cwg_ref, cwv_ref, cbg_ref, cbv_ref, wd_ref, o_ref, h_scr, acc_scr, *, tm, rows_per_seq):
    i = pl.program_id(0)
    f = pl.program_id(1)

    @pl.when(f == 0)
    def _():
        xe = jnp.concatenate([halo_ref[...], x_ref[...]], axis=0)
        h = _rmsnorm(xe, g2_ref[...]) * (1.0 + sc_ref[0]) + sh_ref[0]
        row = lax.broadcasted_iota(jnp.int32, (tm + 8, 1), 0)
        first = (i * tm) % rows_per_seq == 0
        h_scr[...] = jnp.where((row < 8) & first, 0.0, h).astype(BF16)
        acc_scr[...] = jnp.zeros_like(acc_scr)

    h = h_scr[...]

    def conv(w_ref, cw_ref, cb_ref):
        u = _dot(h, w_ref[...])
        y = (cw_ref[2:3] * u + cw_ref[1:2] * pltpu.roll(u, 1, 0) + cw_ref[0:1] * pltpu.roll(u, 2, 0)
             + cb_ref[...])
        return y[8:]

    gate = conv(wg_ref, cwg_ref, cbg_ref)
    val = conv(wv_ref, cwv_ref, cbv_ref)
    act = (gate * _sigmoid(gate) * val).astype(BF16)
    acc_scr[...] += _dot(act, wd_ref[...])

    @pl.when(f == pl.num_programs(1) - 1)
    def _():
        o_ref[...] = x_ref[...] + gf_ref[0] * _rmsnorm(acc_scr[...], g3_ref[...])


def _ffn_call(x2d, sc, sh, gf, g2, g3, w_up, conv_w, conv_b, w_down, S, tm=1024, tf=1408):
    N, D = x2d.shape
    F = w_down.shape[0]
    nf = F // tf
    per = S // tm
    mod = pl.BlockSpec((1, 1, D), lambda i, f: (i // per, 0, 0))
    vec = pl.BlockSpec((1, D), lambda i, f: (0, 0))
    cb = conv_b.reshape(1, 2 * F)
    return pl.pallas_call(
        functools.partial(_ffn_kernel, tm=tm, rows_per_seq=S),
        out_shape=jax.ShapeDtypeStruct((N, D), F32),
        grid=(N // tm, nf),
        in_specs=[pl.BlockSpec((tm, D), lambda i, f: (i, 0)),
                  pl.BlockSpec((8, D), lambda i, f: (jnp.maximum(i * (tm // 8) - 1, 0), 0)),
                  mod, mod, mod, vec, vec,
                  pl.BlockSpec((D, tf), lambda i, f: (0, f)),
                  pl.BlockSpec((D, tf), lambda i, f: (0, nf + f)),
                  pl.BlockSpec((CONV_W, tf), lambda i, f: (0, f)),
                  pl.BlockSpec((CONV_W, tf), lambda i, f: (0, nf + f)),
                  pl.BlockSpec((1, tf), lambda i, f: (0, f)),
                  pl.BlockSpec((1, tf), lambda i, f: (0, nf + f)),
                  pl.BlockSpec((tf, D), lambda i, f: (f, 0))],
        out_specs=pl.BlockSpec((tm, D), lambda i, f: (i, 0)),
        scratch_shapes=[pltpu.VMEM((tm + 8, D), BF16), pltpu.VMEM((tm, D), F32)],
        compiler_params=_cparams(("arbitrary", "arbitrary")),
        name="conv_ffn",
    )(x2d, x2d, sc, sh, gf, g2.reshape(1, D), g3.reshape(1, D), w_up, w_up, conv_w, conv_w, cb, cb, w_down)


def _column_maps():
    n_rwkv = 3 * C_RWKV + R_DECAY + R_AAA + R_GATE
    n_fox = 3 * C_FOX + H_FOX
    fox0 = n_rwkv
    nsa0 = n_rwkv + n_fox
    pf = np.full(PF_COLS, -1, np.int64)
    pf[PF_R:PF_R + 3 * C_RWKV] = np.arange(3 * C_RWKV)
    assert (LORA_A, LORA_G, LANE) == (R_DECAY, R_DECAY + R_AAA, R_DECAY + R_AAA + R_GATE)
    pf[PF_LORA:PF_LORA + LANE] = 3 * C_RWKV + np.arange(LANE)
    pf[PF_MISC + MISC_F:PF_MISC + MISC_F + H_FOX] = fox0 + 3 * C_FOX + np.arange(H_FOX)
    nsa_gate0 = nsa0 + C_NSA + 6 * G_NSA * HEAD_DIM
    pf[PF_MISC + MISC_GATE:PF_MISC + MISC_GATE + 3 * H_NSA] = nsa_gate0 + np.arange(3 * H_NSA)

    pb = np.full(PB_COLS, -1, np.int64)
    scale = np.ones(PB_COLS, np.float32)
    d = np.arange(HEAD_DIM)
    del d
    pb[PB_FQ:PB_FQ + C_FOX] = fox0 + np.arange(C_FOX)
    scale[PB_FQ:PB_FQ + C_FOX] = HEAD_DIM ** -0.5
    pb[PB_FK:PB_FK + C_FOX] = fox0 + C_FOX + np.arange(C_FOX)
    pb[PB_FV:PB_FV + C_FOX] = fox0 + 2 * C_FOX + np.arange(C_FOX)
    pb[PB_NQ:PB_NQ + C_NSA] = nsa0 + np.arange(C_NSA)
    scale[PB_NQ:PB_NQ + C_NSA] = HEAD_DIM ** -0.5
    ckv = G_NSA * HEAD_DIM
    kc0 = nsa0 + C_NSA
    for n, base in enumerate((PB_KC, PB_VC, PB_KS, PB_VS, PB_KW, PB_VW)):
        pb[base:base + ckv] = kc0 + n * ckv + np.arange(ckv)
    return pf, pb, scale


def _lora_rows(w, lane0):
    out = jnp.zeros((LANE, w.shape[1]), w.dtype)
    return out.at[lane0:lane0 + w.shape[0]].set(w).astype(BF16)


def _pad_rows(w, rows):
    return jnp.concatenate([w, jnp.zeros((rows - w.shape[0],) + w.shape[1:], w.dtype)], axis=0)


def _compress_w1(w1):
    hid = w1.shape[1]
    w = w1.reshape(2, D_CMP, HEAD_DIM, hid)
    out = jnp.zeros((G_NSA, 2, D_CMP, G_NSA, HEAD_DIM, hid), w1.dtype)
    for g in range(G_NSA):
        out = out.at[g, :, :, g].set(w)
    return out.reshape(G_NSA, 2, D_CMP * G_NSA * HEAD_DIM, hid).astype(BF16)


def _compress_pe(pe):
    half = pe.reshape(2, 1, D_CMP, 1, HEAD_DIM)
    return jnp.broadcast_to(half, (2, 8, D_CMP, G_NSA, HEAD_DIM)).reshape(2, 8, D_CMP * G_NSA * HEAD_DIM)


def _fox_place():
    place = np.zeros((3, LANE, H_FOX * LANE), np.float32)
    for j in range(3):
        for h in range(H_FOX):
            place[j, MISC_F + h, h * LANE + HEAD_DIM + j] = 1.0
    return jnp.asarray(place, BF16)


def _fox_spread():
    spread = np.zeros((C_FOX, H_FOX * LANE), np.float32)
    ones_row = np.zeros((1, H_FOX * LANE), np.float32)
    d = np.arange(HEAD_DIM)
    for h in range(H_FOX):
        spread[h * HEAD_DIM + d, h * LANE + d] = 1.0
        ones_row[0, h * LANE + HEAD_DIM:h * LANE + HEAD_DIM + 3] = 1.0
    return jnp.asarray(spread, BF16), jnp.asarray(ones_row, F32)


def _overlap_t(S):
    nc = S // D_CMP
    nsb = S // L_SLC
    c0 = np.arange(nc) * D_CMP
    c1 = c0 + L_CMP - 1
    s0 = np.arange(nsb) * L_SLC
    ov = (c0[None, :] <= s0[:, None] + L_SLC - 1) & (c1[None, :] >= s0[:, None])
    ov[:, nc - 1] = False
    return jnp.asarray(ov.astype(np.float32), BF16)


def kernel(x, c, ada_w, ada_b, norm_g, w_in, rwkv_mu, rwkv_w0, rwkv_w_up, rwkv_a0, rwkv_a_up, rwkv_g_up, rwkv_k_k, rwkv_k_a, rwkv_r_k, rwkv_ln_w, rwkv_ln_b, fox_b_f, nsa_pe_k, nsa_pe_v, nsa_ck_w1, nsa_ck_w2, nsa_cv_w1, nsa_cv_w2, rel_bias, w_out, ffn_up, ffn_conv_w, ffn_conv_b, ffn_down):
    B, S, D = x.shape
    L = w_in.shape[0]
    tq_sel = 256
    tq_nsa = WINDOW // 2
    assert S % 1024 == 0 and S // L_SLC <= HEAD_DIM and D == 1024

    pf_idx, pb_idx, pb_scale = _column_maps()
    w_ext = jnp.concatenate([w_in, jnp.zeros((L, D, 1), w_in.dtype)], axis=2)
    w_pf = jnp.take(w_ext, jnp.asarray(pf_idx), axis=2).astype(BF16)
    w_pb = (jnp.take(w_ext, jnp.asarray(pb_idx), axis=2) * pb_scale).astype(BF16)
    mu_ext = jnp.concatenate([rwkv_mu, jnp.zeros((L, 1), F32)], axis=1)
    mu_pf = jnp.take(mu_ext, jnp.asarray(pf_idx[:PF_RWKV]), axis=1)

    mod_all = _mod_call(c, ada_w, ada_b).reshape(L, B, 6, 1, D)
    bias_c, bias_n = _bias_tables(rel_bias, S, tq_nsa)
    place = _fox_place()
    spread, ones_row = _fox_spread()
    overlap_t = _overlap_t(S)

    x2d = x.reshape(B * S, D)
    for l in range(L):
        sh_m, sc_m, g_m, sh_f, sc_f, g_f = (mod_all[l, :, j] for j in range(6))
        pf, pb = _inproj_call(x2d, sc_m, sh_m, norm_g[l, 0], w_pf[l], w_pb[l], S)
        pf3 = pf.reshape(B, S, PF_COLS)
        pb3 = pb.reshape(B, S, PB_COLS)

        ry, mg, bonus, gate = _rwkv_local_call(
            pf, mu_pf[l], rwkv_w0[l], _lora_rows(rwkv_w_up[l], LORA_W), rwkv_a0[l],
            _lora_rows(rwkv_a_up[l], LORA_A), _lora_rows(rwkv_g_up[l], LORA_G),
            rwkv_k_k[l], rwkv_k_a[l], rwkv_r_k[l], S)
        ya = _rwkv_scan_call(ry, mg, bonus, gate, rwkv_ln_w[l], rwkv_ln_b[l], B, S)

        b_f_row = jnp.zeros((1, LANE), F32).at[0, MISC_F:MISC_F + H_FOX].set(fox_b_f[l])
        qaug, kaug, fox_vt = _fox_prep_call(pf3, pb3, b_f_row, place, spread, ones_row)
        yb = _fox_attn_call(qaug, kaug, fox_vt)

        kc_flat = pb3[:, :, PB_KC:PB_KC + LANE].reshape(B, S // D_CMP, D_CMP * LANE)
        vc_flat = pb3[:, :, PB_VC:PB_VC + LANE].reshape(B, S // D_CMP, D_CMP * LANE)
        kcmp, vcmpT = _nsa_compress_call(
            kc_flat, vc_flat, _compress_w1(nsa_ck_w1[l]), _compress_w1(nsa_cv_w1[l]),
            nsa_ck_w2[l].astype(BF16), nsa_cv_w2[l].T.astype(BF16),
            _compress_pe(nsa_pe_k[l]), _compress_pe(nsa_pe_v[l]))
        ocmp, neg = _nsa_select_call(pb3, kcmp, vcmpT, bias_c, overlap_t, tq_sel)
        yc = _nsa_attn_call(pb3, neg, ocmp, pf3, bias_n, spread, tq_nsa)

        x2d = _outproj_call(x2d, ya.reshape(B * S, C_RWKV), yb.reshape(B * S, C_FOX),
                            yc.reshape(B * S, C_NSA), w_out[l].astype(BF16), g_m, norm_g[l, 1], S)
        x2d = _ffn_call(x2d, sc_f, sh_f, g_f, norm_g[l, 2], norm_g[l, 3], ffn_up[l].astype(BF16),
                        ffn_conv_w[l], ffn_conv_b[l], ffn_down[l].astype(BF16), S)
    return x2d.reshape(B, S, D)
```

```python
import functools
import math

import numpy as np
import jax
import jax.numpy as jnp
from jax import lax
from jax.experimental import pallas as pl
from jax.experimental.pallas import tpu as pltpu

F32 = jnp.float32
BF16 = jnp.bfloat16

HEAD_DIM = 64
H_RWKV = 4
C_RWKV = H_RWKV * HEAD_DIM
H_FOX = 6
C_FOX = H_FOX * HEAD_DIM
H_NSA = 6
C_NSA = H_NSA * HEAD_DIM
G_NSA = 2
HPG = H_NSA // G_NSA
R_DECAY = 32
R_AAA = 32
R_GATE = 64
L_CMP = 32
D_CMP = 16
CMP_HID = 128
L_SLC = 64
N_SLC = 16
N_LOCAL = 2
WINDOW = 512
NUM_BUCKETS = 32
MAX_DISTANCE = 128
CONV_W = 3
RMS_EPS = 1e-6
GN_EPS = 64e-5
NEG_INF = -1e30
FORCE_SCORE = 1e4

LANE = 128
CHUNK = 64
VMEM_LIMIT = 56 * 1024 * 1024

PF_R, PF_K, PF_V, PF_LORA, PF_MISC = 0, 256, 512, 768, 896
LORA_W, LORA_A, LORA_G = 0, 32, 64
PF_RWKV = 896
PF_COLS = 1024
MISC_F = 0
MISC_GATE = 8
PB_FQ, PB_FK, PB_FV, PB_NQ = 0, 384, 768, 1152
PB_KC, PB_VC, PB_KS, PB_VS, PB_KW, PB_VW = 1536, 1664, 1792, 1920, 2048, 2176
PB_COLS = 2304


def _bucket_lower_bounds():
    n = np.arange(0, 4 * MAX_DISTANCE, dtype=np.int64)
    max_exact = NUM_BUCKETS // 2
    nf = np.maximum(n, 1).astype(np.float32)
    large = max_exact + (np.log(nf / np.float32(max_exact)) / np.float32(math.log(MAX_DISTANCE / max_exact))
                         * np.float32(NUM_BUCKETS - max_exact)).astype(np.int32)
    large = np.minimum(large, NUM_BUCKETS - 1)
    bucket = np.where(n < max_exact, n, large)
    return [int(np.argmax(bucket >= b)) for b in range(NUM_BUCKETS)]


BUCKET_LB = _bucket_lower_bounds()
BIAS_CONST_DIST = BUCKET_LB[NUM_BUCKETS - 1]


def _cparams(sem, vmem=None):
    return pltpu.CompilerParams(dimension_semantics=sem, vmem_limit_bytes=vmem or VMEM_LIMIT)


def _dot(a, b):
    return jnp.dot(a, b, preferred_element_type=F32)


def _dot_nt(a, b):
    return lax.dot_general(a, b, (((1,), (1,)), ((), ())), preferred_element_type=F32)


def _split_bf16(x, n):
    parts, r = [], x
    for i in range(n):
        p = r.astype(BF16)
        parts.append(p)
        if i + 1 < n:
            r = r - p.astype(F32)
    return parts


def _dot_hp(a, b, nt=False):
    f = _dot_nt if nt else _dot
    ah, al = _split_bf16(a, 2)
    bh, bl = _split_bf16(b, 2)
    return f(ah, bh) + (f(ah, bl) + f(al, bh))


def _dot_exact_lhs(a_bf16, b, n):
    out = None
    for p in _split_bf16(b, n):
        t = _dot(a_bf16, p)
        out = t if out is None else out + t
    return out


def _softplus(x):
    return jnp.maximum(x, 0.0) + jnp.log(1.0 + jnp.exp(-jnp.abs(x)))


def _sigmoid(x):
    return 1.0 / (1.0 + jnp.exp(-x))


def _rmsnorm(x, g):
    return x * lax.rsqrt(jnp.mean(x * x, axis=-1, keepdims=True) + RMS_EPS) * g


def _mod_kernel(c_ref, w_ref, b_ref, o_ref):
    c = c_ref[...]
    s = (c * _sigmoid(c)).astype(BF16)
    o_ref[0] = _dot(s, w_ref[0].astype(BF16)) + b_ref[0]


def _mod_call(c, ada_w, ada_b):
    L, D, N = ada_w.shape
    B = c.shape[0]
    tn = 1536
    return pl.pallas_call(
        _mod_kernel,
        out_shape=jax.ShapeDtypeStruct((L, B, N), F32),
        grid=(L, N // tn),
        in_specs=[pl.BlockSpec((B, D), lambda l, j: (0, 0)),
                  pl.BlockSpec((1, D, tn), lambda l, j: (l, 0, j)),
                  pl.BlockSpec((1, 1, tn), lambda l, j: (l, 0, j))],
        out_specs=pl.BlockSpec((1, B, tn), lambda l, j: (l, 0, j)),
        compiler_params=_cparams(("arbitrary", "arbitrary")),
        name="adaln_mod",
    )(c, ada_w, ada_b.reshape(L, 1, N))


def _inproj_kernel(x_ref, sc_ref, sh_ref, g_ref, wf_ref, wb_ref, of_ref, ob_ref, *, n_chunk):
    h = _rmsnorm(x_ref[...], g_ref[...]) * (1.0 + sc_ref[0]) + sh_ref[0]
    h = h.astype(BF16)
    for w_ref, o_ref in ((wf_ref, of_ref), (wb_ref, ob_ref)):
        n = o_ref.shape[1]
        for n0 in range(0, n, n_chunk):
            n1 = min(n, n0 + n_chunk)
            o_ref[:, n0:n1] = _dot(h, w_ref[:, n0:n1]).astype(o_ref.dtype)


def _inproj_call(x2d, sc, sh, g, w_f32cols, w_bf16cols, S, tm=512):
    N, D = x2d.shape
    cf, cb = w_f32cols.shape[1], w_bf16cols.shape[1]
    per = S // tm
    return pl.pallas_call(
        functools.partial(_inproj_kernel, n_chunk=512),
        out_shape=(jax.ShapeDtypeStruct((N, cf), F32), jax.ShapeDtypeStruct((N, cb), BF16)),
        grid=(N // tm,),
        in_specs=[pl.BlockSpec((tm, D), lambda i: (i, 0)),
                  pl.BlockSpec((1, 1, D), lambda i: (i // per, 0, 0)),
                  pl.BlockSpec((1, 1, D), lambda i: (i // per, 0, 0)),
                  pl.BlockSpec((1, D), lambda i: (0, 0)),
                  pl.BlockSpec((D, cf), lambda i: (0, 0)),
                  pl.BlockSpec((D, cb), lambda i: (0, 0))],
        out_specs=(pl.BlockSpec((tm, cf), lambda i: (i, 0)), pl.BlockSpec((tm, cb), lambda i: (i, 0))),
        compiler_params=_cparams(("arbitrary",)),
        name="in_proj",
    )(x2d, sc, sh, g.reshape(1, D), w_f32cols, w_bf16cols)


def _rwkv_local_kernel(p_ref, halo_ref, mu_ref, w0_ref, wup_ref, a0_ref, aup_ref, gup_ref,
                       kk_ref, ka_ref, rk_ref,
                       ry_ref, mg_ref, bonus_ref, gate_ref, *, tr, rows_per_seq):
    C = C_RWKV
    i = pl.program_id(0)
    first = (i * tr) % rows_per_seq == 0
    p = p_ref[:, :PF_RWKV]
    row = lax.broadcasted_iota(jnp.int32, (tr, 1), 0)
    prev_last = jnp.where(first, 0.0, halo_ref[7:8, :PF_RWKV])
    prev = jnp.where(row == 0, prev_last, pltpu.roll(p, 1, 0))
    ps = p + (prev - p) * mu_ref[...]
    r = ps[:, PF_R:PF_R + C]
    k = ps[:, PF_K:PF_K + C]
    v = ps[:, PF_V:PF_V + C]
    lora = ps[:, PF_LORA:PF_LORA + LANE]
    wl = w0_ref[...] + _dot(jnp.tanh(lora).astype(BF16), wup_ref[...])
    lw = -jnp.exp(-_softplus(-wl) - 0.5)
    a = _sigmoid(a0_ref[...] + _dot(lora.astype(BF16), aup_ref[...]))
    gate_ref[...] = _dot(_sigmoid(lora).astype(BF16), gup_ref[...])

    li = lax.broadcasted_iota(jnp.int32, (C, C), 0) // HEAD_DIM
    lj = lax.broadcasted_iota(jnp.int32, (C, C), 1) // HEAD_DIM
    same_head = li == lj
    head_ones = jnp.where(same_head, 1.0, 0.0).astype(BF16)

    kk = k * kk_ref[...]
    nrm = jnp.sqrt(_dot_exact_lhs_t(kk * kk, head_ones))
    kk = kk / jnp.maximum(nrm, 1e-12)
    k2 = k * (1.0 + (a - 1.0) * ka_ref[...])
    bonus_ref[...] = _dot_exact_lhs_t(r * k2 * rk_ref[...], head_ones) * v
    avec = -kk
    bvec = kk * a

    nch = tr // CHUNK
    ti = lax.broadcasted_iota(jnp.int32, (tr, tr), 0)
    tj = lax.broadcasted_iota(jnp.int32, (tr, tr), 1)
    same_chunk = ti // CHUNK == tj // CHUNK
    tri_incl = jnp.where(same_chunk & (tj <= ti), 1.0, 0.0).astype(BF16)
    chunk_ones = jnp.where(same_chunk, 1.0, 0.0).astype(BF16)
    Lc = _dot_exact_lhs(tri_incl, lw, 3)
    Lend = _dot_exact_lhs(chunk_ones, lw, 3)
    e_cur = jnp.exp(Lc)
    e_inv = jnp.exp(-Lc)
    e_end = jnp.exp(Lend - Lc)
    At = avec * jnp.exp(Lc - lw)
    Rt = r * e_cur
    Bt = bvec * e_inv
    Kt = k2 * e_inv
    bh_t = (bvec * e_end).T
    kh_t = (k2 * e_end).T
    pc = jnp.exp(Lend)

    pairs = [(c, h) for c in range(nch) for h in range(H_RWKV)]
    blk = lambda x: jnp.stack([x[c * CHUNK:(c + 1) * CHUNK, h * HEAD_DIM:(h + 1) * HEAD_DIM]
                               for c, h in pairs]).astype(BF16)
    blk_t = lambda x: jnp.stack([x[h * HEAD_DIM:(h + 1) * HEAD_DIM, c * CHUNK:(c + 1) * CHUNK]
                                 for c, h in pairs]).astype(BF16)
    bmm = lambda a, b: jnp.einsum('nij,njk->nik', a.astype(BF16), b.astype(BF16), preferred_element_type=F32)
    bmm_nt = lambda a, b: jnp.einsum('nid,nkd->nik', a, b, preferred_element_type=F32)

    ci = lax.broadcasted_iota(jnp.int32, (CHUNK, CHUNK), 0)
    cj = lax.broadcasted_iota(jnp.int32, (CHUNK, CHUNK), 1)
    lower_strict = cj < ci
    lower_incl = cj <= ci
    a_b, r_b, b_b, k_b, v_b = blk(At), blk(Rt), blk(Bt), blk(Kt), blk(v)
    ar = jnp.concatenate([a_b, r_b], axis=1)
    ab = bmm_nt(ar, b_b)
    ak = bmm_nt(ar, k_b)
    n_mat = jnp.where(lower_strict, ab[:, :CHUNK], 0.0)
    a_ak = jnp.where(lower_strict, ak[:, :CHUNK], 0.0)
    a_rb = jnp.where(lower_incl, ab[:, CHUNK:], 0.0)
    a_rk = jnp.where(lower_incl, ak[:, CHUNK:], 0.0)
    av = bmm(jnp.concatenate([a_ak, a_rk], axis=1), v_b)
    x = jnp.concatenate([a_b.astype(F32), av[:, :CHUNK]], axis=2)
    x = x + bmm(n_mat, x)
    for _ in range(5):
        n_mat = bmm(n_mat, n_mat)
        x = x + bmm(n_mat, x)
    corr = bmm(a_rb, x)
    rbar = r_b.astype(F32) + corr[:, :, :HEAD_DIM]
    y0 = av[:, CHUNK:] + corr[:, :, HEAD_DIM:]
    m_mat = bmm(blk_t(bh_t), x)
    g_add = bmm(blk_t(kh_t), v_b)
    eye = ci == cj
    for c in range(nch):
        sl = slice(c * CHUNK, (c + 1) * CHUNK)
        ns = [c * H_RWKV + h for h in range(H_RWKV)]
        ry_ref[sl, :C] = jnp.concatenate([rbar[n] for n in ns], axis=1)
        ry_ref[sl, C:] = jnp.concatenate([y0[n] for n in ns], axis=1)
        for h, n in enumerate(ns):
            hs = slice(h * HEAD_DIM, (h + 1) * HEAD_DIM)
            pc_h = pc[c * CHUNK:c * CHUNK + 1, hs]
            m_h = jnp.where(eye, pc_h, 0.0) + m_mat[n, :, :HEAD_DIM]
            g_h = m_mat[n, :, HEAD_DIM:] + g_add[n]
            mg_ref[c, h] = jnp.concatenate([m_h, g_h], axis=1)


def _dot_exact_lhs_t(x, ones_bf16):
    xh, xl = _split_bf16(x, 2)
    return _dot(xh, ones_bf16) + _dot(xl, ones_bf16)


def _rwkv_local_call(pf, mu, w0, wup, a0, aup, gup, k_k, k_a, r_k, S, tr=512):
    N = pf.shape[0]
    C = C_RWKV
    row = lambda a: a.reshape(1, -1)
    full = lambda a: pl.BlockSpec(a.shape, lambda i: (0,) * a.ndim)
    args = [row(mu), row(w0), wup, row(a0), aup, gup, row(k_k), row(k_a), row(r_k)]
    return pl.pallas_call(
        functools.partial(_rwkv_local_kernel, tr=tr, rows_per_seq=S),
        out_shape=(jax.ShapeDtypeStruct((N, 2 * C), F32),
                   jax.ShapeDtypeStruct((N // CHUNK, H_RWKV, HEAD_DIM, 2 * HEAD_DIM), F32),
                   jax.ShapeDtypeStruct((N, C), F32),
                   jax.ShapeDtypeStruct((N, C), F32)),
        grid=(N // tr,),
        in_specs=[pl.BlockSpec((tr, PF_COLS), lambda i: (i, 0)),
                  pl.BlockSpec((8, PF_COLS), lambda i: (jnp.maximum(i * (tr // 8) - 1, 0), 0))]
                 + [full(a) for a in args],
        out_specs=(pl.BlockSpec((tr, 2 * C), lambda i: (i, 0)),
                   pl.BlockSpec((tr // CHUNK, H_RWKV, HEAD_DIM, 2 * HEAD_DIM), lambda i: (i, 0, 0, 0)),
                   pl.BlockSpec((tr, C), lambda i: (i, 0)),
                   pl.BlockSpec((tr, C), lambda i: (i, 0))),
        compiler_params=_cparams(("arbitrary",)),
        name="rwkv_local",
    )(pf, pf, *args)


def _rwkv_scan_kernel(ry_ref, mg_ref, bonus_ref, gate_ref, lnw_ref, lnb_ref, o_ref, s_scr):
    C = C_RWKV
    B = ry_ref.shape[0]

    @pl.when(pl.program_id(0) == 0)
    def _():
        s_scr[...] = jnp.zeros_like(s_scr)

    li = lax.broadcasted_iota(jnp.int32, (C, C), 0) // HEAD_DIM
    lj = lax.broadcasted_iota(jnp.int32, (C, C), 1) // HEAD_DIM
    head_avg = jnp.where(li == lj, 1.0 / HEAD_DIM, 0.0).astype(BF16)
    bs = range(B)
    zero = jnp.zeros((HEAD_DIM, HEAD_DIM), F32)

    def block_diag(blocks):
        return jnp.concatenate([jnp.concatenate([blk if j == h else zero for j in range(H_RWKV)], axis=1)
                                for h, blk in enumerate(blocks)], axis=0)

    s0 = [s_scr[b].astype(BF16) for b in bs]
    ys = [_dot(ry_ref[b, :, :C].astype(BF16), s0[b]) + ry_ref[b, :, C:] for b in bs]
    for b in bs:
        m_bd = block_diag([mg_ref[b, 0, h, :, :HEAD_DIM] for h in range(H_RWKV)])
        g_bd = block_diag([mg_ref[b, 0, h, :, HEAD_DIM:] for h in range(H_RWKV)])
        s_scr[b] = _dot(m_bd.astype(BF16), s0[b]) + g_bd
    means = [_dot_exact_lhs_t(ys[b], head_avg) for b in bs]
    ds = [ys[b] - means[b] for b in bs]
    vs = [_dot_exact_lhs_t(ds[b] * ds[b], head_avg) for b in bs]
    for b in bs:
        yn = ds[b] * lax.rsqrt(vs[b] + GN_EPS) * lnw_ref[...] + lnb_ref[...]
        o_ref[b] = ((yn + bonus_ref[b]) * gate_ref[b]).astype(o_ref.dtype)


def _rwkv_scan_call(ry, mg, bonus, gate, ln_w, ln_b, B, S):
    C = C_RWKV
    nc = S // CHUNK
    return pl.pallas_call(
        _rwkv_scan_kernel,
        out_shape=jax.ShapeDtypeStruct((B, S, C), BF16),
        grid=(nc,),
        in_specs=[pl.BlockSpec((B, CHUNK, 2 * C), lambda c: (0, c, 0)),
                  pl.BlockSpec((B, 1, H_RWKV, HEAD_DIM, 2 * HEAD_DIM), lambda c: (0, c, 0, 0, 0)),
                  pl.BlockSpec((B, CHUNK, C), lambda c: (0, c, 0)),
                  pl.BlockSpec((B, CHUNK, C), lambda c: (0, c, 0)),
                  pl.BlockSpec((1, C), lambda c: (0, 0)),
                  pl.BlockSpec((1, C), lambda c: (0, 0))],
        out_specs=pl.BlockSpec((B, CHUNK, C), lambda c: (0, c, 0)),
        scratch_shapes=[pltpu.VMEM((B, C, C), F32)],
        compiler_params=_cparams(("arbitrary",)),
        name="rwkv_scan",
    )(ry.reshape(B, S, 2 * C), mg.reshape(B, nc, H_RWKV, HEAD_DIM, 2 * HEAD_DIM), bonus.reshape(B, S, C),
      gate.reshape(B, S, C), ln_w.reshape(1, C), ln_b.reshape(1, C))


def _online_update(carry, s, v):
    m, l, acc = carry
    m_new = jnp.maximum(m, jnp.max(s, axis=-1, keepdims=True))
    alpha = jnp.exp(m - m_new)
    p = jnp.exp(s - m_new)
    l = alpha * l + jnp.sum(p, axis=-1, keepdims=True)
    acc = alpha * acc + _dot(p.astype(BF16), v)
    return m_new, l, acc


def _attn_init(rows):
    return (jnp.full((rows, 1), NEG_INF, F32), jnp.zeros((rows, 1), F32), jnp.zeros((rows, LANE), F32))


def _online_update_t(carry, s_t, v_t):
    m, l, acc = carry
    m_new = jnp.maximum(m, jnp.max(s_t, axis=0, keepdims=True))
    alpha = jnp.exp(m - m_new)
    p = jnp.exp(s_t - m_new)
    l = alpha * l + jnp.sum(p, axis=0, keepdims=True)
    acc = alpha * acc + _dot(v_t, p.astype(BF16))
    return m_new, l, acc


def _attn_init_t(d, cols):
    return (jnp.full((1, cols), NEG_INF, F32), jnp.zeros((1, cols), F32), jnp.zeros((d, cols), F32))


def _online_update_multi(carries, ss, vts):
    n = range(len(ss))
    ms = [jnp.maximum(carries[i][0], jnp.max(ss[i], axis=0, keepdims=True)) for i in n]
    ps = [jnp.exp(ss[i] - ms[i]) for i in n]

    def pv(v, p):
        if not isinstance(v, (tuple, list)):
            return _dot(v, p.astype(BF16))
        tk = v[0].shape[1]
        parts = [_dot(v[k], p[k * tk:(k + 1) * tk].astype(BF16)) for k in range(len(v))]
        return functools.reduce(lambda a, b: a + b, parts)

    pvs = [pv(vts[i], ps[i]) for i in n]
    out = []
    for i in n:
        m, l, acc = carries[i]
        alpha = jnp.exp(m - ms[i])
        out.append((ms[i], alpha * l + jnp.sum(ps[i], axis=0, keepdims=True), alpha * acc + pvs[i]))
    return tuple(out)


def _fox_prep_kernel(misc_ref, q_ref, k_ref, v_ref, bf_ref, place_ref, spread_ref, ones_ref,
                     qa_ref, ka_ref, vt_ref, carry_scr, *, t, tk):
    @pl.when(pl.program_id(1) == 0)
    def _():
        carry_scr[...] = jnp.zeros_like(carry_scr)

    v_t = v_ref[0].astype(F32).T
    for h in range(H_FOX):
        for j in range(t // tk):
            vt_ref[0, h, j] = v_t[h * HEAD_DIM:(h + 1) * HEAD_DIM, j * tk:(j + 1) * tk].astype(BF16)

    lf = -_softplus(-(misc_ref[0] + bf_ref[...]))
    ti = lax.broadcasted_iota(jnp.int32, (t, t), 0)
    tj = lax.broadcasted_iota(jnp.int32, (t, t), 1)
    tri = jnp.where(tj <= ti, 1.0, 0.0).astype(BF16)
    cum = _dot_exact_lhs(tri, lf, 3) + carry_scr[...]
    carry_scr[...] = cum[t - 1:t]
    out = _dot(k_ref[0], spread_ref[...])
    for j, part in enumerate(_split_bf16(-cum, 3)):
        out = out + _dot(part, place_ref[j])
    ka_ref[0] = out.astype(BF16)
    qa_ref[0] = (_dot(q_ref[0], spread_ref[...]) + ones_ref[...]).astype(BF16)


FOX_TK = 512


def _fox_prep_call(pf3, pb3, b_f_row, place, spread, ones_row, t=512):
    B, S, _ = pf3.shape
    W = H_FOX * LANE
    tk = FOX_TK
    col = lambda c: pl.BlockSpec((1, t, C_FOX), lambda b, i: (b, i, c // C_FOX))
    const = lambda a: pl.BlockSpec(a.shape, lambda b, i: (0,) * a.ndim)
    return pl.pallas_call(
        functools.partial(_fox_prep_kernel, t=t, tk=tk),
        out_shape=(jax.ShapeDtypeStruct((B, S, W), BF16),
                   jax.ShapeDtypeStruct((B, S, W), BF16),
                   jax.ShapeDtypeStruct((B, H_FOX, S // tk, HEAD_DIM, tk), BF16)),
        grid=(B, S // t),
        in_specs=[pl.BlockSpec((1, t, LANE), lambda b, i: (b, i, PF_MISC // LANE)),
                  col(PB_FQ), col(PB_FK), col(PB_FV),
                  const(b_f_row), const(place), const(spread), const(ones_row)],
        out_specs=(pl.BlockSpec((1, t, W), lambda b, i: (b, i, 0)),
                   pl.BlockSpec((1, t, W), lambda b, i: (b, i, 0)),
                   pl.BlockSpec((1, H_FOX, t // tk, HEAD_DIM, tk), lambda b, i: (b, 0, i, 0, 0))),
        scratch_shapes=[pltpu.VMEM((1, LANE), F32)],
        compiler_params=_cparams(("arbitrary", "arbitrary")),
        name="fox_prep",
    )(pf3, pb3, pb3, pb3, b_f_row, place, spread, ones_row)


def _fox_attn_kernel(q_ref, k_ref, vt_ref, o_ref, *, tq, tk):
    qi = pl.program_id(1)
    krow = lax.broadcasted_iota(jnp.int32, (tk, tq), 0)
    qcol = lax.broadcasted_iota(jnp.int32, (tk, tq), 1)
    per = tq // tk
    heads = range(H_FOX)
    qs = [q_ref[0, :, h * LANE:(h + 1) * LANE] for h in heads]

    def logits(j, h):
        start = pl.multiple_of(j * tk, tk)
        return _dot_nt(k_ref[0, pl.ds(start, tk), h * LANE:(h + 1) * LANE], qs[h])

    carries = tuple(_attn_init_t(HEAD_DIM, tq) for _ in heads)
    for d in range(per):
        j = qi * per + d
        ss = [jnp.where(krow + d * tk <= qcol, logits(j, h), NEG_INF) for h in heads]
        carries = _online_update_multi(carries, ss, [vt_ref[0, h, j] for h in heads])

    def body(j, carries):
        return _online_update_multi(carries, [logits(j, h) for h in heads], [vt_ref[0, h, j] for h in heads])

    carries = lax.fori_loop(0, qi * per, body, carries)
    outs = [acc / l for (_, l, acc) in carries]
    for hp in range(H_FOX // 2):
        pair = jnp.concatenate([outs[2 * hp], outs[2 * hp + 1]], axis=0)
        o_ref[0, :, hp * LANE:(hp + 1) * LANE] = pair.T.astype(o_ref.dtype)


def _fox_attn_call(qaug, kaug, v_t, tq=512):
    B, S, W = qaug.shape
    tk = FOX_TK
    return pl.pallas_call(
        functools.partial(_fox_attn_kernel, tq=tq, tk=tk),
        out_shape=jax.ShapeDtypeStruct((B, S, C_FOX), BF16),
        grid=(B, S // tq),
        in_specs=[pl.BlockSpec((1, tq, W), lambda b, i: (b, i, 0)),
                  pl.BlockSpec((1, S, W), lambda b, i: (b, 0, 0)),
                  pl.BlockSpec((1, H_FOX, S // tk, HEAD_DIM, tk), lambda b, i: (b, 0, 0, 0, 0))],
        out_specs=pl.BlockSpec((1, tq, C_FOX), lambda b, i: (b, i, 0)),
        compiler_params=_cparams(("arbitrary", "arbitrary")),
        name="fox_attn",
    )(qaug, kaug, v_t)


def _bias_of_dist(n, tab_ref, h):
    val = jnp.zeros(n.shape, F32) + tab_ref[0, h]
    for b in range(1, NUM_BUCKETS):
        val = jnp.where(n >= BUCKET_LB[b], tab_ref[b, h], val)
    return val - tab_ref[NUM_BUCKETS - 1, h]


def _bias_cmp_kernel(tab_ref, o_ref, *, tt):
    t0 = pl.program_id(0) * tt
    nc = o_ref.shape[1]
    c = lax.broadcasted_iota(jnp.int32, (nc, tt), 0)
    t = lax.broadcasted_iota(jnp.int32, (nc, tt), 1) + t0
    n = jnp.maximum(t - (c * D_CMP + L_CMP - 1), 0)
    for h in range(H_NSA):
        o_ref[h] = _bias_of_dist(n, tab_ref, h)


def _bias_near_kernel(tab_ref, o_ref, *, tq):
    j = lax.broadcasted_iota(jnp.int32, (tq, tq), 0)
    i = lax.broadcasted_iota(jnp.int32, (tq, tq), 1)
    for g in range(G_NSA):
        for near in range(2):
            n = jnp.maximum(i - j + near * tq, 0)
            for h in range(HPG):
                o_ref[g, near, :, h * tq:(h + 1) * tq] = _bias_of_dist(n, tab_ref, g * HPG + h)


def _bias_tables(rel_bias, S, tq):
    nc = S // D_CMP
    tt = 512
    smem = pl.BlockSpec(memory_space=pltpu.SMEM)
    bias_c = pl.pallas_call(
        functools.partial(_bias_cmp_kernel, tt=tt),
        out_shape=jax.ShapeDtypeStruct((H_NSA, nc, S), F32),
        grid=(S // tt,),
        in_specs=[smem],
        out_specs=pl.BlockSpec((H_NSA, nc, tt), lambda i: (0, 0, i)),
        compiler_params=_cparams(("arbitrary",)),
        name="nsa_bias_cmp",
    )(rel_bias)
    bias_n = pl.pallas_call(
        functools.partial(_bias_near_kernel, tq=tq),
        out_shape=jax.ShapeDtypeStruct((G_NSA, 2, tq, HPG * tq), F32),
        in_specs=[smem],
        name="nsa_bias_near",
    )(rel_bias)
    return bias_c, bias_n


def _gelu_tanh(x):
    return 0.5 * x * (1.0 + jnp.tanh(math.sqrt(2.0 / math.pi) * (x + 0.044715 * (x * x * x))))


def _nsa_compress_kernel(kc_ref, vc_ref, wk1_ref, wv1_ref, wk2_ref, wv2_ref, pek_ref, pev_ref,
                         kcmp_ref, vcmpT_ref):
    nc = kc_ref.shape[1]
    for g in range(G_NSA):
        for src_ref, w1_ref, w2_ref, pe_ref, is_k in ((kc_ref, wk1_ref, wk2_ref, pek_ref, True),
                                                      (vc_ref, wv1_ref, wv2_ref, pev_ref, False)):
            ch = src_ref[0]
            p1 = _dot(ch, w1_ref[g, 0])
            p2 = _dot(ch, w1_ref[g, 1])
            pec = (_dot(pe_ref[0], w1_ref[g, 0].astype(F32)) + _dot(pe_ref[1], w1_ref[g, 1].astype(F32)))[0:1]
            hid = p1 + pltpu.roll(p2, nc - 1, 0) + pec
            act = _gelu_tanh(hid).astype(BF16)
            if is_k:
                kcmp_ref[0, g] = _dot(act, w2_ref[...]).astype(BF16)
            else:
                vcmpT_ref[0, g] = _dot_nt(w2_ref[...], act).astype(BF16)


def _nsa_compress_call(kc_flat, vc_flat, wk1, wv1, wk2, wv2, pek, pev):
    B, nc, W = kc_flat.shape
    full = lambda a: pl.BlockSpec(a.shape, lambda b: (0,) * a.ndim)
    return pl.pallas_call(
        _nsa_compress_kernel,
        out_shape=(jax.ShapeDtypeStruct((B, G_NSA, nc, HEAD_DIM), BF16),
                   jax.ShapeDtypeStruct((B, G_NSA, HEAD_DIM, nc), BF16)),
        grid=(B,),
        in_specs=[pl.BlockSpec((1, nc, W), lambda b: (b, 0, 0)),
                  pl.BlockSpec((1, nc, W), lambda b: (b, 0, 0)),
                  full(wk1), full(wv1), full(wk2), full(wv2), full(pek), full(pev)],
        out_specs=(pl.BlockSpec((1, G_NSA, nc, HEAD_DIM), lambda b: (b, 0, 0, 0)),
                   pl.BlockSpec((1, G_NSA, HEAD_DIM, nc), lambda b: (b, 0, 0, 0))),
        compiler_params=_cparams(("arbitrary",)),
        name="nsa_compress",
    )(kc_flat, vc_flat, wk1, wv1, wk2, wv2, pek, pev)


def _nsa_select_kernel(q_ref, kcmp_ref, vcmpT_ref, bias_ref, ov_ref, ocmp_ref, neg_ref, *, tq):
    qi = pl.program_id(0)
    nc = kcmp_ref.shape[2]
    nsb = ov_ref.shape[0]
    t = lax.broadcasted_iota(jnp.int32, (1, tq), 1) + qi * tq
    cidx = lax.broadcasted_iota(jnp.int32, (nc, 1), 0)
    valid_c = (cidx * D_CMP + L_CMP - 1) <= t
    jf = lax.broadcasted_iota(jnp.int32, (nsb, tq), 0).astype(F32)
    jb = lax.broadcasted_iota(jnp.int32, (nsb, 1), 0)
    back = t // L_SLC - jb
    valid_b = back >= 0
    forced = (jb == 0) | (valid_b & (back < N_LOCAL))
    heads = range(H_NSA)
    groups = range(G_NSA)
    ss = [jnp.where(valid_c, _dot_nt(kcmp_ref[0, hh // HPG], q_ref[0, :, hh * HEAD_DIM:(hh + 1) * HEAD_DIM])
                    + bias_ref[hh], NEG_INF) for hh in heads]
    ms = [jnp.max(s, axis=0, keepdims=True) for s in ss]
    ps = [jnp.where(valid_c, jnp.exp(ss[hh] - ms[hh]), 0.0) for hh in heads]
    ls = [jnp.sum(p, axis=0, keepdims=True) for p in ps]
    ps = [(ps[hh] / jnp.maximum(ls[hh], 1e-30)).astype(BF16) for hh in heads]
    for hh in heads:
        ocmp_ref[0, hh] = _dot(vcmpT_ref[0, hh // HPG], ps[hh])
    imps = [_dot(ov_ref[...], ps[hh]) for hh in heads]
    group_imp = lambda g: functools.reduce(lambda a, b: a + b, imps[g * HPG:(g + 1) * HPG])
    scores = [jnp.where(valid_b, jnp.where(forced, FORCE_SCORE, group_imp(g)), -1.0) for g in groups]
    sels = [jnp.zeros((nsb, tq), F32) for _ in groups]
    for _ in range(N_SLC):
        mxs = [jnp.max(sc, axis=0, keepdims=True) for sc in scores]
        firsts = [jnp.min(jnp.where(scores[g] == mxs[g], jf, float(nsb)), axis=0, keepdims=True) for g in groups]
        picks = [jf == firsts[g] for g in groups]
        sels = [jnp.where(picks[g] & (mxs[g] >= 0.0), 1.0, sels[g]) for g in groups]
        scores = [jnp.where(picks[g], -2.0, scores[g]) for g in groups]
    for g in groups:
        neg = jnp.where(sels[g] > 0.0, 0.0, NEG_INF)
        pieces = [jnp.zeros((HEAD_DIM, tq), F32), neg]
        if nsb < HEAD_DIM:
            pieces.append(jnp.zeros((HEAD_DIM - nsb, tq), F32))
        neg_ref[0, g] = jnp.concatenate(pieces, axis=0).T.astype(BF16)


def _nsa_select_call(pb3, kcmp, vcmpT, bias_c, overlap_t, tq):
    B, S, _ = pb3.shape
    nc = kcmp.shape[2]
    return pl.pallas_call(
        functools.partial(_nsa_select_kernel, tq=tq),
        out_shape=(jax.ShapeDtypeStruct((B, H_NSA, HEAD_DIM, S), F32),
                   jax.ShapeDtypeStruct((B, G_NSA, S, LANE), BF16)),
        grid=(S // tq, B),
        in_specs=[pl.BlockSpec((1, tq, C_NSA), lambda i, b: (b, i, PB_NQ // C_NSA)),
                  pl.BlockSpec((1, G_NSA, nc, HEAD_DIM), lambda i, b: (b, 0, 0, 0)),
                  pl.BlockSpec((1, G_NSA, HEAD_DIM, nc), lambda i, b: (b, 0, 0, 0)),
                  pl.BlockSpec((H_NSA, nc, tq), lambda i, b: (0, 0, i)),
                  pl.BlockSpec(overlap_t.shape, lambda i, b: (0, 0))],
        out_specs=(pl.BlockSpec((1, H_NSA, HEAD_DIM, tq), lambda i, b: (b, 0, 0, i)),
                   pl.BlockSpec((1, G_NSA, tq, LANE), lambda i, b: (b, 0, i, 0))),
        compiler_params=_cparams(("arbitrary", "arbitrary")),
        name="nsa_select",
    )(pb3, kcmp, vcmpT, bias_c, overlap_t)


def _nsa_attn_kernel(q_ref, neg_ref, ks_ref, vs_ref, kw_ref, vw_ref, ocmp_ref, misc_ref, bias_ref, spread_ref,
                     o_ref, kaug_scr, kwin_scr, vst_scr, vwt_scr, *, tq):
    qi = pl.program_id(1)
    S = ks_ref.shape[1]
    cols = HPG * tq
    n_tiles = S // tq
    assert WINDOW == 2 * tq

    @pl.when(qi == 0)
    def _():
        srow = lax.broadcasted_iota(jnp.int32, (tq, LANE), 0)
        slane = lax.broadcasted_iota(jnp.int32, (tq, LANE), 1)

        def fill(t, carry):
            start = pl.multiple_of(t * tq, tq)
            onehot = jnp.where(slane == HEAD_DIM + (srow + t * tq) // L_SLC, 1.0, 0.0)
            ks = ks_ref[0, pl.ds(start, tq), :].astype(F32)
            kw = kw_ref[0, pl.ds(start, tq), :].astype(F32)
            vs_t = vs_ref[0, pl.ds(start, tq), :].astype(F32).T
            vw_t = vw_ref[0, pl.ds(start, tq), :].astype(F32).T
            for g in range(G_NSA):
                ks_g = ks if g == 0 else pltpu.roll(ks, LANE - g * HEAD_DIM, 1)
                kw_g = kw if g == 0 else pltpu.roll(kw, LANE - g * HEAD_DIM, 1)
                kaug_scr[g, pl.ds(start, tq), :] = (jnp.where(slane < HEAD_DIM, ks_g, 0.0) + onehot).astype(BF16)
                kwin_scr[g, pl.ds(start, tq), :] = jnp.where(slane < HEAD_DIM, kw_g, 0.0).astype(BF16)
                vst_scr[g, t] = vs_t[g * HEAD_DIM:(g + 1) * HEAD_DIM].astype(BF16)
                vwt_scr[g, t] = vw_t[g * HEAD_DIM:(g + 1) * HEAD_DIM].astype(BF16)
            return carry

        lax.fori_loop(0, n_tiles, fill, 0)

    jk = lax.broadcasted_iota(jnp.int32, (tq, cols), 0)
    ic = lax.broadcasted_iota(jnp.int32, (tq, cols), 1) % tq
    causal = jk <= ic
    in_window = jk > ic
    prev = jnp.maximum(qi - 1, 0)
    prev2 = jnp.maximum(qi - 2, 0)
    has_prev = qi >= 1
    has_prev2 = qi >= 2
    groups = range(G_NSA)
    q_wide = _dot(q_ref[0], spread_ref[...])
    qa = [jnp.concatenate([(q_wide[:, (g * HPG + h) * LANE:(g * HPG + h + 1) * LANE]
                            + neg_ref[0, g].astype(F32)).astype(BF16)
                           for h in range(HPG)], axis=0) for g in groups]

    def sel_logits(j, g):
        start = pl.multiple_of(j * tq, tq)
        return _dot_nt(kaug_scr[g, pl.ds(start, tq), :], qa[g])

    def win_logits(j, g):
        start = pl.multiple_of(j * tq, tq)
        return _dot_nt(kwin_scr[g, pl.ds(start, tq), :], qa[g])

    carries = tuple(_attn_init_t(HEAD_DIM, cols) for _ in range(2 * G_NSA))
    ss = ([jnp.where(causal, sel_logits(qi, g) + bias_ref[g, 0], NEG_INF) for g in groups]
          + [jnp.where(causal, win_logits(qi, g) + bias_ref[g, 0], NEG_INF) for g in groups])
    carries = _online_update_multi(carries, ss, [vst_scr[g, qi] for g in groups] + [vwt_scr[g, qi] for g in groups])
    ss = ([jnp.where(has_prev, sel_logits(prev, g) + bias_ref[g, 1], NEG_INF) for g in groups]
          + [jnp.where(has_prev, win_logits(prev, g) + bias_ref[g, 1], NEG_INF) for g in groups])
    carries = _online_update_multi(carries, ss,
                                   [vst_scr[g, prev] for g in groups] + [vwt_scr[g, prev] for g in groups])
    ss = [jnp.where(has_prev2 & in_window, win_logits(prev2, g), NEG_INF) for g in groups]
    win = _online_update_multi(carries[G_NSA:], ss, [vwt_scr[g, prev2] for g in groups])

    n_far = prev

    def odd_tile(c):
        return _online_update_multi(c, [sel_logits(prev2, g) for g in groups], [vst_scr[g, prev2] for g in groups])

    sel = lax.cond((n_far % 2) == 1, odd_tile, lambda c: c, carries[:G_NSA])

    def body(j, c):
        start = pl.multiple_of(j * 2 * tq, 2 * tq)
        ss = [_dot_nt(kaug_scr[g, pl.ds(start, 2 * tq), :], qa[g]) for g in groups]
        return _online_update_multi(c, ss, [(vst_scr[g, 2 * j], vst_scr[g, 2 * j + 1]) for g in groups])

    sel = lax.fori_loop(0, n_far // 2, body, sel)

    gates = _sigmoid(misc_ref[0].T)
    outs = []
    for g in groups:
        o_slc = sel[g][2] / sel[g][1]
        o_win = win[g][2] / win[g][1]
        for h in range(HPG):
            hh = g * HPG + h
            base = MISC_GATE + hh * 3
            cs = slice(h * tq, (h + 1) * tq)
            outs.append(gates[base:base + 1] * ocmp_ref[0, hh] + gates[base + 1:base + 2] * o_slc[:, cs]
                        + gates[base + 2:base + 3] * o_win[:, cs])
    for pair in range(H_NSA // 2):
        both = jnp.concatenate([outs[2 * pair], outs[2 * pair + 1]], axis=0)
        o_ref[0, :, pair * LANE:(pair + 1) * LANE] = both.T.astype(o_ref.dtype)


def _nsa_attn_call(pb3, neg, ocmp, pf3, bias_n, spread, tq):
    B, S, _ = pb3.shape
    kv = lambda col: pl.BlockSpec((1, S, LANE), lambda b, i: (b, 0, col // LANE))
    return pl.pallas_call(
        functools.partial(_nsa_attn_kernel, tq=tq),
        out_shape=jax.ShapeDtypeStruct((B, S, C_NSA), BF16),
        grid=(B, S // tq),
        in_specs=[pl.BlockSpec((1, tq, C_NSA), lambda b, i: (b, i, PB_NQ // C_NSA)),
                  pl.BlockSpec((1, G_NSA, tq, LANE), lambda b, i: (b, 0, i, 0)),
                  kv(PB_KS), kv(PB_VS), kv(PB_KW), kv(PB_VW),
                  pl.BlockSpec((1, H_NSA, HEAD_DIM, tq), lambda b, i: (b, 0, 0, i)),
                  pl.BlockSpec((1, tq, LANE), lambda b, i: (b, i, PF_MISC // LANE)),
                  pl.BlockSpec(bias_n.shape, lambda b, i: (0, 0, 0, 0)),
                  pl.BlockSpec(spread.shape, lambda b, i: (0, 0))],
        out_specs=pl.BlockSpec((1, tq, C_NSA), lambda b, i: (b, i, 0)),
        scratch_shapes=[pltpu.VMEM((G_NSA, S, LANE), BF16),
                        pltpu.VMEM((G_NSA, S, LANE), BF16),
                        pltpu.VMEM((G_NSA, S // tq, HEAD_DIM, tq), BF16),
                        pltpu.VMEM((G_NSA, S // tq, HEAD_DIM, tq), BF16)],
        compiler_params=_cparams(("arbitrary", "arbitrary")),
        name="nsa_attn",
    )(pb3, neg, pb3, pb3, pb3, pb3, ocmp, pf3, bias_n, spread)


def _outproj_kernel(x_ref, ya_ref, yb_ref, yc_ref, w_ref, gm_ref, g_ref, o_ref):
    y_cat = jnp.concatenate([ya_ref[...], yb_ref[...], yc_ref[...]], axis=1)
    o_ref[...] = x_ref[...] + gm_ref[0] * _rmsnorm(_dot(y_cat, w_ref[...]), g_ref[...])


def _outproj_call(x2d, ya, yb, yc, w, gm, g, S, tm=512):
    N, D = x2d.shape
    per = S // tm
    rows = lambda a: pl.BlockSpec((tm, a.shape[1]), lambda i: (i, 0))
    return pl.pallas_call(
        _outproj_kernel,
        out_shape=jax.ShapeDtypeStruct((N, D), F32),
        grid=(N // tm,),
        in_specs=[rows(x2d), rows(ya), rows(yb), rows(yc),
                  pl.BlockSpec(w.shape, lambda i: (0, 0)),
                  pl.BlockSpec((1, 1, D), lambda i: (i // per, 0, 0)),
                  pl.BlockSpec((1, D), lambda i: (0, 0))],
        out_specs=pl.BlockSpec((tm, D), lambda i: (i, 0)),
        compiler_params=_cparams(("arbitrary",)),
        name="out_proj",
    )(x2d, ya, yb, yc, w, gm, g.reshape(1, D))


def _ffn_kernel(x_ref, halo_ref, sc_ref, sh_ref, gf_ref, g2_ref, g3_ref, wg_ref, wv_ref,
                cwg_ref, cwv_ref, cbg_ref, cbv_ref, wd_ref, o_ref, h_scr, acc_scr, *, tm, rows_per_seq):
    i = pl.program_id(0)
    f = pl.program_id(1)

    @pl.when(f == 0)
    def _():
        xe = jnp.concatenate([halo_ref[...], x_ref[...]], axis=0)
        h = _rmsnorm(xe, g2_ref[...]) * (1.0 + sc_ref[0]) + sh_ref[0]
        row = lax.broadcasted_iota(jnp.int32, (tm + 8, 1), 0)
        first = (i * tm) % rows_per_seq == 0
        h_scr[...] = jnp.where((row < 8) & first, 0.0, h).astype(BF16)
        acc_scr[...] = jnp.zeros_like(acc_scr)

    h = h_scr[...]

    def conv(w_ref, cw_ref, cb_ref):
        u = _dot(h, w_ref[...])
        y = (cw_ref[2:3] * u + cw_ref[1:2] * pltpu.roll(u, 1, 0) + cw_ref[0:1] * pltpu.roll(u, 2, 0)
             + cb_ref[...])
        return y[8:]

    gate = conv(wg_ref, cwg_ref, cbg_ref)
    val = conv(wv_ref, cwv_ref, cbv_ref)
    act = (gate * _sigmoid(gate) * val).astype(BF16)
    acc_scr[...] += _dot(act, wd_ref[...])

    @pl.when(f == pl.num_programs(1) - 1)
    def _():
        o_ref[...] = x_ref[...] + gf_ref[0] * _rmsnorm(acc_scr[...], g3_ref[...])


def _ffn_call(x2d, sc, sh, gf, g2, g3, w_up, conv_w, conv_b, w_down, S, tm=1024, tf=1408):
    N, D = x2d.shape
    F = w_down.shape[0]
    nf = F // tf
    per = S // tm
    mod = pl.BlockSpec((1, 1, D), lambda i, f: (i // per, 0, 0))
    vec = pl.BlockSpec((1, D), lambda i, f: (0, 0))
    cb = conv_b.reshape(1, 2 * F)
    return pl.pallas_call(
        functools.partial(_ffn_kernel, tm=tm, rows_per_seq=S),
        out_shape=jax.ShapeDtypeStruct((N, D), F32),
        grid=(N // tm, nf),
        in_specs=[pl.BlockSpec((tm, D), lambda i, f: (i, 0)),
                  pl.BlockSpec((8, D), lambda i, f: (jnp.maximum(i * (tm // 8) - 1, 0), 0)),
                  mod, mod, mod, vec, vec,
                  pl.BlockSpec((D, tf), lambda i, f: (0, f)),
                  pl.BlockSpec((D, tf), lambda i, f: (0, nf + f)),
                  pl.BlockSpec((CONV_W, tf), lambda i, f: (0, f)),
                  pl.BlockSpec((CONV_W, tf), lambda i, f: (0, nf + f)),
                  pl.BlockSpec((1, tf), lambda i, f: (0, f)),
                  pl.BlockSpec((1, tf), lambda i, f: (0, nf + f)),
                  pl.BlockSpec((tf, D), lambda i, f: (f, 0))],
        out_specs=pl.BlockSpec((tm, D), lambda i, f: (i, 0)),
        scratch_shapes=[pltpu.VMEM((tm + 8, D), BF16), pltpu.VMEM((tm, D), F32)],
        compiler_params=_cparams(("arbitrary", "arbitrary")),
        name="conv_ffn",
    )(x2d, x2d, sc, sh, gf, g2.reshape(1, D), g3.reshape(1, D), w_up, w_up, conv_w, conv_w, cb, cb, w_down)


def _column_maps():
    n_rwkv = 3 * C_RWKV + R_DECAY + R_AAA + R_GATE
    n_fox = 3 * C_FOX + H_FOX
    fox0 = n_rwkv
    nsa0 = n_rwkv + n_fox
    pf = np.full(PF_COLS, -1, np.int64)
    pf[PF_R:PF_R + 3 * C_RWKV] = np.arange(3 * C_RWKV)
    assert (LORA_A, LORA_G, LANE) == (R_DECAY, R_DECAY + R_AAA, R_DECAY + R_AAA + R_GATE)
    pf[PF_LORA:PF_LORA + LANE] = 3 * C_RWKV + np.arange(LANE)
    pf[PF_MISC + MISC_F:PF_MISC + MISC_F + H_FOX] = fox0 + 3 * C_FOX + np.arange(H_FOX)
    nsa_gate0 = nsa0 + C_NSA + 6 * G_NSA * HEAD_DIM
    pf[PF_MISC + MISC_GATE:PF_MISC + MISC_GATE + 3 * H_NSA] = nsa_gate0 + np.arange(3 * H_NSA)

    pb = np.full(PB_COLS, -1, np.int64)
    scale = np.ones(PB_COLS, np.float32)
    d = np.arange(HEAD_DIM)
    del d
    pb[PB_FQ:PB_FQ + C_FOX] = fox0 + np.arange(C_FOX)
    scale[PB_FQ:PB_FQ + C_FOX] = HEAD_DIM ** -0.5
    pb[PB_FK:PB_FK + C_FOX] = fox0 + C_FOX + np.arange(C_FOX)
    pb[PB_FV:PB_FV + C_FOX] = fox0 + 2 * C_FOX + np.arange(C_FOX)
    pb[PB_NQ:PB_NQ + C_NSA] = nsa0 + np.arange(C_NSA)
    scale[PB_NQ:PB_NQ + C_NSA] = HEAD_DIM ** -0.5
    ckv = G_NSA * HEAD_DIM
    kc0 = nsa0 + C_NSA
    for n, base in enumerate((PB_KC, PB_VC, PB_KS, PB_VS, PB_KW, PB_VW)):
        pb[base:base + ckv] = kc0 + n * ckv + np.arange(ckv)
    return pf, pb, scale


def _lora_rows(w, lane0):
    out = jnp.zeros((LANE, w.shape[1]), w.dtype)
    return out.at[lane0:lane0 + w.shape[0]].set(w).astype(BF16)


def _pad_rows(w, rows):
    return jnp.concatenate([w, jnp.zeros((rows - w.shape[0],) + w.shape[1:], w.dtype)], axis=0)


def _compress_w1(w1):
    hid = w1.shape[1]
    w = w1.reshape(2, D_CMP, HEAD_DIM, hid)
    out = jnp.zeros((G_NSA, 2, D_CMP, G_NSA, HEAD_DIM, hid), w1.dtype)
    for g in range(G_NSA):
        out = out.at[g, :, :, g].set(w)
    return out.reshape(G_NSA, 2, D_CMP * G_NSA * HEAD_DIM, hid).astype(BF16)


def _compress_pe(pe):
    half = pe.reshape(2, 1, D_CMP, 1, HEAD_DIM)
    return jnp.broadcast_to(half, (2, 8, D_CMP, G_NSA, HEAD_DIM)).reshape(2, 8, D_CMP * G_NSA * HEAD_DIM)


def _fox_place():
    place = np.zeros((3, LANE, H_FOX * LANE), np.float32)
    for j in range(3):
        for h in range(H_FOX):
            place[j, MISC_F + h, h * LANE + HEAD_DIM + j] = 1.0
    return jnp.asarray(place, BF16)


def _fox_spread():
    spread = np.zeros((C_FOX, H_FOX * LANE), np.float32)
    ones_row = np.zeros((1, H_FOX * LANE), np.float32)
    d = np.arange(HEAD_DIM)
    for h in range(H_FOX):
        spread[h * HEAD_DIM + d, h * LANE + d] = 1.0
        ones_row[0, h * LANE + HEAD_DIM:h * LANE + HEAD_DIM + 3] = 1.0
    return jnp.asarray(spread, BF16), jnp.asarray(ones_row, F32)


def _overlap_t(S):
    nc = S // D_CMP
    nsb = S // L_SLC
    c0 = np.arange(nc) * D_CMP
    c1 = c0 + L_CMP - 1
    s0 = np.arange(nsb) * L_SLC
    ov = (c0[None, :] <= s0[:, None] + L_SLC - 1) & (c1[None, :] >= s0[:, None])
    ov[:, nc - 1] = False
    return jnp.asarray(ov.astype(np.float32), BF16)


def kernel(x, c, ada_w, ada_b, norm_g, w_in, rwkv_mu, rwkv_w0, rwkv_w_up, rwkv_a0, rwkv_a_up, rwkv_g_up, rwkv_k_k, rwkv_k_a, rwkv_r_k, rwkv_ln_w, rwkv_ln_b, fox_b_f, nsa_pe_k, nsa_pe_v, nsa_ck_w1, nsa_ck_w2, nsa_cv_w1, nsa_cv_w2, rel_bias, w_out, ffn_up, ffn_conv_w, ffn_conv_b, ffn_down):
    B, S, D = x.shape
    L = w_in.shape[0]
    tq_sel = 512
    tq_nsa = WINDOW // 2
    assert S % 1024 == 0 and S // L_SLC <= HEAD_DIM and D == 1024

    pf_idx, pb_idx, pb_scale = _column_maps()
    w_ext = jnp.concatenate([w_in, jnp.zeros((L, D, 1), w_in.dtype)], axis=2)
    w_pf = jnp.take(w_ext, jnp.asarray(pf_idx), axis=2).astype(BF16)
    w_pb = (jnp.take(w_ext, jnp.asarray(pb_idx), axis=2) * pb_scale).astype(BF16)
    mu_ext = jnp.concatenate([rwkv_mu, jnp.zeros((L, 1), F32)], axis=1)
    mu_pf = jnp.take(mu_ext, jnp.asarray(pf_idx[:PF_RWKV]), axis=1)

    mod_all = _mod_call(c, ada_w, ada_b).reshape(L, B, 6, 1, D)
    bias_c, bias_n = _bias_tables(rel_bias, S, tq_nsa)
    place = _fox_place()
    spread, ones_row = _fox_spread()
    overlap_t = _overlap_t(S)

    x2d = x.reshape(B * S, D)
    for l in range(L):
        sh_m, sc_m, g_m, sh_f, sc_f, g_f = (mod_all[l, :, j] for j in range(6))
        pf, pb = _inproj_call(x2d, sc_m, sh_m, norm_g[l, 0], w_pf[l], w_pb[l], S)
        pf3 = pf.reshape(B, S, PF_COLS)
        pb3 = pb.reshape(B, S, PB_COLS)

        ry, mg, bonus, gate = _rwkv_local_call(
            pf, mu_pf[l], rwkv_w0[l], _lora_rows(rwkv_w_up[l], LORA_W), rwkv_a0[l],
            _lora_rows(rwkv_a_up[l], LORA_A), _lora_rows(rwkv_g_up[l], LORA_G),
            rwkv_k_k[l], rwkv_k_a[l], rwkv_r_k[l], S)
        ya = _rwkv_scan_call(ry, mg, bonus, gate, rwkv_ln_w[l], rwkv_ln_b[l], B, S)

        b_f_row = jnp.zeros((1, LANE), F32).at[0, MISC_F:MISC_F + H_FOX].set(fox_b_f[l])
        qaug, kaug, fox_vt = _fox_prep_call(pf3, pb3, b_f_row, place, spread, ones_row)
        yb = _fox_attn_call(qaug, kaug, fox_vt)

        kc_flat = pb3[:, :, PB_KC:PB_KC + LANE].reshape(B, S // D_CMP, D_CMP * LANE)
        vc_flat = pb3[:, :, PB_VC:PB_VC + LANE].reshape(B, S // D_CMP, D_CMP * LANE)
        kcmp, vcmpT = _nsa_compress_call(
            kc_flat, vc_flat, _compress_w1(nsa_ck_w1[l]), _compress_w1(nsa_cv_w1[l]),
            nsa_ck_w2[l].astype(BF16), nsa_cv_w2[l].T.astype(BF16),
            _compress_pe(nsa_pe_k[l]), _compress_pe(nsa_pe_v[l]))
        ocmp, neg = _nsa_select_call(pb3, kcmp, vcmpT, bias_c, overlap_t, tq_sel)
        yc = _nsa_attn_call(pb3, neg, ocmp, pf3, bias_n, spread, tq_nsa)

        x2d = _outproj_call(x2d, ya.reshape(B * S, C_RWKV), yb.reshape(B * S, C_FOX),
                            yc.reshape(B * S, C_NSA), w_out[l].astype(BF16), g_m, norm_g[l, 1], S)
        x2d = _ffn_call(x2d, sc_f, sh_f, g_f, norm_g[l, 2], norm_g[l, 3], ffn_up[l].astype(BF16),
                        ffn_conv_w[l], ffn_conv_b[l], ffn_down[l].astype(BF16), S)
    return x2d.reshape(B, S, D)
```

```python
import functools
import math

import numpy as np
import jax
import jax.numpy as jnp
from jax import lax
from jax.experimental import pallas as pl
from jax.experimental.pallas import tpu as pltpu

F32 = jnp.float32
BF16 = jnp.bfloat16

HEAD_DIM = 64
H_RWKV = 4
C_RWKV = H_RWKV * HEAD_DIM
H_FOX = 6
C_FOX = H_FOX * HEAD_DIM
H_NSA = 6
C_NSA = H_NSA * HEAD_DIM
G_NSA = 2
HPG = H_NSA // G_NSA
R_DECAY = 32
R_AAA = 32
R_GATE = 64
L_CMP = 32
D_CMP = 16
CMP_HID = 128
L_SLC = 64
N_SLC = 16
N_LOCAL = 2
WINDOW = 512
NUM_BUCKETS = 32
MAX_DISTANCE = 128
CONV_W = 3
RMS_EPS = 1e-6
GN_EPS = 64e-5
NEG_INF = -1e30
FORCE_SCORE = 1e4

LANE = 128
CHUNK = 64
VMEM_LIMIT = 56 * 1024 * 1024

TM_PROJ = 512
TM_FFN, TF_FFN = 1024, 1408
TR_RWKV = 512
T_FOX_PREP = 512
FOX_TQ = FOX_TK = 512
TQ_SEL = 512
TQ_NSA = WINDOW // 2

PF_R, PF_K, PF_V, PF_LORA, PF_MISC = 0, 256, 512, 768, 896
LORA_W, LORA_A, LORA_G = 0, 32, 64
PF_RWKV = 896
PF_COLS = 1024
MISC_F = 0
MISC_GATE = 8
PB_FQ, PB_FK, PB_FV, PB_NQ = 0, 384, 768, 1152
PB_KC, PB_VC, PB_KS, PB_VS, PB_KW, PB_VW = 1536, 1664, 1792, 1920, 2048, 2176
PB_COLS = 2304


def _bucket_lower_bounds():
    n = np.arange(0, 4 * MAX_DISTANCE, dtype=np.int64)
    max_exact = NUM_BUCKETS // 2
    nf = np.maximum(n, 1).astype(np.float32)
    large = max_exact + (np.log(nf / np.float32(max_exact)) / np.float32(math.log(MAX_DISTANCE / max_exact))
                         * np.float32(NUM_BUCKETS - max_exact)).astype(np.int32)
    large = np.minimum(large, NUM_BUCKETS - 1)
    bucket = np.where(n < max_exact, n, large)
    return [int(np.argmax(bucket >= b)) for b in range(NUM_BUCKETS)]


BUCKET_LB = _bucket_lower_bounds()


def _cparams(sem, vmem=None):
    return pltpu.CompilerParams(dimension_semantics=sem, vmem_limit_bytes=vmem or VMEM_LIMIT)


def _dot(a, b):
    return jnp.dot(a, b, preferred_element_type=F32)


def _dot_nt(a, b):
    return lax.dot_general(a, b, (((1,), (1,)), ((), ())), preferred_element_type=F32)


def _split_bf16(x, n):
    parts, r = [], x
    for i in range(n):
        p = r.astype(BF16)
        parts.append(p)
        if i + 1 < n:
            r = r - p.astype(F32)
    return parts


def _dot_exact_lhs(a_bf16, b, n):
    out = None
    for p in _split_bf16(b, n):
        t = _dot(a_bf16, p)
        out = t if out is None else out + t
    return out


def _softplus(x):
    return jnp.maximum(x, 0.0) + jnp.log(1.0 + jnp.exp(-jnp.abs(x)))


def _sigmoid(x):
    return 1.0 / (1.0 + jnp.exp(-x))


def _rmsnorm(x, g):
    return x * lax.rsqrt(jnp.mean(x * x, axis=-1, keepdims=True) + RMS_EPS) * g


def _mod_kernel(c_ref, w_ref, b_ref, o_ref):
    c = c_ref[...]
    s = (c * _sigmoid(c)).astype(BF16)
    o_ref[0] = _dot(s, w_ref[0].astype(BF16)) + b_ref[0]


def _mod_call(c, ada_w, ada_b):
    L, D, N = ada_w.shape
    B = c.shape[0]
    tn = 1536
    return pl.pallas_call(
        _mod_kernel,
        out_shape=jax.ShapeDtypeStruct((L, B, N), F32),
        grid=(L, N // tn),
        in_specs=[pl.BlockSpec((B, D), lambda l, j: (0, 0)),
                  pl.BlockSpec((1, D, tn), lambda l, j: (l, 0, j)),
                  pl.BlockSpec((1, 1, tn), lambda l, j: (l, 0, j))],
        out_specs=pl.BlockSpec((1, B, tn), lambda l, j: (l, 0, j)),
        compiler_params=_cparams(("arbitrary", "arbitrary")),
        name="adaln_mod",
    )(c, ada_w, ada_b.reshape(L, 1, N))


def _inproj_kernel(x_ref, sc_ref, sh_ref, g_ref, wf_ref, wb_ref, of_ref, ob_ref, *, n_chunk):
    h = _rmsnorm(x_ref[...], g_ref[...]) * (1.0 + sc_ref[0]) + sh_ref[0]
    h = h.astype(BF16)
    for w_ref, o_ref in ((wf_ref, of_ref), (wb_ref, ob_ref)):
        n = o_ref.shape[1]
        for n0 in range(0, n, n_chunk):
            n1 = min(n, n0 + n_chunk)
            o_ref[:, n0:n1] = _dot(h, w_ref[:, n0:n1]).astype(o_ref.dtype)


def _inproj_call(x2d, sc, sh, g, w_f32cols, w_bf16cols, S, tm=TM_PROJ):
    N, D = x2d.shape
    cf, cb = w_f32cols.shape[1], w_bf16cols.shape[1]
    per = S // tm
    return pl.pallas_call(
        functools.partial(_inproj_kernel, n_chunk=512),
        out_shape=(jax.ShapeDtypeStruct((N, cf), F32), jax.ShapeDtypeStruct((N, cb), BF16)),
        grid=(N // tm,),
        in_specs=[pl.BlockSpec((tm, D), lambda i: (i, 0)),
                  pl.BlockSpec((1, 1, D), lambda i: (i // per, 0, 0)),
                  pl.BlockSpec((1, 1, D), lambda i: (i // per, 0, 0)),
                  pl.BlockSpec((1, D), lambda i: (0, 0)),
                  pl.BlockSpec((D, cf), lambda i: (0, 0)),
                  pl.BlockSpec((D, cb), lambda i: (0, 0))],
        out_specs=(pl.BlockSpec((tm, cf), lambda i: (i, 0)), pl.BlockSpec((tm, cb), lambda i: (i, 0))),
        compiler_params=_cparams(("arbitrary",)),
        name="in_proj",
    )(x2d, sc, sh, g.reshape(1, D), w_f32cols, w_bf16cols)


def _rwkv_local_kernel(p_ref, halo_ref, mu_ref, w0_ref, wup_ref, a0_ref, aup_ref, gup_ref,
                       kk_ref, ka_ref, rk_ref,
                       ry_ref, mg_ref, bonus_ref, gate_ref, *, tr, rows_per_seq):
    C = C_RWKV
    i = pl.program_id(0)
    first = (i * tr) % rows_per_seq == 0
    p = p_ref[:, :PF_RWKV]
    row = lax.broadcasted_iota(jnp.int32, (tr, 1), 0)
    prev_last = jnp.where(first, 0.0, halo_ref[7:8, :PF_RWKV])
    prev = jnp.where(row == 0, prev_last, pltpu.roll(p, 1, 0))
    ps = p + (prev - p) * mu_ref[...]
    r = ps[:, PF_R:PF_R + C]
    k = ps[:, PF_K:PF_K + C]
    v = ps[:, PF_V:PF_V + C]
    lora = ps[:, PF_LORA:PF_LORA + LANE]
    wl = w0_ref[...] + _dot(jnp.tanh(lora).astype(BF16), wup_ref[...])
    lw = -jnp.exp(-_softplus(-wl) - 0.5)
    a = _sigmoid(a0_ref[...] + _dot(lora.astype(BF16), aup_ref[...]))
    gate_ref[...] = _dot(_sigmoid(lora).astype(BF16), gup_ref[...])

    li = lax.broadcasted_iota(jnp.int32, (C, C), 0) // HEAD_DIM
    lj = lax.broadcasted_iota(jnp.int32, (C, C), 1) // HEAD_DIM
    same_head = li == lj
    head_ones = jnp.where(same_head, 1.0, 0.0).astype(BF16)

    kk = k * kk_ref[...]
    nrm = jnp.sqrt(_dot_exact_lhs_t(kk * kk, head_ones))
    kk = kk / jnp.maximum(nrm, 1e-12)
    k2 = k * (1.0 + (a - 1.0) * ka_ref[...])
    bonus_ref[...] = _dot_exact_lhs_t(r * k2 * rk_ref[...], head_ones) * v
    avec = -kk
    bvec = kk * a

    nch = tr // CHUNK
    ti = lax.broadcasted_iota(jnp.int32, (tr, tr), 0)
    tj = lax.broadcasted_iota(jnp.int32, (tr, tr), 1)
    same_chunk = ti // CHUNK == tj // CHUNK
    tri_incl = jnp.where(same_chunk & (tj <= ti), 1.0, 0.0).astype(BF16)
    chunk_ones = jnp.where(same_chunk, 1.0, 0.0).astype(BF16)
    Lc = _dot_exact_lhs(tri_incl, lw, 3)
    Lend = _dot_exact_lhs(chunk_ones, lw, 3)
    e_cur = jnp.exp(Lc)
    e_inv = jnp.exp(-Lc)
    e_end = jnp.exp(Lend - Lc)
    At = avec * jnp.exp(Lc - lw)
    Rt = r * e_cur
    Bt = bvec * e_inv
    Kt = k2 * e_inv
    bh_t = (bvec * e_end).T
    kh_t = (k2 * e_end).T
    pc = jnp.exp(Lend)

    pairs = [(c, h) for c in range(nch) for h in range(H_RWKV)]
    blk = lambda x: jnp.stack([x[c * CHUNK:(c + 1) * CHUNK, h * HEAD_DIM:(h + 1) * HEAD_DIM]
                               for c, h in pairs]).astype(BF16)
    blk_t = lambda x: jnp.stack([x[h * HEAD_DIM:(h + 1) * HEAD_DIM, c * CHUNK:(c + 1) * CHUNK]
                                 for c, h in pairs]).astype(BF16)
    bmm = lambda a, b: jnp.einsum('nij,njk->nik', a.astype(BF16), b.astype(BF16), preferred_element_type=F32)
    bmm_nt = lambda a, b: jnp.einsum('nid,nkd->nik', a, b, preferred_element_type=F32)

    ci = lax.broadcasted_iota(jnp.int32, (CHUNK, CHUNK), 0)
    cj = lax.broadcasted_iota(jnp.int32, (CHUNK, CHUNK), 1)
    lower_strict = cj < ci
    lower_incl = cj <= ci
    a_b, r_b, b_b, k_b, v_b = blk(At), blk(Rt), blk(Bt), blk(Kt), blk(v)
    ar = jnp.concatenate([a_b, r_b], axis=1)
    ab = bmm_nt(ar, b_b)
    ak = bmm_nt(ar, k_b)
    n_mat = jnp.where(lower_strict, ab[:, :CHUNK], 0.0)
    a_ak = jnp.where(lower_strict, ak[:, :CHUNK], 0.0)
    a_rb = jnp.where(lower_incl, ab[:, CHUNK:], 0.0)
    a_rk = jnp.where(lower_incl, ak[:, CHUNK:], 0.0)
    av = bmm(jnp.concatenate([a_ak, a_rk], axis=1), v_b)
    x = jnp.concatenate([a_b.astype(F32), av[:, :CHUNK]], axis=2)
    x = x + bmm(n_mat, x)
    for _ in range(5):
        n_mat = bmm(n_mat, n_mat)
        x = x + bmm(n_mat, x)
    corr = bmm(a_rb, x)
    rbar = r_b.astype(F32) + corr[:, :, :HEAD_DIM]
    y0 = av[:, CHUNK:] + corr[:, :, HEAD_DIM:]
    m_mat = bmm(blk_t(bh_t), x)
    g_add = bmm(blk_t(kh_t), v_b)
    eye = ci == cj
    for c in range(nch):
        sl = slice(c * CHUNK, (c + 1) * CHUNK)
        ns = [c * H_RWKV + h for h in range(H_RWKV)]
        ry_ref[sl, :C] = jnp.concatenate([rbar[n] for n in ns], axis=1)
        ry_ref[sl, C:] = jnp.concatenate([y0[n] for n in ns], axis=1)
        for h, n in enumerate(ns):
            hs = slice(h * HEAD_DIM, (h + 1) * HEAD_DIM)
            pc_h = pc[c * CHUNK:c * CHUNK + 1, hs]
            m_h = jnp.where(eye, pc_h, 0.0) + m_mat[n, :, :HEAD_DIM]
            g_h = m_mat[n, :, HEAD_DIM:] + g_add[n]
            mg_ref[c, h] = jnp.concatenate([m_h, g_h], axis=1)


def _dot_exact_lhs_t(x, ones_bf16):
    xh, xl = _split_bf16(x, 2)
    return _dot(xh, ones_bf16) + _dot(xl, ones_bf16)


def _rwkv_local_call(pf, mu, w0, wup, a0, aup, gup, k_k, k_a, r_k, S, tr=TR_RWKV):
    N = pf.shape[0]
    C = C_RWKV
    row = lambda a: a.reshape(1, -1)
    full = lambda a: pl.BlockSpec(a.shape, lambda i: (0,) * a.ndim)
    args = [row(mu), row(w0), wup, row(a0), aup, gup, row(k_k), row(k_a), row(r_k)]
    return pl.pallas_call(
        functools.partial(_rwkv_local_kernel, tr=tr, rows_per_seq=S),
        out_shape=(jax.ShapeDtypeStruct((N, 2 * C), F32),
                   jax.ShapeDtypeStruct((N // CHUNK, H_RWKV, HEAD_DIM, 2 * HEAD_DIM), F32),
                   jax.ShapeDtypeStruct((N, C), F32),
                   jax.ShapeDtypeStruct((N, C), F32)),
        grid=(N // tr,),
        in_specs=[pl.BlockSpec((tr, PF_COLS), lambda i: (i, 0)),
                  pl.BlockSpec((8, PF_COLS), lambda i: (jnp.maximum(i * (tr // 8) - 1, 0), 0))]
                 + [full(a) for a in args],
        out_specs=(pl.BlockSpec((tr, 2 * C), lambda i: (i, 0)),
                   pl.BlockSpec((tr // CHUNK, H_RWKV, HEAD_DIM, 2 * HEAD_DIM), lambda i: (i, 0, 0, 0)),
                   pl.BlockSpec((tr, C), lambda i: (i, 0)),
                   pl.BlockSpec((tr, C), lambda i: (i, 0))),
        compiler_params=_cparams(("arbitrary",)),
        name="rwkv_local",
    )(pf, pf, *args)


def _rwkv_scan_kernel(ry_ref, mg_ref, bonus_ref, gate_ref, lnw_ref, lnb_ref, o_ref, s_scr):
    C = C_RWKV
    B = ry_ref.shape[0]

    @pl.when(pl.program_id(0) == 0)
    def _():
        s_scr[...] = jnp.zeros_like(s_scr)

    li = lax.broadcasted_iota(jnp.int32, (C, C), 0) // HEAD_DIM
    lj = lax.broadcasted_iota(jnp.int32, (C, C), 1) // HEAD_DIM
    head_avg = jnp.where(li == lj, 1.0 / HEAD_DIM, 0.0).astype(BF16)
    bs = range(B)
    zero = jnp.zeros((HEAD_DIM, HEAD_DIM), F32)

    def block_diag(blocks):
        return jnp.concatenate([jnp.concatenate([blk if j == h else zero for j in range(H_RWKV)], axis=1)
                                for h, blk in enumerate(blocks)], axis=0)

    s0 = [s_scr[b].astype(BF16) for b in bs]
    ys = [_dot(ry_ref[b, :, :C].astype(BF16), s0[b]) + ry_ref[b, :, C:] for b in bs]
    for b in bs:
        m_bd = block_diag([mg_ref[b, 0, h, :, :HEAD_DIM] for h in range(H_RWKV)])
        g_bd = block_diag([mg_ref[b, 0, h, :, HEAD_DIM:] for h in range(H_RWKV)])
        s_scr[b] = _dot(m_bd.astype(BF16), s0[b]) + g_bd
    means = [_dot_exact_lhs_t(ys[b], head_avg) for b in bs]
    ds = [ys[b] - means[b] for b in bs]
    vs = [_dot_exact_lhs_t(ds[b] * ds[b], head_avg) for b in bs]
    for b in bs:
        yn = ds[b] * lax.rsqrt(vs[b] + GN_EPS) * lnw_ref[...] + lnb_ref[...]
        o_ref[b] = ((yn + bonus_ref[b]) * gate_ref[b]).astype(o_ref.dtype)


def _rwkv_scan_call(ry, mg, bonus, gate, ln_w, ln_b, B, S):
    C = C_RWKV
    nc = S // CHUNK
    return pl.pallas_call(
        _rwkv_scan_kernel,
        out_shape=jax.ShapeDtypeStruct((B, S, C), BF16),
        grid=(nc,),
        in_specs=[pl.BlockSpec((B, CHUNK, 2 * C), lambda c: (0, c, 0)),
                  pl.BlockSpec((B, 1, H_RWKV, HEAD_DIM, 2 * HEAD_DIM), lambda c: (0, c, 0, 0, 0)),
                  pl.BlockSpec((B, CHUNK, C), lambda c: (0, c, 0)),
                  pl.BlockSpec((B, CHUNK, C), lambda c: (0, c, 0)),
                  pl.BlockSpec((1, C), lambda c: (0, 0)),
                  pl.BlockSpec((1, C), lambda c: (0, 0))],
        out_specs=pl.BlockSpec((B, CHUNK, C), lambda c: (0, c, 0)),
        scratch_shapes=[pltpu.VMEM((B, C, C), F32)],
        compiler_params=_cparams(("arbitrary",)),
        name="rwkv_scan",
    )(ry.reshape(B, S, 2 * C), mg.reshape(B, nc, H_RWKV, HEAD_DIM, 2 * HEAD_DIM), bonus.reshape(B, S, C),
      gate.reshape(B, S, C), ln_w.reshape(1, C), ln_b.reshape(1, C))


def _attn_init_t(d, cols):
    return (jnp.full((1, cols), NEG_INF, F32), jnp.zeros((1, cols), F32), jnp.zeros((d, cols), F32))


def _online_update_multi(carries, ss, vts):
    n = range(len(ss))
    ms = [jnp.maximum(carries[i][0], jnp.max(ss[i], axis=0, keepdims=True)) for i in n]
    ps = [jnp.exp(ss[i] - ms[i]) for i in n]

    def pv(v, p):
        if not isinstance(v, (tuple, list)):
            return _dot(v, p.astype(BF16))
        tk = v[0].shape[1]
        parts = [_dot(v[k], p[k * tk:(k + 1) * tk].astype(BF16)) for k in range(len(v))]
        return functools.reduce(lambda a, b: a + b, parts)

    pvs = [pv(vts[i], ps[i]) for i in n]
    out = []
    for i in n:
        m, l, acc = carries[i]
        alpha = jnp.exp(m - ms[i])
        out.append((ms[i], alpha * l + jnp.sum(ps[i], axis=0, keepdims=True), alpha * acc + pvs[i]))
    return tuple(out)


def _fox_prep_kernel(misc_ref, q_ref, k_ref, v_ref, bf_ref, place_ref, spread_ref, ones_ref,
                     qa_ref, ka_ref, vt_ref, carry_scr, *, t, tk):
    @pl.when(pl.program_id(1) == 0)
    def _():
        carry_scr[...] = jnp.zeros_like(carry_scr)

    v_t = v_ref[0].astype(F32).T
    for h in range(H_FOX):
        for j in range(t // tk):
            vt_ref[0, h, j] = v_t[h * HEAD_DIM:(h + 1) * HEAD_DIM, j * tk:(j + 1) * tk].astype(BF16)

    lf = -_softplus(-(misc_ref[0] + bf_ref[...]))
    ti = lax.broadcasted_iota(jnp.int32, (t, t), 0)
    tj = lax.broadcasted_iota(jnp.int32, (t, t), 1)
    tri = jnp.where(tj <= ti, 1.0, 0.0).astype(BF16)
    cum = _dot_exact_lhs(tri, lf, 3) + carry_scr[...]
    carry_scr[...] = cum[t - 1:t]
    out = _dot(k_ref[0], spread_ref[...])
    for j, part in enumerate(_split_bf16(-cum, 3)):
        out = out + _dot(part, place_ref[j])
    ka_ref[0] = out.astype(BF16)
    qa_ref[0] = (_dot(q_ref[0], spread_ref[...]) + ones_ref[...]).astype(BF16)


def _fox_prep_call(pf3, pb3, b_f_row, place, spread, ones_row, t=T_FOX_PREP):
    B, S, _ = pf3.shape
    W = H_FOX * LANE
    tk = FOX_TK
    col = lambda c: pl.BlockSpec((1, t, C_FOX), lambda b, i: (b, i, c // C_FOX))
    const = lambda a: pl.BlockSpec(a.shape, lambda b, i: (0,) * a.ndim)
    return pl.pallas_call(
        functools.partial(_fox_prep_kernel, t=t, tk=tk),
        out_shape=(jax.ShapeDtypeStruct((B, S, W), BF16),
                   jax.ShapeDtypeStruct((B, S, W), BF16),
                   jax.ShapeDtypeStruct((B, H_FOX, S // tk, HEAD_DIM, tk), BF16)),
        grid=(B, S // t),
        in_specs=[pl.BlockSpec((1, t, LANE), lambda b, i: (b, i, PF_MISC // LANE)),
                  col(PB_FQ), col(PB_FK), col(PB_FV),
                  const(b_f_row), const(place), const(spread), const(ones_row)],
        out_specs=(pl.BlockSpec((1, t, W), lambda b, i: (b, i, 0)),
                   pl.BlockSpec((1, t, W), lambda b, i: (b, i, 0)),
                   pl.BlockSpec((1, H_FOX, t // tk, HEAD_DIM, tk), lambda b, i: (b, 0, i, 0, 0))),
        scratch_shapes=[pltpu.VMEM((1, LANE), F32)],
        compiler_params=_cparams(("arbitrary", "arbitrary")),
        name="fox_prep",
    )(pf3, pb3, pb3, pb3, b_f_row, place, spread, ones_row)


def _fox_attn_kernel(q_ref, k_ref, vt_ref, o_ref, *, tq, tk):
    qi = pl.program_id(1)
    krow = lax.broadcasted_iota(jnp.int32, (tk, tq), 0)
    qcol = lax.broadcasted_iota(jnp.int32, (tk, tq), 1)
    per = tq // tk
    heads = range(H_FOX)
    qs = [q_ref[0, :, h * LANE:(h + 1) * LANE] for h in heads]

    def logits(j, h):
        start = pl.multiple_of(j * tk, tk)
        return _dot_nt(k_ref[0, pl.ds(start, tk), h * LANE:(h + 1) * LANE], qs[h])

    carries = tuple(_attn_init_t(HEAD_DIM, tq) for _ in heads)
    for d in range(per):
        j = qi * per + d
        ss = [jnp.where(krow + d * tk <= qcol, logits(j, h), NEG_INF) for h in heads]
        carries = _online_update_multi(carries, ss, [vt_ref[0, h, j] for h in heads])

    def body(j, carries):
        return _online_update_multi(carries, [logits(j, h) for h in heads], [vt_ref[0, h, j] for h in heads])

    carries = lax.fori_loop(0, qi * per, body, carries)
    outs = [acc / l for (_, l, acc) in carries]
    for hp in range(H_FOX // 2):
        pair = jnp.concatenate([outs[2 * hp], outs[2 * hp + 1]], axis=0)
        o_ref[0, :, hp * LANE:(hp + 1) * LANE] = pair.T.astype(o_ref.dtype)


def _fox_attn_call(qaug, kaug, v_t, tq=FOX_TQ):
    B, S, W = qaug.shape
    tk = FOX_TK
    return pl.pallas_call(
        functools.partial(_fox_attn_kernel, tq=tq, tk=tk),
        out_shape=jax.ShapeDtypeStruct((B, S, C_FOX), BF16),
        grid=(B, S // tq),
        in_specs=[pl.BlockSpec((1, tq, W), lambda b, i: (b, i, 0)),
                  pl.BlockSpec((1, S, W), lambda b, i: (b, 0, 0)),
                  pl.BlockSpec((1, H_FOX, S // tk, HEAD_DIM, tk), lambda b, i: (b, 0, 0, 0, 0))],
        out_specs=pl.BlockSpec((1, tq, C_FOX), lambda b, i: (b, i, 0)),
        compiler_params=_cparams(("arbitrary", "arbitrary")),
        name="fox_attn",
    )(qaug, kaug, v_t)


def _bias_of_dist(n, tab_ref, h):
    val = jnp.zeros(n.shape, F32) + tab_ref[0, h]
    for b in range(1, NUM_BUCKETS):
        val = jnp.where(n >= BUCKET_LB[b], tab_ref[b, h], val)
    return val - tab_ref[NUM_BUCKETS - 1, h]


def _bias_cmp_kernel(tab_ref, o_ref, *, tt):
    t0 = pl.program_id(0) * tt
    nc = o_ref.shape[1]
    c = lax.broadcasted_iota(jnp.int32, (nc, tt), 0)
    t = lax.broadcasted_iota(jnp.int32, (nc, tt), 1) + t0
    n = jnp.maximum(t - (c * D_CMP + L_CMP - 1), 0)
    for h in range(H_NSA):
        o_ref[h] = _bias_of_dist(n, tab_ref, h)


def _bias_near_kernel(tab_ref, o_ref, *, tq):
    j = lax.broadcasted_iota(jnp.int32, (tq, tq), 0)
    i = lax.broadcasted_iota(jnp.int32, (tq, tq), 1)
    for g in range(G_NSA):
        for near in range(2):
            n = jnp.maximum(i - j + near * tq, 0)
            for h in range(HPG):
                o_ref[g, near, :, h * tq:(h + 1) * tq] = _bias_of_dist(n, tab_ref, g * HPG + h)


def _bias_tables(rel_bias, S, tq):
    nc = S // D_CMP
    tt = 512
    smem = pl.BlockSpec(memory_space=pltpu.SMEM)
    bias_c = pl.pallas_call(
        functools.partial(_bias_cmp_kernel, tt=tt),
        out_shape=jax.ShapeDtypeStruct((H_NSA, nc, S), F32),
        grid=(S // tt,),
        in_specs=[smem],
        out_specs=pl.BlockSpec((H_NSA, nc, tt), lambda i: (0, 0, i)),
        compiler_params=_cparams(("arbitrary",)),
        name="nsa_bias_cmp",
    )(rel_bias)
    bias_n = pl.pallas_call(
        functools.partial(_bias_near_kernel, tq=tq),
        out_shape=jax.ShapeDtypeStruct((G_NSA, 2, tq, HPG * tq), F32),
        in_specs=[smem],
        name="nsa_bias_near",
    )(rel_bias)
    return bias_c, bias_n


def _gelu_tanh(x):
    return 0.5 * x * (1.0 + jnp.tanh(math.sqrt(2.0 / math.pi) * (x + 0.044715 * (x * x * x))))


def _nsa_compress_kernel(kc_ref, vc_ref, wk1_ref, wv1_ref, wk2_ref, wv2_ref, pek_ref, pev_ref,
                         kcmp_ref, vcmpT_ref):
    nc = kc_ref.shape[1]
    for g in range(G_NSA):
        for src_ref, w1_ref, w2_ref, pe_ref, is_k in ((kc_ref, wk1_ref, wk2_ref, pek_ref, True),
                                                      (vc_ref, wv1_ref, wv2_ref, pev_ref, False)):
            ch = src_ref[0]
            p1 = _dot(ch, w1_ref[g, 0])
            p2 = _dot(ch, w1_ref[g, 1])
            pec = (_dot(pe_ref[0], w1_ref[g, 0].astype(F32)) + _dot(pe_ref[1], w1_ref[g, 1].astype(F32)))[0:1]
            hid = p1 + pltpu.roll(p2, nc - 1, 0) + pec
            act = _gelu_tanh(hid).astype(BF16)
            if is_k:
                kcmp_ref[0, g] = _dot(act, w2_ref[...]).astype(BF16)
            else:
                vcmpT_ref[0, g] = _dot_nt(w2_ref[...], act).astype(BF16)


def _nsa_compress_call(kc_flat, vc_flat, wk1, wv1, wk2, wv2, pek, pev):
    B, nc, W = kc_flat.shape
    full = lambda a: pl.BlockSpec(a.shape, lambda b: (0,) * a.ndim)
    return pl.pallas_call(
        _nsa_compress_kernel,
        out_shape=(jax.ShapeDtypeStruct((B, G_NSA, nc, HEAD_DIM), BF16),
                   jax.ShapeDtypeStruct((B, G_NSA, HEAD_DIM, nc), BF16)),
        grid=(B,),
        in_specs=[pl.BlockSpec((1, nc, W), lambda b: (b, 0, 0)),
                  pl.BlockSpec((1, nc, W), lambda b: (b, 0, 0)),
                  full(wk1), full(wv1), full(wk2), full(wv2), full(pek), full(pev)],
        out_specs=(pl.BlockSpec((1, G_NSA, nc, HEAD_DIM), lambda b: (b, 0, 0, 0)),
                   pl.BlockSpec((1, G_NSA, HEAD_DIM, nc), lambda b: (b, 0, 0, 0))),
        compiler_params=_cparams(("arbitrary",)),
        name="nsa_compress",
    )(kc_flat, vc_flat, wk1, wv1, wk2, wv2, pek, pev)


def _nsa_select_kernel(q_ref, kcmp_ref, vcmpT_ref, bias_ref, ov_ref, ocmp_ref, neg_ref, *, tq):
    qi = pl.program_id(0)
    nc = kcmp_ref.shape[2]
    nsb = ov_ref.shape[0]
    t = lax.broadcasted_iota(jnp.int32, (1, tq), 1) + qi * tq
    cidx = lax.broadcasted_iota(jnp.int32, (nc, 1), 0)
    valid_c = (cidx * D_CMP + L_CMP - 1) <= t
    jf = lax.broadcasted_iota(jnp.int32, (nsb, tq), 0).astype(F32)
    jb = lax.broadcasted_iota(jnp.int32, (nsb, 1), 0)
    back = t // L_SLC - jb
    valid_b = back >= 0
    forced = (jb == 0) | (valid_b & (back < N_LOCAL))
    heads = range(H_NSA)
    groups = range(G_NSA)
    ss = [jnp.where(valid_c, _dot_nt(kcmp_ref[0, hh // HPG], q_ref[0, :, hh * HEAD_DIM:(hh + 1) * HEAD_DIM])
                    + bias_ref[hh], NEG_INF) for hh in heads]
    ms = [jnp.max(s, axis=0, keepdims=True) for s in ss]
    ps = [jnp.where(valid_c, jnp.exp(ss[hh] - ms[hh]), 0.0) for hh in heads]
    ls = [jnp.sum(p, axis=0, keepdims=True) for p in ps]
    ps = [(ps[hh] / jnp.maximum(ls[hh], 1e-30)).astype(BF16) for hh in heads]
    for hh in heads:
        ocmp_ref[0, hh] = _dot(vcmpT_ref[0, hh // HPG], ps[hh])
    imps = [_dot(ov_ref[...], ps[hh]) for hh in heads]
    group_imp = lambda g: functools.reduce(lambda a, b: a + b, imps[g * HPG:(g + 1) * HPG])
    scores = [jnp.where(valid_b, jnp.where(forced, FORCE_SCORE, group_imp(g)), -1.0) for g in groups]
    sels = [jnp.zeros((nsb, tq), F32) for _ in groups]
    for _ in range(N_SLC):
        mxs = [jnp.max(sc, axis=0, keepdims=True) for sc in scores]
        firsts = [jnp.min(jnp.where(scores[g] == mxs[g], jf, float(nsb)), axis=0, keepdims=True) for g in groups]
        picks = [jf == firsts[g] for g in groups]
        sels = [jnp.where(picks[g] & (mxs[g] >= 0.0), 1.0, sels[g]) for g in groups]
        scores = [jnp.where(picks[g], -2.0, scores[g]) for g in groups]
    for g in groups:
        neg = jnp.where(sels[g] > 0.0, 0.0, NEG_INF)
        pieces = [jnp.zeros((HEAD_DIM, tq), F32), neg]
        if nsb < HEAD_DIM:
            pieces.append(jnp.zeros((HEAD_DIM - nsb, tq), F32))
        neg_ref[0, g] = jnp.concatenate(pieces, axis=0).T.astype(BF16)


def _nsa_select_call(pb3, kcmp, vcmpT, bias_c, overlap_t, tq):
    B, S, _ = pb3.shape
    nc = kcmp.shape[2]
    return pl.pallas_call(
        functools.partial(_nsa_select_kernel, tq=tq),
        out_shape=(jax.ShapeDtypeStruct((B, H_NSA, HEAD_DIM, S), F32),
                   jax.ShapeDtypeStruct((B, G_NSA, S, LANE), BF16)),
        grid=(S // tq, B),
        in_specs=[pl.BlockSpec((1, tq, C_NSA), lambda i, b: (b, i, PB_NQ // C_NSA)),
                  pl.BlockSpec((1, G_NSA, nc, HEAD_DIM), lambda i, b: (b, 0, 0, 0)),
                  pl.BlockSpec((1, G_NSA, HEAD_DIM, nc), lambda i, b: (b, 0, 0, 0)),
                  pl.BlockSpec((H_NSA, nc, tq), lambda i, b: (0, 0, i)),
                  pl.BlockSpec(overlap_t.shape, lambda i, b: (0, 0))],
        out_specs=(pl.BlockSpec((1, H_NSA, HEAD_DIM, tq), lambda i, b: (b, 0, 0, i)),
                   pl.BlockSpec((1, G_NSA, tq, LANE), lambda i, b: (b, 0, i, 0))),
        compiler_params=_cparams(("arbitrary", "arbitrary")),
        name="nsa_select",
    )(pb3, kcmp, vcmpT, bias_c, overlap_t)


def _nsa_attn_kernel(q_ref, neg_ref, ks_ref, vs_ref, kw_ref, vw_ref, ocmp_ref, misc_ref, bias_ref, spread_ref,
                     o_ref, kaug_scr, kwin_scr, vst_scr, vwt_scr, *, tq):
    qi = pl.program_id(1)
    S = ks_ref.shape[1]
    cols = HPG * tq
    n_tiles = S // tq
    assert WINDOW == 2 * tq

    @pl.when(qi == 0)
    def _():
        srow = lax.broadcasted_iota(jnp.int32, (tq, LANE), 0)
        slane = lax.broadcasted_iota(jnp.int32, (tq, LANE), 1)

        def fill(t, carry):
            start = pl.multiple_of(t * tq, tq)
            onehot = jnp.where(slane == HEAD_DIM + (srow + t * tq) // L_SLC, 1.0, 0.0)
            ks = ks_ref[0, pl.ds(start, tq), :].astype(F32)
            kw = kw_ref[0, pl.ds(start, tq), :].astype(F32)
            vs_t = vs_ref[0, pl.ds(start, tq), :].astype(F32).T
            vw_t = vw_ref[0, pl.ds(start, tq), :].astype(F32).T
            for g in range(G_NSA):
                ks_g = ks if g == 0 else pltpu.roll(ks, LANE - g * HEAD_DIM, 1)
                kw_g = kw if g == 0 else pltpu.roll(kw, LANE - g * HEAD_DIM, 1)
                kaug_scr[g, pl.ds(start, tq), :] = (jnp.where(slane < HEAD_DIM, ks_g, 0.0) + onehot).astype(BF16)
                kwin_scr[g, pl.ds(start, tq), :] = jnp.where(slane < HEAD_DIM, kw_g, 0.0).astype(BF16)
                vst_scr[g, t] = vs_t[g * HEAD_DIM:(g + 1) * HEAD_DIM].astype(BF16)
                vwt_scr[g, t] = vw_t[g * HEAD_DIM:(g + 1) * HEAD_DIM].astype(BF16)
            return carry

        lax.fori_loop(0, n_tiles, fill, 0)

    jk = lax.broadcasted_iota(jnp.int32, (tq, cols), 0)
    ic = lax.broadcasted_iota(jnp.int32, (tq, cols), 1) % tq
    causal = jk <= ic
    in_window = jk > ic
    prev = jnp.maximum(qi - 1, 0)
    prev2 = jnp.maximum(qi - 2, 0)
    has_prev = qi >= 1
    has_prev2 = qi >= 2
    groups = range(G_NSA)
    q_wide = _dot(q_ref[0], spread_ref[...])
    qa = [jnp.concatenate([(q_wide[:, (g * HPG + h) * LANE:(g * HPG + h + 1) * LANE]
                            + neg_ref[0, g].astype(F32)).astype(BF16)
                           for h in range(HPG)], axis=0) for g in groups]

    def sel_logits(j, g):
        start = pl.multiple_of(j * tq, tq)
        return _dot_nt(kaug_scr[g, pl.ds(start, tq), :], qa[g])

    def win_logits(j, g):
        start = pl.multiple_of(j * tq, tq)
        return _dot_nt(kwin_scr[g, pl.ds(start, tq), :], qa[g])

    carries = tuple(_attn_init_t(HEAD_DIM, cols) for _ in range(2 * G_NSA))
    ss = ([jnp.where(causal, sel_logits(qi, g) + bias_ref[g, 0], NEG_INF) for g in groups]
          + [jnp.where(causal, win_logits(qi, g) + bias_ref[g, 0], NEG_INF) for g in groups])
    carries = _online_update_multi(carries, ss, [vst_scr[g, qi] for g in groups] + [vwt_scr[g, qi] for g in groups])
    ss = ([jnp.where(has_prev, sel_logits(prev, g) + bias_ref[g, 1], NEG_INF) for g in groups]
          + [jnp.where(has_prev, win_logits(prev, g) + bias_ref[g, 1], NEG_INF) for g in groups])
    carries = _online_update_multi(carries, ss,
                                   [vst_scr[g, prev] for g in groups] + [vwt_scr[g, prev] for g in groups])
    ss = [jnp.where(has_prev2 & in_window, win_logits(prev2, g), NEG_INF) for g in groups]
    win = _online_update_multi(carries[G_NSA:], ss, [vwt_scr[g, prev2] for g in groups])

    n_far = prev

    def odd_tile(c):
        return _online_update_multi(c, [sel_logits(prev2, g) for g in groups], [vst_scr[g, prev2] for g in groups])

    sel = lax.cond((n_far % 2) == 1, odd_tile, lambda c: c, carries[:G_NSA])

    def body(j, c):
        start = pl.multiple_of(j * 2 * tq, 2 * tq)
        ss = [_dot_nt(kaug_scr[g, pl.ds(start, 2 * tq), :], qa[g]) for g in groups]
        return _online_update_multi(c, ss, [(vst_scr[g, 2 * j], vst_scr[g, 2 * j + 1]) for g in groups])

    sel = lax.fori_loop(0, n_far // 2, body, sel)

    gates = _sigmoid(misc_ref[0].T)
    outs = []
    for g in groups:
        o_slc = sel[g][2] / sel[g][1]
        o_win = win[g][2] / win[g][1]
        for h in range(HPG):
            hh = g * HPG + h
            base = MISC_GATE + hh * 3
            cs = slice(h * tq, (h + 1) * tq)
            outs.append(gates[base:base + 1] * ocmp_ref[0, hh] + gates[base + 1:base + 2] * o_slc[:, cs]
                        + gates[base + 2:base + 3] * o_win[:, cs])
    for pair in range(H_NSA // 2):
        both = jnp.concatenate([outs[2 * pair], outs[2 * pair + 1]], axis=0)
        o_ref[0, :, pair * LANE:(pair + 1) * LANE] = both.T.astype(o_ref.dtype)


def _nsa_attn_call(pb3, neg, ocmp, pf3, bias_n, spread, tq):
    B, S, _ = pb3.shape
    kv = lambda col: pl.BlockSpec((1, S, LANE), lambda b, i: (b, 0, col // LANE))
    return pl.pallas_call(
        functools.partial(_nsa_attn_kernel, tq=tq),
        out_shape=jax.ShapeDtypeStruct((B, S, C_NSA), BF16),
        grid=(B, S // tq),
        in_specs=[pl.BlockSpec((1, tq, C_NSA), lambda b, i: (b, i, PB_NQ // C_NSA)),
                  pl.BlockSpec((1, G_NSA, tq, LANE), lambda b, i: (b, 0, i, 0)),
                  kv(PB_KS), kv(PB_VS), kv(PB_KW), kv(PB_VW),
                  pl.BlockSpec((1, H_NSA, HEAD_DIM, tq), lambda b, i: (b, 0, 0, i)),
                  pl.BlockSpec((1, tq, LANE), lambda b, i: (b, i, PF_MISC // LANE)),
                  pl.BlockSpec(bias_n.shape, lambda b, i: (0, 0, 0, 0)),
                  pl.BlockSpec(spread.shape, lambda b, i: (0, 0))],
        out_specs=pl.BlockSpec((1, tq, C_NSA), lambda b, i: (b, i, 0)),
        scratch_shapes=[pltpu.VMEM((G_NSA, S, LANE), BF16),
                        pltpu.VMEM((G_NSA, S, LANE), BF16),
                        pltpu.VMEM((G_NSA, S // tq, HEAD_DIM, tq), BF16),
                        pltpu.VMEM((G_NSA, S // tq, HEAD_DIM, tq), BF16)],
        compiler_params=_cparams(("arbitrary", "arbitrary")),
        name="nsa_attn",
    )(pb3, neg, pb3, pb3, pb3, pb3, ocmp, pf3, bias_n, spread)


def _outproj_kernel(x_ref, ya_ref, yb_ref, yc_ref, w_ref, gm_ref, g_ref, o_ref):
    y_cat = jnp.concatenate([ya_ref[...], yb_ref[...], yc_ref[...]], axis=1)
    o_ref[...] = x_ref[...] + gm_ref[0] * _rmsnorm(_dot(y_cat, w_ref[...]), g_ref[...])


def _outproj_call(x2d, ya, yb, yc, w, gm, g, S, tm=TM_PROJ):
    N, D = x2d.shape
    per = S // tm
    rows = lambda a: pl.BlockSpec((tm, a.shape[1]), lambda i: (i, 0))
    return pl.pallas_call(
        _outproj_kernel,
        out_shape=jax.ShapeDtypeStruct((N, D), F32),
        grid=(N // tm,),
        in_specs=[rows(x2d), rows(ya), rows(yb), rows(yc),
                  pl.BlockSpec(w.shape, lambda i: (0, 0)),
                  pl.BlockSpec((1, 1, D), lambda i: (i // per, 0, 0)),
                  pl.BlockSpec((1, D), lambda i: (0, 0))],
        out_specs=pl.BlockSpec((tm, D), lambda i: (i, 0)),
        compiler_params=_cparams(("arbitrary",)),
        name="out_proj",
    )(x2d, ya, yb, yc, w, gm, g.reshape(1, D))


def _ffn_kernel(x_ref, halo_ref, sc_ref, sh_ref, gf_ref, g2_ref, g3_ref, wg_ref, wv_ref,
                cwg_ref, cwv_ref, cbg_ref, cbv_ref, wd_ref, o_ref, h_scr, acc_scr, *, tm, rows_per_seq):
    i = pl.program_id(0)
    f = pl.program_id(1)

    @pl.when(f == 0)
    def _():
        xe = jnp.concatenate([halo_ref[...], x_ref[...]], axis=0)
        h = _rmsnorm(xe, g2_ref[...]) * (1.0 + sc_ref[0]) + sh_ref[0]
        row = lax.broadcasted_iota(jnp.int32, (tm + 8, 1), 0)
        first = (i * tm) % rows_per_seq == 0
        h_scr[...] = jnp.where((row < 8) & first, 0.0, h).astype(BF16)
        acc_scr[...] = jnp.zeros_like(acc_scr)

    h = h_scr[...]

    def conv(w_ref, cw_ref, cb_ref):
        u = _dot(h, w_ref[...])
        y = (cw_ref[2:3] * u + cw_ref[1:2] * pltpu.roll(u, 1, 0) + cw_ref[0:1] * pltpu.roll(u, 2, 0)
             + cb_ref[...])
        return y[8:]

    gate = conv(wg_ref, cwg_ref, cbg_ref)
    val = conv(wv_ref, cwv_ref, cbv_ref)
    act = (gate * _sigmoid(gate) * val).astype(BF16)
    acc_scr[...] += _dot(act, wd_ref[...])

    @pl.when(f == pl.num_programs(1) - 1)
    def _():
        o_ref[...] = x_ref[...] + gf_ref[0] * _rmsnorm(acc_scr[...], g3_ref[...])


def _ffn_call(x2d, sc, sh, gf, g2, g3, w_up, conv_w, conv_b, w_down, S, tm=TM_FFN, tf=TF_FFN):
    N, D = x2d.shape
    F = w_down.shape[0]
    nf = F // tf
    per = S // tm
    mod = pl.BlockSpec((1, 1, D), lambda i, f: (i // per, 0, 0))
    vec = pl.BlockSpec((1, D), lambda i, f: (0, 0))
    cb = conv_b.reshape(1, 2 * F)
    return pl.pallas_call(
        functools.partial(_ffn_kernel, tm=tm, rows_per_seq=S),
        out_shape=jax.ShapeDtypeStruct((N, D), F32),
        grid=(N // tm, nf),
        in_specs=[pl.BlockSpec((tm, D), lambda i, f: (i, 0)),
                  pl.BlockSpec((8, D), lambda i, f: (jnp.maximum(i * (tm // 8) - 1, 0), 0)),
                  mod, mod, mod, vec, vec,
                  pl.BlockSpec((D, tf), lambda i, f: (0, f)),
                  pl.BlockSpec((D, tf), lambda i, f: (0, nf + f)),
                  pl.BlockSpec((CONV_W, tf), lambda i, f: (0, f)),
                  pl.BlockSpec((CONV_W, tf), lambda i, f: (0, nf + f)),
                  pl.BlockSpec((1, tf), lambda i, f: (0, f)),
                  pl.BlockSpec((1, tf), lambda i, f: (0, nf + f)),
                  pl.BlockSpec((tf, D), lambda i, f: (f, 0))],
        out_specs=pl.BlockSpec((tm, D), lambda i, f: (i, 0)),
        scratch_shapes=[pltpu.VMEM((tm + 8, D), BF16), pltpu.VMEM((tm, D), F32)],
        compiler_params=_cparams(("arbitrary", "arbitrary")),
        name="conv_ffn",
    )(x2d, x2d, sc, sh, gf, g2.reshape(1, D), g3.reshape(1, D), w_up, w_up, conv_w, conv_w, cb, cb, w_down)


def _column_maps():
    n_rwkv = 3 * C_RWKV + R_DECAY + R_AAA + R_GATE
    n_fox = 3 * C_FOX + H_FOX
    fox0 = n_rwkv
    nsa0 = n_rwkv + n_fox
    pf = np.full(PF_COLS, -1, np.int64)
    pf[PF_R:PF_R + 3 * C_RWKV] = np.arange(3 * C_RWKV)
    assert (LORA_A, LORA_G, LANE) == (R_DECAY, R_DECAY + R_AAA, R_DECAY + R_AAA + R_GATE)
    pf[PF_LORA:PF_LORA + LANE] = 3 * C_RWKV + np.arange(LANE)
    pf[PF_MISC + MISC_F:PF_MISC + MISC_F + H_FOX] = fox0 + 3 * C_FOX + np.arange(H_FOX)
    nsa_gate0 = nsa0 + C_NSA + 6 * G_NSA * HEAD_DIM
    pf[PF_MISC + MISC_GATE:PF_MISC + MISC_GATE + 3 * H_NSA] = nsa_gate0 + np.arange(3 * H_NSA)

    pb = np.full(PB_COLS, -1, np.int64)
    scale = np.ones(PB_COLS, np.float32)
    d = np.arange(HEAD_DIM)
    del d
    pb[PB_FQ:PB_FQ + C_FOX] = fox0 + np.arange(C_FOX)
    scale[PB_FQ:PB_FQ + C_FOX] = HEAD_DIM ** -0.5
    pb[PB_FK:PB_FK + C_FOX] = fox0 + C_FOX + np.arange(C_FOX)
    pb[PB_FV:PB_FV + C_FOX] = fox0 + 2 * C_FOX + np.arange(C_FOX)
    pb[PB_NQ:PB_NQ + C_NSA] = nsa0 + np.arange(C_NSA)
    scale[PB_NQ:PB_NQ + C_NSA] = HEAD_DIM ** -0.5
    ckv = G_NSA * HEAD_DIM
    kc0 = nsa0 + C_NSA
    for n, base in enumerate((PB_KC, PB_VC, PB_KS, PB_VS, PB_KW, PB_VW)):
        pb[base:base + ckv] = kc0 + n * ckv + np.arange(ckv)
    return pf, pb, scale


def _lora_rows(w, lane0):
    out = jnp.zeros((LANE, w.shape[1]), w.dtype)
    return out.at[lane0:lane0 + w.shape[0]].set(w).astype(BF16)


def _compress_w1(w1):
    hid = w1.shape[1]
    w = w1.reshape(2, D_CMP, HEAD_DIM, hid)
    out = jnp.zeros((G_NSA, 2, D_CMP, G_NSA, HEAD_DIM, hid), w1.dtype)
    for g in range(G_NSA):
        out = out.at[g, :, :, g].set(w)
    return out.reshape(G_NSA, 2, D_CMP * G_NSA * HEAD_DIM, hid).astype(BF16)


def _compress_pe(pe):
    half = pe.reshape(2, 1, D_CMP, 1, HEAD_DIM)
    return jnp.broadcast_to(half, (2, 8, D_CMP, G_NSA, HEAD_DIM)).reshape(2, 8, D_CMP * G_NSA * HEAD_DIM)


def _fox_place():
    place = np.zeros((3, LANE, H_FOX * LANE), np.float32)
    for j in range(3):
        for h in range(H_FOX):
            place[j, MISC_F + h, h * LANE + HEAD_DIM + j] = 1.0
    return jnp.asarray(place, BF16)


def _fox_spread():
    spread = np.zeros((C_FOX, H_FOX * LANE), np.float32)
    ones_row = np.zeros((1, H_FOX * LANE), np.float32)
    d = np.arange(HEAD_DIM)
    for h in range(H_FOX):
        spread[h * HEAD_DIM + d, h * LANE + d] = 1.0
        ones_row[0, h * LANE + HEAD_DIM:h * LANE + HEAD_DIM + 3] = 1.0
    return jnp.asarray(spread, BF16), jnp.asarray(ones_row, F32)


def _overlap_t(S):
    nc = S // D_CMP
    nsb = S // L_SLC
    c0 = np.arange(nc) * D_CMP
    c1 = c0 + L_CMP - 1
    s0 = np.arange(nsb) * L_SLC
    ov = (c0[None, :] <= s0[:, None] + L_SLC - 1) & (c1[None, :] >= s0[:, None])
    ov[:, nc - 1] = False
    return jnp.asarray(ov.astype(np.float32), BF16)


def kernel(x, c, ada_w, ada_b, norm_g, w_in, rwkv_mu, rwkv_w0, rwkv_w_up, rwkv_a0, rwkv_a_up, rwkv_g_up, rwkv_k_k, rwkv_k_a, rwkv_r_k, rwkv_ln_w, rwkv_ln_b, fox_b_f, nsa_pe_k, nsa_pe_v, nsa_ck_w1, nsa_ck_w2, nsa_cv_w1, nsa_cv_w2, rel_bias, w_out, ffn_up, ffn_conv_w, ffn_conv_b, ffn_down):
    B, S, D = x.shape
    L = w_in.shape[0]
    tq_sel, tq_nsa = TQ_SEL, TQ_NSA
    assert S % TM_FFN == 0 and S // L_SLC <= HEAD_DIM and D == 1024

    pf_idx, pb_idx, pb_scale = _column_maps()
    w_ext = jnp.concatenate([w_in, jnp.zeros((L, D, 1), w_in.dtype)], axis=2)
    w_pf = jnp.take(w_ext, jnp.asarray(pf_idx), axis=2).astype(BF16)
    w_pb = (jnp.take(w_ext, jnp.asarray(pb_idx), axis=2) * pb_scale).astype(BF16)
    mu_ext = jnp.concatenate([rwkv_mu, jnp.zeros((L, 1), F32)], axis=1)
    mu_pf = jnp.take(mu_ext, jnp.asarray(pf_idx[:PF_RWKV]), axis=1)

    mod_all = _mod_call(c, ada_w, ada_b).reshape(L, B, 6, 1, D)
    bias_c, bias_n = _bias_tables(rel_bias, S, tq_nsa)
    place = _fox_place()
    spread, ones_row = _fox_spread()
    overlap_t = _overlap_t(S)

    x2d = x.reshape(B * S, D)
    for l in range(L):
        sh_m, sc_m, g_m, sh_f, sc_f, g_f = (mod_all[l, :, j] for j in range(6))
        pf, pb = _inproj_call(x2d, sc_m, sh_m, norm_g[l, 0], w_pf[l], w_pb[l], S)
        pf3 = pf.reshape(B, S, PF_COLS)
        pb3 = pb.reshape(B, S, PB_COLS)

        ry, mg, bonus, gate = _rwkv_local_call(
            pf, mu_pf[l], rwkv_w0[l], _lora_rows(rwkv_w_up[l], LORA_W), rwkv_a0[l],
            _lora_rows(rwkv_a_up[l], LORA_A), _lora_rows(rwkv_g_up[l], LORA_G),
            rwkv_k_k[l], rwkv_k_a[l], rwkv_r_k[l], S)
        ya = _rwkv_scan_call(ry, mg, bonus, gate, rwkv_ln_w[l], rwkv_ln_b[l], B, S)

        b_f_row = jnp.zeros((1, LANE), F32).at[0, MISC_F:MISC_F + H_FOX].set(fox_b_f[l])
        qaug, kaug, fox_vt = _fox_prep_call(pf3, pb3, b_f_row, place, spread, ones_row)
        yb = _fox_attn_call(qaug, kaug, fox_vt)

        kc_flat = pb3[:, :, PB_KC:PB_KC + LANE].reshape(B, S // D_CMP, D_CMP * LANE)
        vc_flat = pb3[:, :, PB_VC:PB_VC + LANE].reshape(B, S // D_CMP, D_CMP * LANE)
        kcmp, vcmpT = _nsa_compress_call(
            kc_flat, vc_flat, _compress_w1(nsa_ck_w1[l]), _compress_w1(nsa_cv_w1[l]),
            nsa_ck_w2[l].astype(BF16), nsa_cv_w2[l].T.astype(BF16),
            _compress_pe(nsa_pe_k[l]), _compress_pe(nsa_pe_v[l]))
        ocmp, neg = _nsa_select_call(pb3, kcmp, vcmpT, bias_c, overlap_t, tq_sel)
        yc = _nsa_attn_call(pb3, neg, ocmp, pf3, bias_n, spread, tq_nsa)

        x2d = _outproj_call(x2d, ya.reshape(B * S, C_RWKV), yb.reshape(B * S, C_FOX),
                            yc.reshape(B * S, C_NSA), w_out[l].astype(BF16), g_m, norm_g[l, 1], S)
        x2d = _ffn_call(x2d, sc_f, sh_f, g_f, norm_g[l, 2], norm_g[l, 3], ffn_up[l].astype(BF16),
                        ffn_conv_w[l], ffn_conv_b[l], ffn_down[l].astype(BF16), S)
    return x2d.reshape(B, S, D)
```
